```python
import jax
import jax.numpy as jnp
from jax import lax
import numpy as np

D_MODEL = 1024
BATCH = 8
SEQ = 4096
DEPTH = 2
DEC_BATCH = 32
DEC_SEQ = 32
PAST_LEN = 2048

CHUNK = 64
D_MIX = 2 * D_MODEL
N_MIX_GROUPS = 4
D_GROUP = D_MIX // N_MIX_GROUPS
CONV_W = 4
LRU_BLOCK = 64
LRU_BLOCKS = D_GROUP // LRU_BLOCK
LRU_C = 8.0
RWKV_HEAD = 64
RWKV_HEADS = D_GROUP // RWKV_HEAD
RWKV_RANK_W = 64
RWKV_RANK_A = 64
RWKV_SHIFT_W = 3 * D_GROUP + RWKV_RANK_W + RWKV_RANK_A
RWKV_DECAY_SCALE = 0.6065306597126334
RWKV_LN_EPS = 64e-5
SSD_HEAD = 64
SSD_HEADS = D_GROUP // SSD_HEAD
SSD_STATE = 64
SSD_GROUPS = 2
SSD_CONV_CH = D_GROUP + 2 * SSD_GROUPS * SSD_STATE
HGRN_HEADS = 8
HGRN_DK = D_GROUP // HGRN_HEADS
HGRN_DV = D_GROUP // HGRN_HEADS
MEM_LEN = 256
MEM_HEADS = 4
MEM_HEAD_DIM = D_MODEL // MEM_HEADS
NORM_EPS = 1e-6
SPLITS = (D_GROUP, D_GROUP, RWKV_SHIFT_W, D_GROUP, SSD_CONV_CH, SSD_HEADS, D_GROUP, D_GROUP, D_GROUP, D_GROUP, D_GROUP)
D_IN_PROJ = 8 * D_GROUP + RWKV_SHIFT_W + SSD_CONV_CH + SSD_HEADS

kernel_name = 'hybrid_stream_encoder_step'


def rms_norm(x, g):
    xf = x.astype(jnp.float32)
    y = xf * lax.rsqrt(jnp.mean(xf * xf, axis=-1, keepdims=True) + NORM_EPS)
    return (y * g.astype(jnp.float32)).astype(x.dtype)


def split_last(t, sizes):
    out, s = [], 0
    for n in sizes:
        out.append(t[..., s:s + n])
        s += n
    return out


def chunk_len(T):
    return CHUNK if T % CHUNK == 0 else T


def causal_conv(x, buf, w, b):
    T = x.shape[1]
    xp = jnp.concatenate([buf.astype(x.dtype), x], axis=1)
    y = b
    for j in range(CONV_W):
        y = y + xp[:, j:j + T] * w[j]
    return y, xp[:, T:]


def rg_lru(x, h0, w_r, b_r, w_i, b_i, lam):
    Bsz, T, C = x.shape
    xb = x.reshape(Bsz, T, LRU_BLOCKS, LRU_BLOCK)
    r = jax.nn.sigmoid(jnp.einsum('btnd,nde->btne', xb, w_r).reshape(Bsz, T, C) + b_r)
    i = jax.nn.sigmoid(jnp.einsum('btnd,nde->btne', xb, w_i).reshape(Bsz, T, C) + b_i)
    log_a = -LRU_C * r * jax.nn.softplus(-lam)
    a = jnp.exp(log_a)
    u = jnp.sqrt(-jnp.expm1(2.0 * log_a)) * (i * x)
    u = u.at[:, 0].add(a[:, 0] * h0)

    def combine(left, right):
        return (left[0] * right[0], right[0] * left[1] + right[1])

    _, h = lax.associative_scan(combine, (a, u), axis=1)
    return h, h[:, -1]


def rwkv7_mix(pb, prev, s0, mu, w0, w_up, a0, a_up, k_k, k_a, r_k, ln_g, ln_b):
    Bsz, T, _ = pb.shape
    shifted = jnp.concatenate([prev[:, None, :].astype(pb.dtype), pb[:, :-1]], axis=1)
    xm = pb + (shifted - pb) * mu
    r, k, v, wl, al = split_last(xm, (D_GROUP, D_GROUP, D_GROUP, RWKV_RANK_W, RWKV_RANK_A))
    decay = jnp.exp(-RWKV_DECAY_SCALE * jax.nn.sigmoid(w0 + jnp.matmul(jnp.tanh(wl), w_up)))
    a = jax.nn.sigmoid(a0 + jnp.matmul(al, a_up))
    hd = lambda t: t.reshape(Bsz, T, RWKV_HEADS, RWKV_HEAD)
    kk = hd(k * k_k)
    kk = kk * lax.rsqrt(jnp.sum(kk * kk, axis=-1, keepdims=True) + 1e-12)
    k = k * (1.0 + (a - 1.0) * k_a)
    r, k, v, a, decay = hd(r), hd(k), hd(v), hd(a), hd(decay)

    def step(S, inp):
        r_t, w_t, k_t, v_t, kk_t, a_t = inp
        sa = jnp.einsum('bhvk,bhk->bhv', S, -kk_t)
        S = S * w_t[:, :, None, :] + sa[..., None] * (kk_t * a_t)[:, :, None, :] + v_t[..., None] * k_t[:, :, None, :]
        return S, jnp.einsum('bhvk,bhk->bhv', S, r_t)

    tm = lambda t: jnp.moveaxis(t, 1, 0)
    sT, y = lax.scan(step, s0, (tm(r), tm(decay), tm(k), tm(v), tm(kk), tm(a)))
    y = jnp.moveaxis(y, 0, 1)
    m = jnp.mean(y, axis=-1, keepdims=True)
    var = jnp.mean(jnp.square(y - m), axis=-1, keepdims=True)
    y = ((y - m) * lax.rsqrt(var + RWKV_LN_EPS)).reshape(Bsz, T, D_GROUP) * ln_g + ln_b
    bonus = jnp.sum(r * k * r_k, axis=-1, keepdims=True) * v
    y = y + bonus.reshape(Bsz, T, D_GROUP)
    return y, pb[:, -1], sT


def ssd_scan(x, dt, A, Bm, Cm, h0):
    Bsz, T, H, P = x.shape
    L = chunk_len(T)
    nc = T // L
    rep = H // SSD_GROUPS
    Bh = jnp.repeat(Bm, rep, axis=2)
    Ch = jnp.repeat(Cm, rep, axis=2)
    ch = lambda t: t.reshape((Bsz, nc, L) + t.shape[2:])
    x, dt, Bh, Ch = ch(x), ch(dt), ch(Bh), ch(Cm if False else Ch)
    acs = jnp.cumsum(dt * A, axis=2)
    mask = jnp.tril(jnp.ones((L, L), dtype=bool))[None, None, :, :, None]
    seg = acs[:, :, :, None, :] - acs[:, :, None, :, :]
    decay = jnp.exp(jnp.where(mask, seg, -jnp.inf))
    scores = jnp.einsum('bclhn,bcshn->bclsh', Ch, Bh) * decay * dt[:, :, None, :, :]
    y = jnp.einsum('bclsh,bcshp->bclhp', scores, x)
    dlast = jnp.exp(acs[:, :, -1:, :] - acs)
    st = jnp.einsum('bclhn,bclh,bclhp->bchpn', Bh, dlast * dt, x)
    tot = jnp.exp(acs[:, :, -1, :])

    def step(h, inp):
        s_c, g_c = inp
        return h * g_c[:, :, None, None] + s_c, h

    hT, hin = lax.scan(step, h0, (jnp.moveaxis(st, 1, 0), jnp.moveaxis(tot, 1, 0)))
    hin = jnp.moveaxis(hin, 0, 1)
    y = y + jnp.einsum('bclhn,bchpn,bclh->bclhp', Ch, hin, jnp.exp(acs))
    return y.reshape(Bsz, T, H, P), hT


def hgrn2_chunked(q, logf, v, s0):
    Bsz, T, H, K = q.shape
    L = chunk_len(T)
    nc = T // L
    k = -jnp.expm1(logf)
    to_c = lambda t: jnp.moveaxis(t.reshape(Bsz, nc, L, H, t.shape[-1]), 1, 0)
    mask = jnp.tril(jnp.ones((L, L), dtype=bool))[None, :, :, None, None]

    def step(S, inp):
        qc, lfc, kc, vc = inp
        b = jnp.cumsum(lfc, axis=1)
        o = jnp.einsum('blhk,bhkv->blhv', qc * jnp.exp(b), S)
        dec = jnp.exp(jnp.where(mask, b[:, :, None] - b[:, None, :], -jnp.inf))
        att = jnp.einsum('bthk,bshk,btshk->btsh', qc, kc, dec)
        o = o + jnp.einsum('btsh,bshv->bthv', att, vc)
        bl = b[:, -1]
        S = S * jnp.exp(bl)[..., None] + jnp.einsum('bshk,bshv->bhkv', kc * jnp.exp(bl[:, None] - b), vc)
        return S, o

    sT, o = lax.scan(step, s0, (to_c(q), to_c(logf), to_c(k), to_c(v)))
    return jnp.moveaxis(o, 0, 1).reshape(Bsz, T, H, v.shape[-1]), sT


def mem_kv(mem, g, w_k, w_v):
    m = rms_norm(mem, g)
    Bsz = mem.shape[0]
    k = jnp.matmul(m, w_k).reshape(Bsz, MEM_LEN, MEM_HEADS, MEM_HEAD_DIM)
    v = jnp.matmul(m, w_v).reshape(Bsz, MEM_LEN, MEM_HEADS, MEM_HEAD_DIM)
    return k, v


def cross_attn(h, mk, mv, w_q, w_o):
    Bsz, T, _ = h.shape
    q = jnp.matmul(h, w_q).reshape(Bsz, T, MEM_HEADS, MEM_HEAD_DIM)
    s = jnp.einsum('bthd,bmhd->bhtm', q, mk).astype(jnp.float32) * (MEM_HEAD_DIM ** -0.5)
    p = jax.nn.softmax(s, axis=-1).astype(mv.dtype)
    o = jnp.einsum('bhtm,bmhd->bthd', p, mv).reshape(Bsz, T, D_MODEL)
    return jnp.matmul(o.astype(h.dtype), w_o)


def setup_inputs(seed: int = 0) -> dict:
    key = jax.random.key(seed)
    ks = iter(jax.random.split(key, 64))
    f32 = jnp.float32
    nrm = lambda shape, s: jax.random.normal(next(ks), shape, f32) * s
    uni = lambda shape, lo, hi: jax.random.uniform(next(ks), shape, f32, lo, hi)
    gain = lambda shape: 1.0 + nrm(shape, 0.02)
    u_lru = uni((DEPTH, D_GROUP), 0.9, 0.999)
    a_lru = u_lru ** (1.0 / LRU_C)
    dt0 = jnp.exp(uni((DEPTH, SSD_HEADS), float(np.log(1e-3)), float(np.log(1e-1))))
    return {
        'x_prompt': nrm((BATCH, SEQ, D_MODEL), 1.0),
        'x_sample': nrm((DEC_BATCH, DEC_SEQ, D_MODEL), 1.0),
        'state_lru_conv': nrm((DEPTH, DEC_BATCH, CONV_W - 1, D_GROUP), 1.0),
        'state_lru_h': nrm((DEPTH, DEC_BATCH, D_GROUP), 0.5),
        'state_rwkv_shift': nrm((DEPTH, DEC_BATCH, RWKV_SHIFT_W), 1.0),
        'state_rwkv_wkv': nrm((DEPTH, DEC_BATCH, RWKV_HEADS, RWKV_HEAD, RWKV_HEAD), 0.1),
        'state_ssd_conv': nrm((DEPTH, DEC_BATCH, CONV_W - 1, SSD_CONV_CH), 1.0),
        'state_ssd_h': nrm((DEPTH, DEC_BATCH, SSD_HEADS, SSD_HEAD, SSD_STATE), 0.1),
        'state_hgrn_s': nrm((DEPTH, DEC_BATCH, HGRN_HEADS, HGRN_DK, HGRN_DV), 0.1),
        'cache_mem_k': nrm((DEPTH, DEC_BATCH, MEM_LEN, MEM_HEADS, MEM_HEAD_DIM), 1.0),
        'cache_mem_v': nrm((DEPTH, DEC_BATCH, MEM_LEN, MEM_HEADS, MEM_HEAD_DIM), 1.0),
        'mem_prompt': nrm((BATCH, MEM_LEN, D_MODEL), 1.0),
        'g_pre': gain((DEPTH, D_MODEL)),
        'g_post': gain((DEPTH, D_MODEL)),
        'g_pre_x': gain((DEPTH, D_MODEL)),
        'g_post_x': gain((DEPTH, D_MODEL)),
        'w_in': nrm((DEPTH, D_MODEL, D_IN_PROJ), D_MODEL ** -0.5),
        'w_out': nrm((DEPTH, D_MIX, D_MODEL), D_MIX ** -0.5),
        'lru_conv_w': nrm((DEPTH, CONV_W, D_GROUP), CONV_W ** -0.5),
        'lru_conv_b': nrm((DEPTH, D_GROUP), 0.01),
        'lru_w_r': nrm((DEPTH, LRU_BLOCKS, LRU_BLOCK, LRU_BLOCK), LRU_BLOCK ** -0.5),
        'lru_b_r': nrm((DEPTH, D_GROUP), 0.01),
        'lru_w_i': nrm((DEPTH, LRU_BLOCKS, LRU_BLOCK, LRU_BLOCK), LRU_BLOCK ** -0.5),
        'lru_b_i': nrm((DEPTH, D_GROUP), 0.01),
        'lru_lambda': jnp.log(a_lru) - jnp.log1p(-a_lru),
        'rwkv_mu': uni((DEPTH, RWKV_SHIFT_W), 0.0, 1.0),
        'rwkv_w0': nrm((DEPTH, D_GROUP), 1.0),
        'rwkv_w_up': nrm((DEPTH, RWKV_RANK_W, D_GROUP), RWKV_RANK_W ** -0.5),
        'rwkv_a0': nrm((DEPTH, D_GROUP), 0.5),
        'rwkv_a_up': nrm((DEPTH, RWKV_RANK_A, D_GROUP), RWKV_RANK_A ** -0.5),
        'rwkv_k_k': 0.85 + nrm((DEPTH, D_GROUP), 0.05),
        'rwkv_k_a': 1.0 + nrm((DEPTH, D_GROUP), 0.05),
        'rwkv_r_k': nrm((DEPTH, RWKV_HEADS, RWKV_HEAD), 0.1),
        'rwkv_ln_g': gain((DEPTH, D_GROUP)),
        'rwkv_ln_b': nrm((DEPTH, D_GROUP), 0.01),
        'ssd_conv_w': nrm((DEPTH, CONV_W, SSD_CONV_CH), CONV_W ** -0.5),
        'ssd_conv_b': nrm((DEPTH, SSD_CONV_CH), 0.01),
        'ssd_dt_bias': dt0 + jnp.log(-jnp.expm1(-dt0)),
        'ssd_a_log': jnp.log(uni((DEPTH, SSD_HEADS), 1.0, 16.0)),
        'ssd_d': 1.0 + nrm((DEPTH, SSD_HEADS), 0.1),
        'ssd_norm_g': gain((DEPTH, D_GROUP)),
        'hgrn_lb_logits': nrm((DEPTH, D_GROUP), 1.0),
        'hgrn_norm_g': gain((DEPTH, D_GROUP)),
        'mem_g': gain((DEPTH, D_MODEL)),
        'mem_w_q': nrm((DEPTH, D_MODEL, D_MODEL), D_MODEL ** -0.5),
        'mem_w_k': nrm((DEPTH, D_MODEL, D_MODEL), D_MODEL ** -0.5),
        'mem_w_v': nrm((DEPTH, D_MODEL, D_MODEL), D_MODEL ** -0.5),
        'mem_w_o': nrm((DEPTH, D_MODEL, D_MODEL), D_MODEL ** -0.5),
    }


def reference(x_prompt, x_sample, state_lru_conv, state_lru_h, state_rwkv_shift, state_rwkv_wkv,
              state_ssd_conv, state_ssd_h, state_hgrn_s, cache_mem_k, cache_mem_v, mem_prompt,
              g_pre, g_post, g_pre_x, g_post_x, w_in, w_out,
              lru_conv_w, lru_conv_b, lru_w_r, lru_b_r, lru_w_i, lru_b_i, lru_lambda,
              rwkv_mu, rwkv_w0, rwkv_w_up, rwkv_a0, rwkv_a_up, rwkv_k_k, rwkv_k_a, rwkv_r_k, rwkv_ln_g, rwkv_ln_b,
              ssd_conv_w, ssd_conv_b, ssd_dt_bias, ssd_a_log, ssd_d, ssd_norm_g,
              hgrn_lb_logits, hgrn_norm_g,
              mem_g, mem_w_q, mem_w_k, mem_w_v, mem_w_o):
    f32 = jnp.float32
    lb_all = jnp.cumsum(jax.nn.softmax(hgrn_lb_logits.astype(f32), axis=0), axis=0)
    lb_all = lb_all - lb_all[0]

    def run(x, mk, mv, conv_a, h_a, shift_b, wkv_b, conv_c, h_c, s_d):
        Bsz, T = x.shape[0], x.shape[1]
        n_conv_a, n_h_a, n_shift_b, n_wkv_b, n_conv_c, n_h_c, n_s_d = [], [], [], [], [], [], []
        for l in range(DEPTH):
            h = rms_norm(x, g_pre[l])
            p = jnp.matmul(h, w_in[l]).astype(f32)
            xa, ga, pb, gb, xbc, dtr, z, qd, fd, idd, gd = split_last(p, SPLITS)
            xa_c, nca = causal_conv(xa, conv_a[l], lru_conv_w[l], lru_conv_b[l])
            ya, nha = rg_lru(xa_c, h_a[l].astype(f32), lru_w_r[l], lru_b_r[l], lru_w_i[l], lru_b_i[l], lru_lambda[l])
            ya = ya * jax.nn.silu(ga)
            yb, nsb, nwb = rwkv7_mix(pb, shift_b[l].astype(f32), wkv_b[l].astype(f32), rwkv_mu[l], rwkv_w0[l],
                                     rwkv_w_up[l], rwkv_a0[l], rwkv_a_up[l], rwkv_k_k[l], rwkv_k_a[l],
                                     rwkv_r_k[l], rwkv_ln_g[l], rwkv_ln_b[l])
            yb = yb * jax.nn.silu(gb)
            xbc_c, ncc = causal_conv(xbc, conv_c[l], ssd_conv_w[l], ssd_conv_b[l])
            xbc_c = jax.nn.silu(xbc_c)
            xs, bm, cm = split_last(xbc_c, (D_GROUP, SSD_GROUPS * SSD_STATE, SSD_GROUPS * SSD_STATE))
            xs = xs.reshape(Bsz, T, SSD_HEADS, SSD_HEAD)
            dt = jax.nn.softplus(dtr + ssd_dt_bias[l])
            yc, nhc = ssd_scan(xs, dt, -jnp.exp(ssd_a_log[l].astype(f32)),
                               bm.reshape(Bsz, T, SSD_GROUPS, SSD_STATE), cm.reshape(Bsz, T, SSD_GROUPS, SSD_STATE),
                               h_c[l].astype(f32))
            yc = (yc + ssd_d[l][:, None] * xs).reshape(Bsz, T, D_GROUP)
            yc = rms_norm(yc * jax.nn.silu(z), ssd_norm_g[l])
            lb = lb_all[l]
            logf = jnp.logaddexp(jnp.log(lb), jnp.log1p(-lb) + jax.nn.log_sigmoid(fd))
            hh = lambda t: t.reshape(Bsz, T, HGRN_HEADS, t.shape[-1] // HGRN_HEADS)
            yd, nsd = hgrn2_chunked(hh(jax.nn.silu(qd)), hh(logf), hh(idd), s_d[l].astype(f32))
            yd = rms_norm(yd, hgrn_norm_g[l].reshape(HGRN_HEADS, HGRN_DV)).reshape(Bsz, T, D_GROUP)
            yd = yd * jax.nn.silu(gd)
            y = jnp.matmul(jnp.concatenate([ya, yb, yc, yd], axis=-1).astype(x.dtype), w_out[l])
            x = x + rms_norm(y, g_post[l])
            hx = rms_norm(x, g_pre_x[l])
            x = x + rms_norm(cross_attn(hx, mk[l], mv[l], mem_w_q[l], mem_w_o[l]), g_post_x[l])
            n_conv_a.append(nca)
            n_h_a.append(nha)
            n_shift_b.append(nsb)
            n_wkv_b.append(nwb)
            n_conv_c.append(ncc)
            n_h_c.append(nhc)
            n_s_d.append(nsd)
        return (x, jnp.stack(n_conv_a), jnp.stack(n_h_a), jnp.stack(n_shift_b), jnp.stack(n_wkv_b),
                jnp.stack(n_conv_c), jnp.stack(n_h_c), jnp.stack(n_s_d))

    Bp = x_prompt.shape[0]
    zeros = lambda *s: jnp.zeros((DEPTH, Bp) + s, f32)
    kv = [mem_kv(mem_prompt, mem_g[l], mem_w_k[l], mem_w_v[l]) for l in range(DEPTH)]
    mem_k_p = jnp.stack([t[0] for t in kv])
    mem_v_p = jnp.stack([t[1] for t in kv])
    (y_prompt, lru_conv_p, lru_h_p, rwkv_shift_p, rwkv_wkv_p, ssd_conv_p, ssd_h_p, hgrn_s_p) = run(
        x_prompt, mem_k_p, mem_v_p,
        zeros(CONV_W - 1, D_GROUP), zeros(D_GROUP), zeros(RWKV_SHIFT_W),
        zeros(RWKV_HEADS, RWKV_HEAD, RWKV_HEAD), zeros(CONV_W - 1, SSD_CONV_CH),
        zeros(SSD_HEADS, SSD_HEAD, SSD_STATE), zeros(HGRN_HEADS, HGRN_DK, HGRN_DV))
    (y_sample, lru_conv_s, lru_h_s, rwkv_shift_s, rwkv_wkv_s, ssd_conv_s, ssd_h_s, hgrn_s_s) = run(
        x_sample, cache_mem_k, cache_mem_v, state_lru_conv, state_lru_h, state_rwkv_shift, state_rwkv_wkv,
        state_ssd_conv, state_ssd_h, state_hgrn_s)
    return (y_prompt, y_sample, lru_conv_p, lru_conv_s, lru_h_p, lru_h_s, rwkv_shift_p, rwkv_shift_s,
            rwkv_wkv_p, rwkv_wkv_s, ssd_conv_p, ssd_conv_s, ssd_h_p, ssd_h_s, hgrn_s_p, hgrn_s_s,
            mem_k_p, mem_v_p)
```

```python
import functools

import numpy as np
import jax
import jax.numpy as jnp
from jax import lax
from jax.experimental import pallas as pl
from jax.experimental.pallas import tpu as pltpu

F32 = jnp.float32
BF16 = jnp.bfloat16
HIGHEST = lax.Precision.HIGHEST

LANES = 128
HEAD = 64
PAIR = 2 * HEAD
SUB = 16
SUB_SHIFT = 4
CHUNK = 64
CONV_W = 4
CONV_PAD = 8
LRU_C = 8.0
RWKV_DECAY_SCALE = 0.6065306597126334
RWKV_LN_EPS = 64e-5
NORM_EPS = 1e-6
VMEM_LIMIT = 56 * 1024 * 1024


def _dot(a, b):
    return jnp.dot(a.astype(BF16), b.astype(BF16), preferred_element_type=F32)


def _dot_nt(a, b):
    return lax.dot_general(a.astype(BF16), b.astype(BF16), (((1,), (1,)), ((), ())),
                           preferred_element_type=F32)


def _dot_tn(a, b):
    return lax.dot_general(a.astype(BF16), b.astype(BF16), (((0,), (0,)), ((), ())),
                           preferred_element_type=F32)


def _dot_hi(a, b):
    return jnp.dot(a, b, preferred_element_type=F32, precision=HIGHEST)


def _dot_nt_hi(a, b):
    return lax.dot_general(a, b, (((1,), (1,)), ((), ())), preferred_element_type=F32,
                           precision=HIGHEST)


def _split3(x):
    hi = x.astype(BF16)
    r1 = x - hi.astype(F32)
    mid = r1.astype(BF16)
    lo = (r1 - mid.astype(F32)).astype(BF16)
    return hi, mid, lo


def _dot01(m01, x):
    hi, mid, lo = _split3(x)
    d = lambda p: jnp.dot(m01, p, preferred_element_type=F32)
    return d(hi) + d(mid) + d(lo)


def _dot01_nt(m01, x):
    hi, mid, lo = _split3(x)
    d = lambda p: lax.dot_general(m01, p, (((1,), (1,)), ((), ())), preferred_element_type=F32)
    return d(hi) + d(mid) + d(lo)


def _rms(x, g):
    return x * lax.rsqrt(jnp.mean(x * x, axis=-1, keepdims=True) + NORM_EPS) * g


def _sigmoid(x):
    return jax.nn.sigmoid(x)


def _silu(x):
    return x * jax.nn.sigmoid(x)


def _softplus(x):
    return jnp.maximum(x, 0.0) + jnp.log1p(jnp.exp(-jnp.abs(x)))


def _neg_expm1(z):
    return -jnp.tanh(0.5 * z) * (jnp.exp(z) + 1.0)


def _head_masks():
    lane = lax.broadcasted_iota(jnp.int32, (1, PAIR), 1)
    return lane < HEAD, lane >= HEAD


def _stack_heads(x, m0, m1):
    return jnp.concatenate([jnp.where(m0, x, 0.0), jnp.where(m1, x, 0.0)], axis=0)


def _bd_mask():
    r = lax.broadcasted_iota(jnp.int32, (PAIR, PAIR), 0)
    c = lax.broadcasted_iota(jnp.int32, (PAIR, PAIR), 1)
    return (r >= HEAD) == (c >= HEAD)


def _cat_masks(L):
    t = lax.broadcasted_iota(jnp.int32, (L, 2 * L), 0)
    c = lax.broadcasted_iota(jnp.int32, (L, 2 * L), 1)
    s = jnp.where(c >= L, c - L, c)
    return s < t, s <= t, c < L


def _cat_to_blockdiag(m, first):
    return jnp.concatenate([jnp.where(first, m, 0.0), jnp.where(first, 0.0, m)], axis=0)


def _tri_inv(nn, L):
    n = nn.shape[0]
    ri = lax.broadcasted_iota(jnp.int32, (n, n), 0)
    ci = lax.broadcasted_iota(jnp.int32, (n, n), 1)
    eye = (ri == ci).astype(F32)
    nd = jnp.where((ri >> SUB_SHIFT) == (ci >> SUB_SHIFT), nn, 0.0)
    no = nn - nd
    td = eye + nd
    p = nd
    k = 2
    while k < SUB:
        p = _dot_hi(p, p)
        td = td + _dot_hi(td, p)
        k *= 2
    w = _dot_hi(td, no)
    tm = eye + w
    pw = w
    k = 2
    while k < L // SUB:
        pw = _dot_hi(pw, pw)
        tm = tm + _dot_hi(tm, pw)
        k *= 2
    return _dot_hi(tm, td)


def _shift_rows(x, d, fill):
    row = lax.broadcasted_iota(jnp.int32, x.shape, 0)
    return jnp.where(row >= d, pltpu.roll(x, d, axis=0), fill)


def _affine_scan(a, u):
    n = a.shape[0]
    d = 1
    while d < n:
        a_s = _shift_rows(a, d, 1.0)
        u_s = _shift_rows(u, d, 0.0)
        u = a * u_s + u
        a = a * a_s
        d *= 2
    return a, u


def _causal_conv(cbuf, b, xa, tb, w_ref, bias):
    cbuf[b, CONV_PAD:CONV_PAD + tb, :] = xa
    y = bias + xa * w_ref[CONV_W - 1:CONV_W, :]
    for j in range(1, CONV_W):
        y = y + cbuf[b, CONV_PAD - j:CONV_PAD - j + tb, :] * w_ref[CONV_W - 1 - j:CONV_W - j, :]
    hist = cbuf[b, tb + CONV_PAD - (CONV_W - 1):tb + CONV_PAD, :]
    cbuf[b, CONV_PAD - (CONV_W - 1):CONV_PAD, :] = hist
    return y


def _first_step():
    return pl.program_id(1) == 0


def _last_step():
    return pl.program_id(1) == pl.num_programs(1) - 1


def _lru_kernel(x_ref, g_ref, w_ref, cw_ref, cb_ref, wg_ref, bg_ref, lam_ref, conv0_ref, h0_ref,
                y_ref, convo_ref, ho_ref, cbuf, hbuf, *, nb, tb):
    c = y_ref.shape[-1]

    @pl.when(_first_step())
    def _():
        cbuf[:, CONV_PAD - (CONV_W - 1):CONV_PAD, :] = conv0_ref[...]
        hbuf[...] = h0_ref[...]

    x = x_ref[...].reshape(nb * tb, x_ref.shape[-1])
    p = _dot(_rms(x, g_ref[...]), w_ref[...])
    xa = p[:, :c]
    ga = p[:, c:]
    xc = jnp.concatenate(
        [_causal_conv(cbuf, b, xa[b * tb:(b + 1) * tb], tb, cw_ref, cb_ref[...]) for b in range(nb)],
        axis=0)
    gates = _dot(xc, wg_ref[...]) + bg_ref[...]
    r = _sigmoid(gates[:, :c])
    i = _sigmoid(gates[:, c:])
    log_a = (-LRU_C) * r * _softplus(-lam_ref[...])
    a = jnp.exp(log_a)
    u = jnp.sqrt(_neg_expm1(2.0 * log_a)) * (i * xc)
    hs = []
    for b in range(nb):
        a_c, h_c = _affine_scan(a[b * tb:(b + 1) * tb], u[b * tb:(b + 1) * tb])
        h = h_c + a_c * hbuf[b]
        hbuf[b] = h[tb - 1:tb, :]
        hs.append(h)
    h = jnp.concatenate(hs, axis=0)
    y_ref[...] = (h * _silu(ga)).reshape(nb, tb, c)

    @pl.when(_last_step())
    def _():
        convo_ref[...] = cbuf[:, CONV_PAD - (CONV_W - 1):CONV_PAD, :]
        ho_ref[...] = hbuf[...]


def _rwkv_kernel(x_ref, g_ref, w_ref, mu_ref, w0_ref, a0_ref, wup_ref, kk_ref, ka_ref, rk_ref,
                 lng_ref, lnb_ref, seg_ref, tri_ref, shift0_ref, s0_ref,
                 y_ref, shifto_ref, so_ref,
                 shbuf, sbuf, lw_s, kk_s, kka_s, kp_s, r_s, v_s, y_s, *, nb, tb, L):
    c = y_ref.shape[-1]
    nc = tb // L
    npair = c // PAIR

    @pl.when(_first_step())
    def _():
        shbuf[...] = shift0_ref[...]
        sbuf[...] = s0_ref[...]

    x = x_ref[...].reshape(nb * tb, x_ref.shape[-1])
    p = _dot(_rms(x, g_ref[...]), w_ref[...])
    nsh = shbuf.shape[-1]
    pb = p[:, :nsh]
    gb = p[:, nsh:]
    xms = []
    for b in range(nb):
        slab = pb[b * tb:(b + 1) * tb]
        sh = _shift_rows(slab, 1, shbuf[b])
        shbuf[b] = slab[tb - 1:tb, :]
        xms.append(slab + (sh - slab) * mu_ref[...])
    xm = jnp.concatenate(xms, axis=0)
    r = xm[:, 0:c]
    k = xm[:, c:2 * c]
    v = xm[:, 2 * c:3 * c]
    lowrank = xm[:, 3 * c:]
    lane = lax.broadcasted_iota(jnp.int32, (1, lowrank.shape[-1]), 1)
    lowrank = jnp.where(lane < lowrank.shape[-1] // 2, jnp.tanh(lowrank), lowrank)
    up = _dot(lowrank, wup_ref[...])
    lw = (-RWKV_DECAY_SCALE) * _sigmoid(w0_ref[...] + up[:, :c])
    a = _sigmoid(a0_ref[...] + up[:, c:])
    kk = k * kk_ref[...]
    kk = kk * lax.rsqrt(_dot(kk * kk, seg_ref[...]) + 1e-12)
    kp = k * (1.0 + (a - 1.0) * ka_ref[...])
    lw_s[...] = lw
    kk_s[...] = kk
    kka_s[...] = kk * a
    kp_s[...] = kp
    r_s[...] = r
    v_s[...] = v
    bonus = _dot(r * kp * rk_ref[...], seg_ref[...]) * v

    m0, m1 = _head_masks()
    bd = _bd_mask()
    strict, incl, first = _cat_masks(L)
    tri = tri_ref[...]

    def chunk(it, carry):
        b = it // nc
        r0 = pl.multiple_of(it * L, L)
        rows = pl.ds(r0, L)
        lw_c = lw_s[rows, :]
        cl = _dot01(tri, lw_c)
        cl_last = cl[L - 1:L, :]
        e_p = jnp.exp(cl)
        e_ip = jnp.exp(-cl)
        e_rel = jnp.exp(cl_last - cl)
        kk_c = kk_s[rows, :]
        kka_c = kka_s[rows, :]
        kp_c = kp_s[rows, :]
        ab = -kk_c * jnp.exp(cl - lw_c)
        bb = kka_c * e_ip
        kb = kp_c * e_ip
        rb = r_s[rows, :] * e_p
        bbl = kka_c * e_rel
        kbl = kp_c * e_rel
        p_last = jnp.exp(cl_last)
        v_c = v_s[rows, :]
        for j in range(npair):
            ls = slice(j * PAIR, (j + 1) * PAIR)
            vp = v_c[:, ls]
            z = jnp.concatenate([_stack_heads(bb[:, ls], m0, m1), _stack_heads(kb[:, ls], m0, m1)], axis=0)
            xg = jnp.concatenate([ab[:, ls], rb[:, ls]], axis=0)
            g = _dot_nt_hi(xg, z)
            a_b = g[0:L, 0:2 * L]
            a_k = g[0:L, 2 * L:4 * L]
            r_b = g[L:2 * L, 0:2 * L]
            r_k = g[L:2 * L, 2 * L:4 * L]
            nn = _cat_to_blockdiag(jnp.where(strict, a_b, 0.0), first)
            mk = _cat_to_blockdiag(jnp.where(strict, a_k, 0.0), first)
            tinv = _tri_inv(nn, L)
            s = sbuf[b, j]
            a_s = _dot_nt(ab[:, ls], s)
            vst = _stack_heads(vp, m0, m1)
            rhs = _stack_heads(a_s, m0, m1) + _dot_hi(mk, vst)
            ust = _dot_hi(tinv, rhs)
            y = (_dot_nt(rb[:, ls], s) + _dot(jnp.where(incl, r_b, 0.0), ust)
                 + _dot(jnp.where(incl, r_k, 0.0), vst))
            u = ust[0:L] + ust[L:2 * L]
            s_new = s * p_last[:, ls] + _dot_tn(u, bbl[:, ls]) + _dot_tn(vp, kbl[:, ls])
            sbuf[b, j] = jnp.where(bd, s_new, 0.0)
            y_s[rows, ls] = y
        return carry

    lax.fori_loop(0, nb * nc, chunk, 0)

    y = y_s[...]
    inv = 1.0 / HEAD
    mean = _dot(y, seg_ref[...]) * inv
    yc = y - mean
    var = _dot(yc * yc, seg_ref[...]) * inv
    yn = yc * lax.rsqrt(var + RWKV_LN_EPS) * lng_ref[...] + lnb_ref[...]
    y_ref[...] = ((yn + bonus) * _silu(gb)).reshape(nb, tb, c)

    @pl.when(_last_step())
    def _():
        shifto_ref[...] = shbuf[...]
        so_ref[...] = sbuf[...]


def _ssd_kernel(x_ref, g_ref, w_ref, cw_ref, cb_ref, dtb_ref, alog_ref, d_ref, ng_ref, tri_ref,
                pick_ref, conv0_ref, h0_ref,
                y_ref, convo_ref, ho_ref,
                cbuf, hbuf, xs_s, bd_s, cd_s, dt_s, dta_s, y_s, *, nb, tb, L):
    c = y_ref.shape[-1]
    nc = tb // L
    npair = c // PAIR
    nconv = cbuf.shape[-1]

    @pl.when(_first_step())
    def _():
        cbuf[:, CONV_PAD - (CONV_W - 1):CONV_PAD, :] = conv0_ref[...]
        hbuf[...] = h0_ref[...]

    x = x_ref[...].reshape(nb * tb, x_ref.shape[-1])
    p = _dot(_rms(x, g_ref[...]), w_ref[...])
    xbc = p[:, :nconv]
    dtr = p[:, nconv:nconv + c]
    z = p[:, nconv + c:]
    conv = jnp.concatenate(
        [_causal_conv(cbuf, b, xbc[b * tb:(b + 1) * tb], tb, cw_ref, cb_ref[...]) for b in range(nb)],
        axis=0)
    conv = _silu(conv)
    xs = conv[:, :c]
    nbc = (nconv - c) // 2
    dt = _softplus(dtr + dtb_ref[...])
    xs_s[...] = xs
    bd_s[...] = conv[:, c:c + nbc]
    cd_s[...] = conv[:, c + nbc:]
    dt_s[...] = dt
    dta_s[...] = dt * (-jnp.exp(alog_ref[...]))

    m0, m1 = _head_masks()
    bdm = _bd_mask()
    _, incl, _ = _cat_masks(L)
    tri = tri_ref[...]
    pick = pick_ref[...]

    def cat_cols(x):
        if 2 * L == PAIR:
            return x
        return jnp.concatenate([x[:, 0:L], x[:, HEAD:HEAD + L]], axis=1)

    def chunk(it, carry):
        b = it // nc
        r0 = pl.multiple_of(it * L, L)
        rows = pl.ds(r0, L)
        acs = _dot01(tri, dta_s[rows, :])
        acs_last = acs[L - 1:L, :]
        dt_c = dt_s[rows, :]
        xs_c = xs_s[rows, :]
        e_acs = jnp.exp(acs)
        dl = jnp.exp(acs_last - acs) * dt_c
        tot = jnp.exp(acs_last)
        for j in range(npair):
            ls = slice(j * PAIR, (j + 1) * PAIR)
            gs = slice((j // 2) * PAIR, (j // 2 + 1) * PAIR)
            acs_p = acs[:, ls]
            xs_p = xs_c[:, ls]
            bd_p = bd_s[rows, gs]
            cd_p = cd_s[rows, gs]
            picked = _dot01_nt(pick, jnp.concatenate([acs_p, dt_c[:, ls]], axis=0))
            acs_row = jnp.concatenate([picked[0:1, 0:L], picked[1:2, 0:L]], axis=1)
            dt_row = jnp.concatenate([picked[0:1, L:2 * L], picked[1:2, L:2 * L]], axis=1)
            decay = jnp.where(incl, jnp.exp(cat_cols(acs_p) - acs_row), 0.0)
            cb = _dot_nt(cd_p, _stack_heads(bd_p, m0, m1))
            scores = cb * decay * dt_row
            h = hbuf[b, j]
            y = _dot(scores, _stack_heads(xs_p, m0, m1)) + e_acs[:, ls] * _dot_nt(cd_p, h)
            h_new = h * tot[:, ls] + _dot_tn(xs_p * dl[:, ls], bd_p)
            hbuf[b, j] = jnp.where(bdm, h_new, 0.0)
            y_s[rows, ls] = y
        return carry

    lax.fori_loop(0, nb * nc, chunk, 0)

    yc = y_s[...] + d_ref[...] * xs
    y_ref[...] = _rms(yc * _silu(z), ng_ref[...]).reshape(nb, tb, c)

    @pl.when(_last_step())
    def _():
        convo_ref[...] = cbuf[:, CONV_PAD - (CONV_W - 1):CONV_PAD, :]
        ho_ref[...] = hbuf[...]


def _hgrn_kernel(x_ref, g_ref, w_ref, lbl_ref, ng_ref, hall_ref, lmask_ref, seg_ref, s0_ref,
                 y_ref, so_ref,
                 sbuf, q_s, k_s, v_s, lf_s, y_s, *, nb, tb, L, layer):
    c = y_ref.shape[-1]
    nc = tb // L
    npair = c // PAIR
    nlev = lmask_ref.shape[0]

    @pl.when(_first_step())
    def _():
        sbuf[...] = s0_ref[...]

    x = x_ref[...].reshape(nb * tb, x_ref.shape[-1])
    p = _dot(_rms(x, g_ref[...]), w_ref[...])
    qd = p[:, 0:c]
    fd = p[:, c:2 * c]
    v = p[:, 2 * c:3 * c]
    gd = p[:, 3 * c:]
    logits = lbl_ref[...]
    ex = jnp.exp(logits - jnp.max(logits, axis=0, keepdims=True))
    sm = ex / jnp.sum(ex, axis=0, keepdims=True)
    lb = jnp.zeros_like(sm[0:1])
    for i in range(1, layer + 1):
        lb = lb + sm[i:i + 1]
    log_lb = jnp.log(lb)
    b2 = jnp.log1p(-lb) - _softplus(-fd)
    logf = jnp.maximum(log_lb, b2) + jnp.log1p(jnp.exp(-jnp.abs(log_lb - b2)))
    q = _silu(qd)
    kx = (1.0 - lb) * _sigmoid(-fd)
    q_s[...] = q
    k_s[...] = kx
    v_s[...] = v
    lf_s[...] = logf
    diag = _dot(q * kx, seg_ref[...]) * v

    m0, m1 = _head_masks()
    bdm = _bd_mask()
    hall = hall_ref[...]

    def chunk(it, carry):
        b = it // nc
        r0 = pl.multiple_of(it * L, L)
        rows = pl.ds(r0, L)
        d_all = _dot01(hall, lf_s[rows, :])
        q_c = q_s[rows, :]
        k_c = k_s[rows, :]
        v_c = v_s[rows, :]
        qe = q_c * jnp.exp(d_all[0:L])
        kl = k_c * jnp.exp(d_all[L:2 * L])
        tot = jnp.exp(d_all[L - 1:L])
        qn = []
        kn = []
        for m in range(nlev):
            e_m = jnp.exp(d_all[(2 + m) * L:(3 + m) * L])
            qn.append(q_c * e_m)
            kn.append(k_c * e_m)
        for j in range(npair):
            ls = slice(j * PAIR, (j + 1) * PAIR)
            att = jnp.zeros((L, 2 * L), F32)
            for m in range(nlev):
                att = att + lmask_ref[m] * _dot_nt(qn[m][:, ls], _stack_heads(kn[m][:, ls], m0, m1))
            st = sbuf[b, j]
            vp = v_c[:, ls]
            y = _dot_nt(qe[:, ls], st) + _dot(att, _stack_heads(vp, m0, m1))
            st_new = st * tot[:, ls] + _dot_tn(vp, kl[:, ls])
            sbuf[b, j] = jnp.where(bdm, st_new, 0.0)
            y_s[rows, ls] = y
        return carry

    lax.fori_loop(0, nb * nc, chunk, 0)

    o = y_s[...] + diag
    ms = _dot(o * o, seg_ref[...]) * (1.0 / HEAD)
    yd = o * lax.rsqrt(ms + NORM_EPS) * ng_ref[...]
    y_ref[...] = (yd * _silu(gd)).reshape(nb, tb, c)

    @pl.when(_last_step())
    def _():
        so_ref[...] = sbuf[...]


def _post_kernel(ya_ref, yb_ref, yc_ref, yd_ref, x_ref, wo_ref, gpost_ref, gprex_ref, wq_ref,
                 mk_ref, mv_ref, wox_ref, gpostx_ref, o_ref, *, nb, tb, heads):
    d = x_ref.shape[-1]
    c = ya_ref.shape[-1]
    hd = d // heads
    rows = nb * tb
    y = None
    for i, ref in enumerate((ya_ref, yb_ref, yc_ref, yd_ref)):
        t = _dot(ref[...].reshape(rows, c), wo_ref[i * c:(i + 1) * c, :])
        y = t if y is None else y + t
    x1 = x_ref[...].reshape(rows, d) + _rms(y, gpost_ref[...])
    q = _dot(_rms(x1, gprex_ref[...]), wq_ref[...])
    scale = hd ** -0.5
    outs = []
    for b in range(nb):
        qb = q[b * tb:(b + 1) * tb]
        heads_o = []
        for h in range(heads):
            hs = slice(h * hd, (h + 1) * hd)
            s = _dot_nt(qb[:, hs], mk_ref[b, :, hs]) * scale
            s = s - jnp.max(s, axis=-1, keepdims=True)
            e = jnp.exp(s)
            pr = e / jnp.sum(e, axis=-1, keepdims=True)
            heads_o.append(_dot(pr, mv_ref[b, :, hs]))
        outs.append(jnp.concatenate(heads_o, axis=1))
    o = jnp.concatenate(outs, axis=0)
    x2 = x1 + _rms(_dot(o, wox_ref[...]), gpostx_ref[...])
    o_ref[...] = x2.reshape(nb, tb, d)


def _memkv_kernel(m_ref, g_ref, wk_ref, wv_ref, k_ref, v_ref):
    m = _rms(m_ref[0], g_ref[...])
    k_ref[0] = _dot(m, wk_ref[...])
    v_ref[0] = _dot(m, wv_ref[...])


def _full(shape):
    nd = len(shape)
    return pl.BlockSpec(shape, lambda b, t: (0,) * nd)


def _per_batch(shape_tail, nb):
    nd = len(shape_tail)
    return pl.BlockSpec((nb,) + shape_tail, lambda b, t: (b,) + (0,) * nd)


def _tokens(nb, tb, width):
    return pl.BlockSpec((nb, tb, width), lambda b, t: (b, t, 0))


def _params():
    return pltpu.CompilerParams(dimension_semantics=("arbitrary", "arbitrary"),
                                vmem_limit_bytes=VMEM_LIMIT)


def _call(kern, name, grid, in_arrays, in_specs, out_shapes, out_specs, scratch):
    return pl.pallas_call(
        kern, name=name, grid=grid, in_specs=in_specs, out_specs=out_specs,
        out_shape=out_shapes, scratch_shapes=scratch, compiler_params=_params())(*in_arrays)


def _row(v):
    return v.reshape(1, -1).astype(F32)


def _lru_call(x, lp, conv0, h0, nb, tb):
    B, T, D = x.shape
    c = lp["lru_cw"].shape[-1]
    grid = (B // nb, T // tb)
    ins = [x, lp["g_pre"], lp["w_a"], lp["lru_cw"], lp["lru_cb"], lp["lru_wg"], lp["lru_bg"], lp["lru_lam"],
           conv0, h0]
    specs = [_tokens(nb, tb, D)] + [_full(a.shape) for a in ins[1:8]] + [
        _per_batch((CONV_W - 1, c), nb), _per_batch((1, c), nb)]
    outs = [jax.ShapeDtypeStruct((B, T, c), F32), jax.ShapeDtypeStruct((B, CONV_W - 1, c), F32),
            jax.ShapeDtypeStruct((B, 1, c), F32)]
    ospecs = [_tokens(nb, tb, c), _per_batch((CONV_W - 1, c), nb), _per_batch((1, c), nb)]
    scratch = [pltpu.VMEM((nb, tb + CONV_PAD, c), F32), pltpu.VMEM((nb, 1, c), F32)]
    return _call(functools.partial(_lru_kernel, nb=nb, tb=tb), "mix_lru", grid, ins, specs, outs, ospecs, scratch)


def _rwkv_call(x, lp, consts, shift0, s0, nb, tb, L):
    B, T, D = x.shape
    c = lp["rw_w0"].shape[-1]
    nsh = shift0.shape[-1]
    npair = c // PAIR
    grid = (B // nb, T // tb)
    ins = [x, lp["g_pre"], lp["w_b"], lp["rw_mu"], lp["rw_w0"], lp["rw_a0"], lp["rw_wup"], lp["rw_kk"],
           lp["rw_ka"], lp["rw_rk"], lp["rw_lng"], lp["rw_lnb"], consts["seg"], consts["tri"], shift0, s0]
    specs = [_tokens(nb, tb, D)] + [_full(a.shape) for a in ins[1:14]] + [
        _per_batch((1, nsh), nb), _per_batch((npair, PAIR, PAIR), nb)]
    outs = [jax.ShapeDtypeStruct((B, T, c), F32), jax.ShapeDtypeStruct((B, 1, nsh), F32),
            jax.ShapeDtypeStruct((B, npair, PAIR, PAIR), F32)]
    ospecs = [_tokens(nb, tb, c), _per_batch((1, nsh), nb), _per_batch((npair, PAIR, PAIR), nb)]
    rows = nb * tb
    scratch = [pltpu.VMEM((nb, 1, nsh), F32), pltpu.VMEM((nb, npair, PAIR, PAIR), F32)] + [
        pltpu.VMEM((rows, c), F32) for _ in range(7)]
    return _call(functools.partial(_rwkv_kernel, nb=nb, tb=tb, L=L), "mix_rwkv", grid, ins, specs, outs,
                 ospecs, scratch)


def _ssd_call(x, lp, consts, conv0, h0, nb, tb, L):
    B, T, D = x.shape
    c = lp["ssd_dtb"].shape[-1]
    nconv = lp["ssd_cw"].shape[-1]
    nbc = (nconv - c) // 2
    npair = c // PAIR
    grid = (B // nb, T // tb)
    ins = [x, lp["g_pre"], lp["w_c"], lp["ssd_cw"], lp["ssd_cb"], lp["ssd_dtb"], lp["ssd_alog"], lp["ssd_d"],
           lp["ssd_ng"], consts["tri"], consts["pick"], conv0, h0]
    specs = [_tokens(nb, tb, D)] + [_full(a.shape) for a in ins[1:11]] + [
        _per_batch((CONV_W - 1, nconv), nb), _per_batch((npair, PAIR, PAIR), nb)]
    outs = [jax.ShapeDtypeStruct((B, T, c), F32), jax.ShapeDtypeStruct((B, CONV_W - 1, nconv), F32),
            jax.ShapeDtypeStruct((B, npair, PAIR, PAIR), F32)]
    ospecs = [_tokens(nb, tb, c), _per_batch((CONV_W - 1, nconv), nb), _per_batch((npair, PAIR, PAIR), nb)]
    rows = nb * tb
    scratch = [pltpu.VMEM((nb, tb + CONV_PAD, nconv), F32), pltpu.VMEM((nb, npair, PAIR, PAIR), F32),
               pltpu.VMEM((rows, c), F32), pltpu.VMEM((rows, nbc), F32), pltpu.VMEM((rows, nbc), F32),
               pltpu.VMEM((rows, c), F32), pltpu.VMEM((rows, c), F32), pltpu.VMEM((rows, c), F32)]
    return _call(functools.partial(_ssd_kernel, nb=nb, tb=tb, L=L), "mix_ssd", grid, ins, specs, outs,
                 ospecs, scratch)


def _hgrn_call(x, lp, consts, s0, nb, tb, L, layer):
    B, T, D = x.shape
    c = lp["hg_ng"].shape[-1]
    npair = c // PAIR
    grid = (B // nb, T // tb)
    ins = [x, lp["g_pre"], lp["w_d"], lp["hg_lbl"], lp["hg_ng"], consts["hall"], consts["lmask"], consts["seg"], s0]
    specs = [_tokens(nb, tb, D)] + [_full(a.shape) for a in ins[1:8]] + [_per_batch((npair, PAIR, PAIR), nb)]
    outs = [jax.ShapeDtypeStruct((B, T, c), F32), jax.ShapeDtypeStruct((B, npair, PAIR, PAIR), F32)]
    ospecs = [_tokens(nb, tb, c), _per_batch((npair, PAIR, PAIR), nb)]
    rows = nb * tb
    scratch = [pltpu.VMEM((nb, npair, PAIR, PAIR), F32)] + [pltpu.VMEM((rows, c), F32) for _ in range(5)]
    return _call(functools.partial(_hgrn_kernel, nb=nb, tb=tb, L=L, layer=layer), "mix_hgrn", grid, ins, specs,
                 outs, ospecs, scratch)


def _post_call(ys, x, lp, mk, mv, nb, tb, heads):
    B, T, D = x.shape
    c = ys[0].shape[-1]
    M = mk.shape[1]
    grid = (B // nb, T // tb)
    ins = list(ys) + [x, lp["w_out"], lp["g_post"], lp["g_pre_x"], lp["w_q"], mk, mv, lp["w_o"], lp["g_post_x"]]
    specs = [_tokens(nb, tb, c)] * 4 + [_tokens(nb, tb, D)] + [_full(a.shape) for a in ins[5:9]] + [
        _per_batch((M, D), nb), _per_batch((M, D), nb)] + [_full(a.shape) for a in ins[11:13]]
    return _call(functools.partial(_post_kernel, nb=nb, tb=tb, heads=heads), "post_attn", grid, ins, specs,
                 jax.ShapeDtypeStruct((B, T, D), F32), _tokens(nb, tb, D), [])


def _memkv_call(mem, g, wk, wv):
    B, M, D = mem.shape
    spec = pl.BlockSpec((1, M, D), lambda b: (b, 0, 0))
    wspec = pl.BlockSpec((D, D), lambda b: (0, 0))
    return pl.pallas_call(
        _memkv_kernel, name="mem_kv", grid=(B,),
        in_specs=[spec, pl.BlockSpec((1, D), lambda b: (0, 0)), wspec, wspec],
        out_specs=[spec, spec],
        out_shape=[jax.ShapeDtypeStruct((B, M, D), F32)] * 2,
        compiler_params=pltpu.CompilerParams(dimension_semantics=("arbitrary",),
                                             vmem_limit_bytes=VMEM_LIMIT))(mem, g, wk, wv)


def _chunk_consts(L, c):
    r = np.arange(L)[:, None]
    j = np.arange(L)[None, :]
    tri = (j <= r).astype(np.float32)
    last = (j > r).astype(np.float32)
    secs = [tri, last]
    masks = []
    t = np.arange(L)[:, None]
    col = np.arange(2 * L)[None, :]
    s = col % L
    m = 1
    while (1 << m) <= L:
        size, half = 1 << m, 1 << (m - 1)
        mid = (r // size) * size + half - 1
        secs.append(np.where(r > mid, (j > mid) & (j <= r), (j > r) & (j <= mid)).astype(np.float32))
        masks.append(((t // size == s // size) & (t % size >= half) & (s % size < half)).astype(np.float32))
        m += 1
    seg = (np.arange(c)[:, None] // HEAD == np.arange(c)[None, :] // HEAD).astype(np.float32)
    pick = np.zeros((8, PAIR), np.float32)
    pick[0, 0] = 1.0
    pick[1, HEAD] = 1.0
    return {
        "tri": jnp.asarray(tri, BF16),
        "hall": jnp.asarray(np.concatenate(secs, axis=0), BF16),
        "lmask": jnp.asarray(np.stack(masks), F32),
        "seg": jnp.asarray(seg, BF16),
        "pick": jnp.asarray(pick, BF16),
    }


def _to_pairs(s):
    B, H, a, b = s.shape
    s = s.reshape(B, H // 2, 2, a, b)
    z = jnp.zeros_like(s[:, :, 0])
    top = jnp.concatenate([s[:, :, 0], z], axis=-1)
    bot = jnp.concatenate([z, s[:, :, 1]], axis=-1)
    return jnp.concatenate([top, bot], axis=-2)


def _from_pairs(s):
    B, P = s.shape[:2]
    return jnp.stack([s[:, :, :HEAD, :HEAD], s[:, :, HEAD:, HEAD:]], axis=2).reshape(B, 2 * P, HEAD, HEAD)


def _expand_bc(t, c, n):
    xs, bm, cm = t[..., :c], t[..., c:c + 2 * n], t[..., c + 2 * n:]
    dup = lambda u: jnp.concatenate([u[..., :n], u[..., :n], u[..., n:], u[..., n:]], axis=-1)
    return jnp.concatenate([xs, dup(bm), dup(cm)], axis=-1)


def _shrink_bc(t, c, n):
    pick = lambda u: jnp.concatenate([u[..., :n], u[..., 2 * n:3 * n]], axis=-1)
    return jnp.concatenate([t[..., :c], pick(t[..., c:c + 4 * n]), pick(t[..., c + 4 * n:])], axis=-1)


def _blockdiag(w):
    n, d, e = w.shape
    eye = jnp.eye(n, dtype=w.dtype)
    return (eye[:, None, :, None] * w[:, :, None, :]).reshape(n * d, n * e)


def kernel(x_prompt, x_sample, state_lru_conv, state_lru_h, state_rwkv_shift, state_rwkv_wkv, state_ssd_conv, state_ssd_h, state_hgrn_s, cache_mem_k, cache_mem_v, mem_prompt, g_pre, g_post, g_pre_x, g_post_x, w_in, w_out, lru_conv_w, lru_conv_b, lru_w_r, lru_b_r, lru_w_i, lru_b_i, lru_lambda, rwkv_mu, rwkv_w0, rwkv_w_up, rwkv_a0, rwkv_a_up, rwkv_k_k, rwkv_k_a, rwkv_r_k, rwkv_ln_g, rwkv_ln_b, ssd_conv_w, ssd_conv_b, ssd_dt_bias, ssd_a_log, ssd_d, ssd_norm_g, hgrn_lb_logits, hgrn_norm_g, mem_g, mem_w_q, mem_w_k, mem_w_v, mem_w_o):
    depth = w_in.shape[0]
    D = x_prompt.shape[-1]
    c = lru_conv_w.shape[-1]
    nsh = rwkv_mu.shape[-1]
    rank = (nsh - 3 * c) // 2
    ssd_heads = ssd_dt_bias.shape[-1]
    nstate = (ssd_conv_w.shape[-1] - c) // 4
    heads_x = cache_mem_k.shape[-2]
    mem_len = mem_prompt.shape[1]

    o_b = 2 * c
    o_c = o_b + nsh + c
    o_dt = o_c + c + 4 * nstate
    o_z = o_dt + ssd_heads
    o_d = o_z + c

    layers = []
    for l in range(depth):
        w = w_in[l]
        zero = jnp.zeros((rank, c), F32)
        wup = jnp.concatenate([jnp.concatenate([rwkv_w_up[l], zero], axis=1),
                               jnp.concatenate([zero, rwkv_a_up[l]], axis=1)], axis=0)
        rep = lambda v: jnp.repeat(v, c // ssd_heads, axis=-1)
        w_c = jnp.concatenate([_expand_bc(w[:, o_c:o_dt], c, nstate), rep(w[:, o_dt:o_z]), w[:, o_z:o_d]], axis=1)
        layers.append({
            "g_pre": _row(g_pre[l]), "g_post": _row(g_post[l]), "g_pre_x": _row(g_pre_x[l]),
            "g_post_x": _row(g_post_x[l]),
            "w_a": w[:, :o_b].astype(BF16), "w_b": w[:, o_b:o_c].astype(BF16), "w_c": w_c.astype(BF16),
            "w_d": w[:, o_d:].astype(BF16),
            "w_out": w_out[l].astype(BF16), "w_q": mem_w_q[l].astype(BF16), "w_o": mem_w_o[l].astype(BF16),
            "lru_cw": lru_conv_w[l], "lru_cb": _row(lru_conv_b[l]),
            "lru_wg": jnp.concatenate([_blockdiag(lru_w_r[l]), _blockdiag(lru_w_i[l])], axis=1).astype(BF16),
            "lru_bg": _row(jnp.concatenate([lru_b_r[l], lru_b_i[l]])), "lru_lam": _row(lru_lambda[l]),
            "rw_mu": _row(rwkv_mu[l]), "rw_w0": _row(rwkv_w0[l]), "rw_a0": _row(rwkv_a0[l]),
            "rw_wup": wup.astype(BF16), "rw_kk": _row(rwkv_k_k[l]), "rw_ka": _row(rwkv_k_a[l]),
            "rw_rk": _row(rwkv_r_k[l]), "rw_lng": _row(rwkv_ln_g[l]), "rw_lnb": _row(rwkv_ln_b[l]),
            "ssd_cw": _expand_bc(ssd_conv_w[l], c, nstate), "ssd_cb": _row(_expand_bc(ssd_conv_b[l], c, nstate)),
            "ssd_dtb": _row(rep(ssd_dt_bias[l])), "ssd_alog": _row(rep(ssd_a_log[l])), "ssd_d": _row(rep(ssd_d[l])),
            "ssd_ng": _row(ssd_norm_g[l]),
            "hg_lbl": hgrn_lb_logits.astype(F32), "hg_ng": _row(hgrn_norm_g[l]),
        })

    def run(x, mk, mv, conv_a, h_a, shift_b, wkv_b, conv_c, h_c, s_d, nb, tb, L):
        consts = _chunk_consts(L, c)
        B = x.shape[0]
        acc = [[] for _ in range(7)]
        for l in range(depth):
            lp = layers[l]
            ya, nca, nha = _lru_call(x, lp, conv_a[l], h_a[l].reshape(B, 1, c), nb, tb)
            yb, nsb, nwb = _rwkv_call(x, lp, consts, shift_b[l].reshape(B, 1, nsh), _to_pairs(wkv_b[l]), nb, tb, L)
            yc, ncc, nhc = _ssd_call(x, lp, consts, _expand_bc(conv_c[l], c, nstate), _to_pairs(h_c[l]), nb, tb, L)
            yd, nsd = _hgrn_call(x, lp, consts, _to_pairs(jnp.swapaxes(s_d[l], -1, -2)), nb, tb, L, l)
            x = _post_call((ya, yb, yc, yd), x, lp, mk[l].reshape(B, mem_len, D), mv[l].reshape(B, mem_len, D),
                           nb, tb, heads_x)
            for lst, val in zip(acc, (nca, nha.reshape(B, c), nsb.reshape(B, nsh), _from_pairs(nwb),
                                      _shrink_bc(ncc, c, nstate), _from_pairs(nhc),
                                      jnp.swapaxes(_from_pairs(nsd), -1, -2))):
                lst.append(val)
        return (x,) + tuple(jnp.stack(v) for v in acc)

    Bp, Tp = x_prompt.shape[:2]
    Bs, Ts = x_sample.shape[:2]
    kv = [_memkv_call(mem_prompt, _row(mem_g[l]), mem_w_k[l].astype(BF16), mem_w_v[l].astype(BF16))
          for l in range(depth)]
    kv_shape = (Bp, mem_len, heads_x, D // heads_x)
    mem_k_p = jnp.stack([t[0].reshape(kv_shape) for t in kv])
    mem_v_p = jnp.stack([t[1].reshape(kv_shape) for t in kv])
    zeros = lambda *s: jnp.zeros((depth, Bp) + s, F32)
    Lp = CHUNK if Tp % CHUNK == 0 else Tp
    Ls = CHUNK if Ts % CHUNK == 0 else Ts
    tb_p = 4 * Lp if Tp % (4 * Lp) == 0 else Lp
    nb_s = 8 if Bs % 8 == 0 else 1
    (y_prompt, lru_conv_p, lru_h_p, rwkv_shift_p, rwkv_wkv_p, ssd_conv_p, ssd_h_p, hgrn_s_p) = run(
        x_prompt, mem_k_p, mem_v_p,
        zeros(CONV_W - 1, c), zeros(c), zeros(nsh), zeros(c // HEAD, HEAD, HEAD),
        zeros(CONV_W - 1, c + 4 * nstate), zeros(ssd_heads, HEAD, nstate), zeros(c // HEAD, HEAD, HEAD),
        1, tb_p, Lp)
    (y_sample, lru_conv_s, lru_h_s, rwkv_shift_s, rwkv_wkv_s, ssd_conv_s, ssd_h_s, hgrn_s_s) = run(
        x_sample, cache_mem_k, cache_mem_v, state_lru_conv, state_lru_h, state_rwkv_shift, state_rwkv_wkv,
        state_ssd_conv, state_ssd_h, state_hgrn_s, nb_s, Ts, Ls)
    return (y_prompt, y_sample, lru_conv_p, lru_conv_s, lru_h_p, lru_h_s, rwkv_shift_p, rwkv_shift_s,
            rwkv_wkv_p, rwkv_wkv_s, ssd_conv_p, ssd_conv_s, ssd_h_p, ssd_h_s, hgrn_s_p, hgrn_s_s,
            mem_k_p, mem_v_p)
```

```python
import functools

import numpy as np
import jax
import jax.numpy as jnp
from jax import lax
from jax.experimental import pallas as pl
from jax.experimental.pallas import tpu as pltpu

F32 = jnp.float32
BF16 = jnp.bfloat16

LANES = 128
HEAD = 64
PAIR = 2 * HEAD
SUB = 16
SUB_SHIFT = 4
CHUNK = 64
CONV_W = 4
CONV_PAD = 8
LRU_C = 8.0
RWKV_DECAY_SCALE = 0.6065306597126334
RWKV_LN_EPS = 64e-5
NORM_EPS = 1e-6
VMEM_LIMIT = 56 * 1024 * 1024


def _dot(a, b):
    return jnp.dot(a.astype(BF16), b.astype(BF16), preferred_element_type=F32)


def _dot_nt(a, b):
    return lax.dot_general(a.astype(BF16), b.astype(BF16), (((1,), (1,)), ((), ())),
                           preferred_element_type=F32)


def _dot_tn(a, b):
    return lax.dot_general(a.astype(BF16), b.astype(BF16), (((0,), (0,)), ((), ())),
                           preferred_element_type=F32)


def _split3(x):
    hi = x.astype(BF16)
    r1 = x - hi.astype(F32)
    mid = r1.astype(BF16)
    lo = (r1 - mid.astype(F32)).astype(BF16)
    return hi, mid, lo


def _split2(x):
    hi = x.astype(BF16)
    return hi, (x - hi.astype(F32)).astype(BF16)


def _dot3(a, b):
    ah, al = _split2(a)
    bh, bl = _split2(b)
    d = lambda u, w: jnp.dot(u, w, preferred_element_type=F32)
    return d(ah, bh) + d(ah, bl) + d(al, bh)


def _dot3_nt(a, b):
    ah, al = _split2(a)
    bh, bl = _split2(b)
    d = lambda u, w: lax.dot_general(u, w, (((1,), (1,)), ((), ())), preferred_element_type=F32)
    return d(ah, bh) + d(ah, bl) + d(al, bh)


def _group_size(n):
    return 2 if n % 2 == 0 else 1


def _dot01(m01, x):
    hi, mid, lo = _split3(x)
    d = lambda p: jnp.dot(m01, p, preferred_element_type=F32)
    return d(hi) + d(mid) + d(lo)


def _dot01_nt(m01, x):
    hi, mid, lo = _split3(x)
    d = lambda p: lax.dot_general(m01, p, (((1,), (1,)), ((), ())), preferred_element_type=F32)
    return d(hi) + d(mid) + d(lo)


def _rms(x, g):
    return x * lax.rsqrt(jnp.mean(x * x, axis=-1, keepdims=True) + NORM_EPS) * g


def _sigmoid(x):
    return jax.nn.sigmoid(x)


def _silu(x):
    return x * jax.nn.sigmoid(x)


def _softplus(x):
    return jnp.maximum(x, 0.0) + jnp.log1p(jnp.exp(-jnp.abs(x)))


def _neg_expm1(z):
    return -jnp.tanh(0.5 * z) * (jnp.exp(z) + 1.0)


def _head_masks():
    lane = lax.broadcasted_iota(jnp.int32, (1, PAIR), 1)
    return lane < HEAD, lane >= HEAD


def _stack_heads(x, m0, m1):
    return jnp.concatenate([jnp.where(m0, x, 0.0), jnp.where(m1, x, 0.0)], axis=0)


def _bd_mask():
    r = lax.broadcasted_iota(jnp.int32, (PAIR, PAIR), 0)
    c = lax.broadcasted_iota(jnp.int32, (PAIR, PAIR), 1)
    return (r >= HEAD) == (c >= HEAD)


def _cat_masks(L):
    t = lax.broadcasted_iota(jnp.int32, (L, 2 * L), 0)
    c = lax.broadcasted_iota(jnp.int32, (L, 2 * L), 1)
    s = jnp.where(c >= L, c - L, c)
    return s < t, s <= t, c < L


def _cat_to_blockdiag(m, first):
    return jnp.concatenate([jnp.where(first, m, 0.0), jnp.where(first, 0.0, m)], axis=0)


def _tri_inv(nns, L):
    n = nns[0].shape[0]
    ri = lax.broadcasted_iota(jnp.int32, (n, n), 0)
    ci = lax.broadcasted_iota(jnp.int32, (n, n), 1)
    eye = (ri == ci).astype(F32)
    diag = (ri >> SUB_SHIFT) == (ci >> SUB_SHIFT)
    nd = [jnp.where(diag, t, 0.0) for t in nns]
    no = [t - d for t, d in zip(nns, nd)]
    td = [eye + d for d in nd]
    p = nd
    k = 2
    while k < SUB:
        p = [_dot3(t, t) for t in p]
        td = [t + _dot3(t, q) for t, q in zip(td, p)]
        k *= 2
    w = [_dot3(t, o) for t, o in zip(td, no)]
    tm = [eye + t for t in w]
    pw = w
    k = 2
    while k < L // SUB:
        pw = [_dot3(t, t) for t in pw]
        tm = [t + _dot3(t, q) for t, q in zip(tm, pw)]
        k *= 2
    return [_dot3(a, b) for a, b in zip(tm, td)]


def _shift_rows(x, d, fill):
    row = lax.broadcasted_iota(jnp.int32, x.shape, 0)
    return jnp.where(row >= d, pltpu.roll(x, d, axis=0), fill)


def _affine_scan(a, u):
    n = a.shape[0]
    d = 1
    while d < n:
        a_s = _shift_rows(a, d, 1.0)
        u_s = _shift_rows(u, d, 0.0)
        u = a * u_s + u
        a = a * a_s
        d *= 2
    return a, u


def _causal_conv(cbuf, b, xa, tb, w_ref, bias):
    cbuf[b, CONV_PAD:CONV_PAD + tb, :] = xa
    y = bias + xa * w_ref[CONV_W - 1:CONV_W, :]
    for j in range(1, CONV_W):
        y = y + cbuf[b, CONV_PAD - j:CONV_PAD - j + tb, :] * w_ref[CONV_W - 1 - j:CONV_W - j, :]
    hist = cbuf[b, tb + CONV_PAD - (CONV_W - 1):tb + CONV_PAD, :]
    cbuf[b, CONV_PAD - (CONV_W - 1):CONV_PAD, :] = hist
    return y


def _first_step():
    return pl.program_id(1) == 0


def _last_step():
    return pl.program_id(1) == pl.num_programs(1) - 1


def _lru_kernel(x_ref, g_ref, w_ref, cw_ref, cb_ref, wg_ref, bg_ref, lam_ref, conv0_ref, h0_ref,
                y_ref, convo_ref, ho_ref, cbuf, hbuf, *, nb, tb):
    c = y_ref.shape[-1]

    @pl.when(_first_step())
    def _():
        cbuf[:, CONV_PAD - (CONV_W - 1):CONV_PAD, :] = conv0_ref[...]
        hbuf[...] = h0_ref[...]

    x = x_ref[...].reshape(nb * tb, x_ref.shape[-1])
    p = _dot(_rms(x, g_ref[...]), w_ref[...])
    xa = p[:, :c]
    ga = p[:, c:]
    xc = jnp.concatenate(
        [_causal_conv(cbuf, b, xa[b * tb:(b + 1) * tb], tb, cw_ref, cb_ref[...]) for b in range(nb)],
        axis=0)
    gates = _dot(xc, wg_ref[...]) + bg_ref[...]
    r = _sigmoid(gates[:, :c])
    i = _sigmoid(gates[:, c:])
    log_a = (-LRU_C) * r * _softplus(-lam_ref[...])
    a = jnp.exp(log_a)
    u = jnp.sqrt(_neg_expm1(2.0 * log_a)) * (i * xc)
    hs = []
    for b in range(nb):
        a_c, h_c = _affine_scan(a[b * tb:(b + 1) * tb], u[b * tb:(b + 1) * tb])
        h = h_c + a_c * hbuf[b]
        hbuf[b] = h[tb - 1:tb, :]
        hs.append(h)
    h = jnp.concatenate(hs, axis=0)
    y_ref[...] = (h * _silu(ga)).reshape(nb, tb, c)

    @pl.when(_last_step())
    def _():
        convo_ref[...] = cbuf[:, CONV_PAD - (CONV_W - 1):CONV_PAD, :]
        ho_ref[...] = hbuf[...]


def _rwkv_kernel(x_ref, g_ref, w_ref, mu_ref, w0_ref, a0_ref, wup_ref, kk_ref, ka_ref, rk_ref,
                 lng_ref, lnb_ref, seg_ref, tri_ref, shift0_ref, s0_ref,
                 y_ref, shifto_ref, so_ref,
                 shbuf, sbuf, lw_s, kk_s, kka_s, kp_s, r_s, v_s, y_s, *, nb, tb, L):
    c = y_ref.shape[-1]
    nc = tb // L
    npair = c // PAIR

    @pl.when(_first_step())
    def _():
        shbuf[...] = shift0_ref[...]
        sbuf[...] = s0_ref[...]

    x = x_ref[...].reshape(nb * tb, x_ref.shape[-1])
    p = _dot(_rms(x, g_ref[...]), w_ref[...])
    nsh = shbuf.shape[-1]
    pb = p[:, :nsh]
    gb = p[:, nsh:]
    xms = []
    for b in range(nb):
        slab = pb[b * tb:(b + 1) * tb]
        sh = _shift_rows(slab, 1, shbuf[b])
        shbuf[b] = slab[tb - 1:tb, :]
        xms.append(slab + (sh - slab) * mu_ref[...])
    xm = jnp.concatenate(xms, axis=0)
    r = xm[:, 0:c]
    k = xm[:, c:2 * c]
    v = xm[:, 2 * c:3 * c]
    lowrank = xm[:, 3 * c:]
    lane = lax.broadcasted_iota(jnp.int32, (1, lowrank.shape[-1]), 1)
    lowrank = jnp.where(lane < lowrank.shape[-1] // 2, jnp.tanh(lowrank), lowrank)
    up = _dot(lowrank, wup_ref[...])
    lw = (-RWKV_DECAY_SCALE) * _sigmoid(w0_ref[...] + up[:, :c])
    a = _sigmoid(a0_ref[...] + up[:, c:])
    kk = k * kk_ref[...]
    kk = kk * lax.rsqrt(_dot(kk * kk, seg_ref[...]) + 1e-12)
    kp = k * (1.0 + (a - 1.0) * ka_ref[...])
    lw_s[...] = lw
    kk_s[...] = kk
    kka_s[...] = kk * a
    kp_s[...] = kp
    r_s[...] = r
    v_s[...] = v
    bonus = _dot(r * kp * rk_ref[...], seg_ref[...]) * v

    m0, m1 = _head_masks()
    bd = _bd_mask()
    strict, incl, first = _cat_masks(L)
    tri = tri_ref[...]

    grp = _group_size(nb * nc)
    chains = [(g, j) for g in range(grp) for j in range(npair)]
    lanes = lambda j: slice(j * PAIR, (j + 1) * PAIR)

    def group(i, carry):
        per = []
        for g in range(grp):
            it = i * grp + g
            rows = pl.ds(pl.multiple_of(it * L, L), L)
            lw_c = lw_s[rows, :]
            cl = _dot01(tri, lw_c)
            cl_last = cl[L - 1:L, :]
            e_ip = jnp.exp(-cl)
            e_rel = jnp.exp(cl_last - cl)
            kk_c = kk_s[rows, :]
            kka_c = kka_s[rows, :]
            kp_c = kp_s[rows, :]
            per.append(dict(
                b=it // nc, rows=rows,
                ab=-kk_c * jnp.exp(cl - lw_c), bb=kka_c * e_ip, kb=kp_c * e_ip,
                rb=r_s[rows, :] * jnp.exp(cl), bbl=kka_c * e_rel, kbl=kp_c * e_rel,
                p_last=jnp.exp(cl_last), v=v_s[rows, :]))
        part = lambda key: [per[g][key][:, lanes(j)] for g, j in chains]
        abp, rbp, vp = part("ab"), part("rb"), part("v")
        bblp, kblp = part("bbl"), part("kbl")
        zb = [_stack_heads(t, m0, m1) for t in part("bb")]
        zk = [_stack_heads(t, m0, m1) for t in part("kb")]
        a_b = [_dot3_nt(a, z) for a, z in zip(abp, zb)]
        g1 = [_dot_nt(jnp.concatenate([a, r], axis=0), jnp.concatenate([b_, k_], axis=0))
              for a, r, b_, k_ in zip(abp, rbp, zb, zk)]
        nn = [_cat_to_blockdiag(jnp.where(strict, t, 0.0), first) for t in a_b]
        mk = [_cat_to_blockdiag(jnp.where(strict, t[0:L, 2 * L:4 * L], 0.0), first) for t in g1]
        r_b = [jnp.where(incl, t[L:2 * L, 0:2 * L], 0.0) for t in g1]
        r_k = [jnp.where(incl, t[L:2 * L, 2 * L:4 * L], 0.0) for t in g1]
        tinv = _tri_inv(nn, L)
        vst = [_stack_heads(t, m0, m1) for t in vp]
        mkv = [_dot(m_, v_) for m_, v_ in zip(mk, vst)]
        ykv = [_dot(m_, v_) for m_, v_ in zip(r_k, vst)]
        skv = [_dot_tn(v_, k_) for v_, k_ in zip(vp, kblp)]
        for g in range(grp):
            ns = [n for n, (gg, _) in enumerate(chains) if gg == g]
            b, rows, p_last = per[g]["b"], per[g]["rows"], per[g]["p_last"]
            s = [sbuf[b, j] for j in range(npair)]
            a_s = [_dot_nt(abp[n], s[j]) for j, n in enumerate(ns)]
            r_s_ = [_dot_nt(rbp[n], s[j]) for j, n in enumerate(ns)]
            ust = [_dot3(tinv[n], _stack_heads(a_s[j], m0, m1) + mkv[n]) for j, n in enumerate(ns)]
            yv = [r_s_[j] + _dot(r_b[n], ust[j]) + ykv[n] for j, n in enumerate(ns)]
            su = [_dot_tn(ust[j][0:L] + ust[j][L:2 * L], bblp[n]) for j, n in enumerate(ns)]
            for j, n in enumerate(ns):
                s_new = s[j] * p_last[:, lanes(j)] + su[j] + skv[n]
                sbuf[b, j] = jnp.where(bd, s_new, 0.0)
                y_s[rows, lanes(j)] = yv[j]
        return carry

    lax.fori_loop(0, nb * nc // grp, group, 0)

    y = y_s[...]
    inv = 1.0 / HEAD
    mean = _dot(y, seg_ref[...]) * inv
    yc = y - mean
    var = _dot(yc * yc, seg_ref[...]) * inv
    yn = yc * lax.rsqrt(var + RWKV_LN_EPS) * lng_ref[...] + lnb_ref[...]
    y_ref[...] = ((yn + bonus) * _silu(gb)).reshape(nb, tb, c)

    @pl.when(_last_step())
    def _():
        shifto_ref[...] = shbuf[...]
        so_ref[...] = sbuf[...]


def _ssd_kernel(x_ref, g_ref, w_ref, cw_ref, cb_ref, dtb_ref, alog_ref, d_ref, ng_ref, tri_ref,
                pick_ref, conv0_ref, h0_ref,
                y_ref, convo_ref, ho_ref,
                cbuf, hbuf, xs_s, bd_s, cd_s, dt_s, dta_s, y_s, *, nb, tb, L):
    c = y_ref.shape[-1]
    nc = tb // L
    npair = c // PAIR
    nconv = cbuf.shape[-1]

    @pl.when(_first_step())
    def _():
        cbuf[:, CONV_PAD - (CONV_W - 1):CONV_PAD, :] = conv0_ref[...]
        hbuf[...] = h0_ref[...]

    x = x_ref[...].reshape(nb * tb, x_ref.shape[-1])
    p = _dot(_rms(x, g_ref[...]), w_ref[...])
    xbc = p[:, :nconv]
    dtr = p[:, nconv:nconv + c]
    z = p[:, nconv + c:]
    conv = jnp.concatenate(
        [_causal_conv(cbuf, b, xbc[b * tb:(b + 1) * tb], tb, cw_ref, cb_ref[...]) for b in range(nb)],
        axis=0)
    conv = _silu(conv)
    xs = conv[:, :c]
    nbc = (nconv - c) // 2
    dt = _softplus(dtr + dtb_ref[...])
    xs_s[...] = xs
    bd_s[...] = conv[:, c:c + nbc]
    cd_s[...] = conv[:, c + nbc:]
    dt_s[...] = dt
    dta_s[...] = dt * (-jnp.exp(alog_ref[...]))

    m0, m1 = _head_masks()
    bdm = _bd_mask()
    _, incl, _ = _cat_masks(L)
    tri = tri_ref[...]
    pick = pick_ref[...]

    def cat_cols(x):
        if 2 * L == PAIR:
            return x
        return jnp.concatenate([x[:, 0:L], x[:, HEAD:HEAD + L]], axis=1)

    grp = _group_size(nb * nc)
    chains = [(g, j) for g in range(grp) for j in range(npair)]
    lanes = lambda j: slice(j * PAIR, (j + 1) * PAIR)

    def group(i, carry):
        per = []
        for g in range(grp):
            it = i * grp + g
            rows = pl.ds(pl.multiple_of(it * L, L), L)
            acs = _dot01(tri, dta_s[rows, :])
            acs_last = acs[L - 1:L, :]
            dt_c = dt_s[rows, :]
            per.append(dict(b=it // nc, rows=rows, acs=acs, dt=dt_c, xs=xs_s[rows, :], e_acs=jnp.exp(acs),
                            dl=jnp.exp(acs_last - acs) * dt_c, tot=jnp.exp(acs_last),
                            bd=bd_s[rows, :], cd=cd_s[rows, :]))
        part = lambda key: [per[g][key][:, lanes(j)] for g, j in chains]
        group_part = lambda key: [per[g][key][:, lanes(j // 2)] for g, j in chains]
        acs_p, dt_p, xs_p, dl_p = part("acs"), part("dt"), part("xs"), part("dl")
        bd_p, cd_p = group_part("bd"), group_part("cd")
        picked = [_dot01_nt(pick, jnp.concatenate([a, d_], axis=0)) for a, d_ in zip(acs_p, dt_p)]
        cb = [_dot_nt(c_, _stack_heads(b_, m0, m1)) for c_, b_ in zip(cd_p, bd_p)]
        upd = [_dot_tn(x_ * d_, b_) for x_, d_, b_ in zip(xs_p, dl_p, bd_p)]
        scores = []
        for n in range(len(chains)):
            acs_row = jnp.concatenate([picked[n][0:1, 0:L], picked[n][1:2, 0:L]], axis=1)
            dt_row = jnp.concatenate([picked[n][0:1, L:2 * L], picked[n][1:2, L:2 * L]], axis=1)
            decay = jnp.where(incl, jnp.exp(cat_cols(acs_p[n]) - acs_row), 0.0)
            scores.append(cb[n] * decay * dt_row)
        yx = [_dot(s_, _stack_heads(x_, m0, m1)) for s_, x_ in zip(scores, xs_p)]
        for g in range(grp):
            b, rows = per[g]["b"], per[g]["rows"]
            ns = [n for n, (gg, _) in enumerate(chains) if gg == g]
            h = [hbuf[b, j] for j in range(npair)]
            chg = [_dot_nt(cd_p[n], h[j]) for j, n in enumerate(ns)]
            for j, n in enumerate(ns):
                y_s[rows, lanes(j)] = yx[n] + per[g]["e_acs"][:, lanes(j)] * chg[j]
                hbuf[b, j] = jnp.where(bdm, h[j] * per[g]["tot"][:, lanes(j)] + upd[n], 0.0)
        return carry

    lax.fori_loop(0, nb * nc // grp, group, 0)

    yc = y_s[...] + d_ref[...] * xs
    y_ref[...] = _rms(yc * _silu(z), ng_ref[...]).reshape(nb, tb, c)

    @pl.when(_last_step())
    def _():
        convo_ref[...] = cbuf[:, CONV_PAD - (CONV_W - 1):CONV_PAD, :]
        ho_ref[...] = hbuf[...]


def _hgrn_kernel(x_ref, g_ref, w_ref, lbl_ref, ng_ref, hall_ref, lmask_ref, seg_ref, s0_ref,
                 y_ref, so_ref,
                 sbuf, q_s, k_s, v_s, lf_s, y_s, *, nb, tb, L, layer):
    c = y_ref.shape[-1]
    nc = tb // L
    npair = c // PAIR
    nlev = lmask_ref.shape[0]

    @pl.when(_first_step())
    def _():
        sbuf[...] = s0_ref[...]

    x = x_ref[...].reshape(nb * tb, x_ref.shape[-1])
    p = _dot(_rms(x, g_ref[...]), w_ref[...])
    qd = p[:, 0:c]
    fd = p[:, c:2 * c]
    v = p[:, 2 * c:3 * c]
    gd = p[:, 3 * c:]
    logits = lbl_ref[...]
    ex = jnp.exp(logits - jnp.max(logits, axis=0, keepdims=True))
    sm = ex / jnp.sum(ex, axis=0, keepdims=True)
    lb = jnp.zeros_like(sm[0:1])
    for i in range(1, layer + 1):
        lb = lb + sm[i:i + 1]
    log_lb = jnp.log(lb)
    b2 = jnp.log1p(-lb) - _softplus(-fd)
    logf = jnp.maximum(log_lb, b2) + jnp.log1p(jnp.exp(-jnp.abs(log_lb - b2)))
    q = _silu(qd)
    kx = (1.0 - lb) * _sigmoid(-fd)
    q_s[...] = q
    k_s[...] = kx
    v_s[...] = v
    lf_s[...] = logf
    diag = _dot(q * kx, seg_ref[...]) * v

    m0, m1 = _head_masks()
    bdm = _bd_mask()
    hall = hall_ref[...]

    grp = _group_size(nb * nc)
    chains = [(g, j) for g in range(grp) for j in range(npair)]
    lanes = lambda j: slice(j * PAIR, (j + 1) * PAIR)

    def group(i, carry):
        per = []
        for g in range(grp):
            it = i * grp + g
            rows = pl.ds(pl.multiple_of(it * L, L), L)
            d_all = _dot01(hall, lf_s[rows, :])
            q_c = q_s[rows, :]
            k_c = k_s[rows, :]
            e_lev = [jnp.exp(d_all[(2 + m) * L:(3 + m) * L]) for m in range(nlev)]
            per.append(dict(b=it // nc, rows=rows, v=v_s[rows, :],
                            qe=q_c * jnp.exp(d_all[0:L]), kl=k_c * jnp.exp(d_all[L:2 * L]),
                            tot=jnp.exp(d_all[L - 1:L]),
                            qn=[q_c * e for e in e_lev], kn=[k_c * e for e in e_lev]))
        part = lambda key: [per[g][key][:, lanes(j)] for g, j in chains]
        vp, qe_p, kl_p = part("v"), part("qe"), part("kl")
        att = [jnp.zeros((L, 2 * L), F32) for _ in chains]
        for m in range(nlev):
            lev = [_dot_nt(per[g]["qn"][m][:, lanes(j)], _stack_heads(per[g]["kn"][m][:, lanes(j)], m0, m1))
                   for g, j in chains]
            att = [a + lmask_ref[m] * t for a, t in zip(att, lev)]
        yv = [_dot(a, _stack_heads(v_, m0, m1)) for a, v_ in zip(att, vp)]
        upd = [_dot_tn(v_, k_) for v_, k_ in zip(vp, kl_p)]
        for g in range(grp):
            b, rows = per[g]["b"], per[g]["rows"]
            ns = [n for n, (gg, _) in enumerate(chains) if gg == g]
            st = [sbuf[b, j] for j in range(npair)]
            ys = [_dot_nt(qe_p[n], st[j]) for j, n in enumerate(ns)]
            for j, n in enumerate(ns):
                y_s[rows, lanes(j)] = ys[j] + yv[n]
                sbuf[b, j] = jnp.where(bdm, st[j] * per[g]["tot"][:, lanes(j)] + upd[n], 0.0)
        return carry

    lax.fori_loop(0, nb * nc // grp, group, 0)

    o = y_s[...] + diag
    ms = _dot(o * o, seg_ref[...]) * (1.0 / HEAD)
    yd = o * lax.rsqrt(ms + NORM_EPS) * ng_ref[...]
    y_ref[...] = (yd * _silu(gd)).reshape(nb, tb, c)

    @pl.when(_last_step())
    def _():
        so_ref[...] = sbuf[...]


def _post_kernel(ya_ref, yb_ref, yc_ref, yd_ref, x_ref, wo_ref, gpost_ref, gprex_ref, wq_ref,
                 mk_ref, mv_ref, wox_ref, gpostx_ref, o_ref, *, nb, tb, heads):
    d = x_ref.shape[-1]
    c = ya_ref.shape[-1]
    hd = d // heads
    rows = nb * tb
    y = None
    for i, ref in enumerate((ya_ref, yb_ref, yc_ref, yd_ref)):
        t = _dot(ref[...].reshape(rows, c), wo_ref[i * c:(i + 1) * c, :])
        y = t if y is None else y + t
    x1 = x_ref[...].reshape(rows, d) + _rms(y, gpost_ref[...])
    q = _dot(_rms(x1, gprex_ref[...]), wq_ref[...])
    scale = hd ** -0.5
    outs = []
    for b in range(nb):
        qb = q[b * tb:(b + 1) * tb]
        heads_o = []
        for h in range(heads):
            hs = slice(h * hd, (h + 1) * hd)
            s = _dot_nt(qb[:, hs], mk_ref[b, :, hs]) * scale
            s = s - jnp.max(s, axis=-1, keepdims=True)
            e = jnp.exp(s)
            pr = e / jnp.sum(e, axis=-1, keepdims=True)
            heads_o.append(_dot(pr, mv_ref[b, :, hs]))
        outs.append(jnp.concatenate(heads_o, axis=1))
    o = jnp.concatenate(outs, axis=0)
    x2 = x1 + _rms(_dot(o, wox_ref[...]), gpostx_ref[...])
    o_ref[...] = x2.reshape(nb, tb, d)


def _memkv_kernel(m_ref, g_ref, wk_ref, wv_ref, k_ref, v_ref):
    m = _rms(m_ref[0], g_ref[...])
    k_ref[0] = _dot(m, wk_ref[...])
    v_ref[0] = _dot(m, wv_ref[...])


def _full(shape):
    nd = len(shape)
    return pl.BlockSpec(shape, lambda b, t: (0,) * nd)


def _per_batch(shape_tail, nb):
    nd = len(shape_tail)
    return pl.BlockSpec((nb,) + shape_tail, lambda b, t: (b,) + (0,) * nd)


def _tokens(nb, tb, width):
    return pl.BlockSpec((nb, tb, width), lambda b, t: (b, t, 0))


def _params():
    return pltpu.CompilerParams(dimension_semantics=("arbitrary", "arbitrary"),
                                vmem_limit_bytes=VMEM_LIMIT)


def _call(kern, name, grid, in_arrays, in_specs, out_shapes, out_specs, scratch):
    return pl.pallas_call(
        kern, name=name, grid=grid, in_specs=in_specs, out_specs=out_specs,
        out_shape=out_shapes, scratch_shapes=scratch, compiler_params=_params())(*in_arrays)


def _row(v):
    return v.reshape(1, -1).astype(F32)


def _lru_call(x, lp, conv0, h0, nb, tb):
    B, T, D = x.shape
    c = lp["lru_cw"].shape[-1]
    grid = (B // nb, T // tb)
    ins = [x, lp["g_pre"], lp["w_a"], lp["lru_cw"], lp["lru_cb"], lp["lru_wg"], lp["lru_bg"], lp["lru_lam"],
           conv0, h0]
    specs = [_tokens(nb, tb, D)] + [_full(a.shape) for a in ins[1:8]] + [
        _per_batch((CONV_W - 1, c), nb), _per_batch((1, c), nb)]
    outs = [jax.ShapeDtypeStruct((B, T, c), F32), jax.ShapeDtypeStruct((B, CONV_W - 1, c), F32),
            jax.ShapeDtypeStruct((B, 1, c), F32)]
    ospecs = [_tokens(nb, tb, c), _per_batch((CONV_W - 1, c), nb), _per_batch((1, c), nb)]
    scratch = [pltpu.VMEM((nb, tb + CONV_PAD, c), F32), pltpu.VMEM((nb, 1, c), F32)]
    return _call(functools.partial(_lru_kernel, nb=nb, tb=tb), "mix_lru", grid, ins, specs, outs, ospecs, scratch)


def _rwkv_call(x, lp, consts, shift0, s0, nb, tb, L):
    B, T, D = x.shape
    c = lp["rw_w0"].shape[-1]
    nsh = shift0.shape[-1]
    npair = c // PAIR
    grid = (B // nb, T // tb)
    ins = [x, lp["g_pre"], lp["w_b"], lp["rw_mu"], lp["rw_w0"], lp["rw_a0"], lp["rw_wup"], lp["rw_kk"],
           lp["rw_ka"], lp["rw_rk"], lp["rw_lng"], lp["rw_lnb"], consts["seg"], consts["tri"], shift0, s0]
    specs = [_tokens(nb, tb, D)] + [_full(a.shape) for a in ins[1:14]] + [
        _per_batch((1, nsh), nb), _per_batch((npair, PAIR, PAIR), nb)]
    outs = [jax.ShapeDtypeStruct((B, T, c), F32), jax.ShapeDtypeStruct((B, 1, nsh), F32),
            jax.ShapeDtypeStruct((B, npair, PAIR, PAIR), F32)]
    ospecs = [_tokens(nb, tb, c), _per_batch((1, nsh), nb), _per_batch((npair, PAIR, PAIR), nb)]
    rows = nb * tb
    scratch = [pltpu.VMEM((nb, 1, nsh), F32), pltpu.VMEM((nb, npair, PAIR, PAIR), F32)] + [
        pltpu.VMEM((rows, c), F32) for _ in range(7)]
    return _call(functools.partial(_rwkv_kernel, nb=nb, tb=tb, L=L), "mix_rwkv", grid, ins, specs, outs,
                 ospecs, scratch)


def _ssd_call(x, lp, consts, conv0, h0, nb, tb, L):
    B, T, D = x.shape
    c = lp["ssd_dtb"].shape[-1]
    nconv = lp["ssd_cw"].shape[-1]
    nbc = (nconv - c) // 2
    npair = c // PAIR
    grid = (B // nb, T // tb)
    ins = [x, lp["g_pre"], lp["w_c"], lp["ssd_cw"], lp["ssd_cb"], lp["ssd_dtb"], lp["ssd_alog"], lp["ssd_d"],
           lp["ssd_ng"], consts["tri"], consts["pick"], conv0, h0]
    specs = [_tokens(nb, tb, D)] + [_full(a.shape) for a in ins[1:11]] + [
        _per_batch((CONV_W - 1, nconv), nb), _per_batch((npair, PAIR, PAIR), nb)]
    outs = [jax.ShapeDtypeStruct((B, T, c), F32), jax.ShapeDtypeStruct((B, CONV_W - 1, nconv), F32),
            jax.ShapeDtypeStruct((B, npair, PAIR, PAIR), F32)]
    ospecs = [_tokens(nb, tb, c), _per_batch((CONV_W - 1, nconv), nb), _per_batch((npair, PAIR, PAIR), nb)]
    rows = nb * tb
    scratch = [pltpu.VMEM((nb, tb + CONV_PAD, nconv), F32), pltpu.VMEM((nb, npair, PAIR, PAIR), F32),
               pltpu.VMEM((rows, c), F32), pltpu.VMEM((rows, nbc), F32), pltpu.VMEM((rows, nbc), F32),
               pltpu.VMEM((rows, c), F32), pltpu.VMEM((rows, c), F32), pltpu.VMEM((rows, c), F32)]
    return _call(functools.partial(_ssd_kernel, nb=nb, tb=tb, L=L), "mix_ssd", grid, ins, specs, outs,
                 ospecs, scratch)


def _hgrn_call(x, lp, consts, s0, nb, tb, L, layer):
    B, T, D = x.shape
    c = lp["hg_ng"].shape[-1]
    npair = c // PAIR
    grid = (B // nb, T // tb)
    ins = [x, lp["g_pre"], lp["w_d"], lp["hg_lbl"], lp["hg_ng"], consts["hall"], consts["lmask"], consts["seg"], s0]
    specs = [_tokens(nb, tb, D)] + [_full(a.shape) for a in ins[1:8]] + [_per_batch((npair, PAIR, PAIR), nb)]
    outs = [jax.ShapeDtypeStruct((B, T, c), F32), jax.ShapeDtypeStruct((B, npair, PAIR, PAIR), F32)]
    ospecs = [_tokens(nb, tb, c), _per_batch((npair, PAIR, PAIR), nb)]
    rows = nb * tb
    scratch = [pltpu.VMEM((nb, npair, PAIR, PAIR), F32)] + [pltpu.VMEM((rows, c), F32) for _ in range(5)]
    return _call(functools.partial(_hgrn_kernel, nb=nb, tb=tb, L=L, layer=layer), "mix_hgrn", grid, ins, specs,
                 outs, ospecs, scratch)


def _post_call(ys, x, lp, mk, mv, nb, tb, heads):
    B, T, D = x.shape
    c = ys[0].shape[-1]
    M = mk.shape[1]
    grid = (B // nb, T // tb)
    ins = list(ys) + [x, lp["w_out"], lp["g_post"], lp["g_pre_x"], lp["w_q"], mk, mv, lp["w_o"], lp["g_post_x"]]
    specs = [_tokens(nb, tb, c)] * 4 + [_tokens(nb, tb, D)] + [_full(a.shape) for a in ins[5:9]] + [
        _per_batch((M, D), nb), _per_batch((M, D), nb)] + [_full(a.shape) for a in ins[11:13]]
    return _call(functools.partial(_post_kernel, nb=nb, tb=tb, heads=heads), "post_attn", grid, ins, specs,
                 jax.ShapeDtypeStruct((B, T, D), F32), _tokens(nb, tb, D), [])


def _memkv_call(mem, g, wk, wv):
    B, M, D = mem.shape
    spec = pl.BlockSpec((1, M, D), lambda b: (b, 0, 0))
    wspec = pl.BlockSpec((D, D), lambda b: (0, 0))
    return pl.pallas_call(
        _memkv_kernel, name="mem_kv", grid=(B,),
        in_specs=[spec, pl.BlockSpec((1, D), lambda b: (0, 0)), wspec, wspec],
        out_specs=[spec, spec],
        out_shape=[jax.ShapeDtypeStruct((B, M, D), F32)] * 2,
        compiler_params=pltpu.CompilerParams(dimension_semantics=("arbitrary",),
                                             vmem_limit_bytes=VMEM_LIMIT))(mem, g, wk, wv)


def _chunk_consts(L, c):
    r = np.arange(L)[:, None]
    j = np.arange(L)[None, :]
    tri = (j <= r).astype(np.float32)
    last = (j > r).astype(np.float32)
    secs = [tri, last]
    masks = []
    t = np.arange(L)[:, None]
    col = np.arange(2 * L)[None, :]
    s = col % L
    m = 1
    while (1 << m) <= L:
        size, half = 1 << m, 1 << (m - 1)
        mid = (r // size) * size + half - 1
        secs.append(np.where(r > mid, (j > mid) & (j <= r), (j > r) & (j <= mid)).astype(np.float32))
        masks.append(((t // size == s // size) & (t % size >= half) & (s % size < half)).astype(np.float32))
        m += 1
    seg = (np.arange(c)[:, None] // HEAD == np.arange(c)[None, :] // HEAD).astype(np.float32)
    pick = np.zeros((8, PAIR), np.float32)
    pick[0, 0] = 1.0
    pick[1, HEAD] = 1.0
    return {
        "tri": jnp.asarray(tri, BF16),
        "hall": jnp.asarray(np.concatenate(secs, axis=0), BF16),
        "lmask": jnp.asarray(np.stack(masks), F32),
        "seg": jnp.asarray(seg, BF16),
        "pick": jnp.asarray(pick, BF16),
    }


def _to_pairs(s):
    B, H, a, b = s.shape
    s = s.reshape(B, H // 2, 2, a, b)
    z = jnp.zeros_like(s[:, :, 0])
    top = jnp.concatenate([s[:, :, 0], z], axis=-1)
    bot = jnp.concatenate([z, s[:, :, 1]], axis=-1)
    return jnp.concatenate([top, bot], axis=-2)


def _from_pairs(s):
    B, P = s.shape[:2]
    return jnp.stack([s[:, :, :HEAD, :HEAD], s[:, :, HEAD:, HEAD:]], axis=2).reshape(B, 2 * P, HEAD, HEAD)


def _expand_bc(t, c, n):
    xs, bm, cm = t[..., :c], t[..., c:c + 2 * n], t[..., c + 2 * n:]
    dup = lambda u: jnp.concatenate([u[..., :n], u[..., :n], u[..., n:], u[..., n:]], axis=-1)
    return jnp.concatenate([xs, dup(bm), dup(cm)], axis=-1)


def _shrink_bc(t, c, n):
    pick = lambda u: jnp.concatenate([u[..., :n], u[..., 2 * n:3 * n]], axis=-1)
    return jnp.concatenate([t[..., :c], pick(t[..., c:c + 4 * n]), pick(t[..., c + 4 * n:])], axis=-1)


def _blockdiag(w):
    n, d, e = w.shape
    eye = jnp.eye(n, dtype=w.dtype)
    return (eye[:, None, :, None] * w[:, :, None, :]).reshape(n * d, n * e)


def kernel(x_prompt, x_sample, state_lru_conv, state_lru_h, state_rwkv_shift, state_rwkv_wkv, state_ssd_conv, state_ssd_h, state_hgrn_s, cache_mem_k, cache_mem_v, mem_prompt, g_pre, g_post, g_pre_x, g_post_x, w_in, w_out, lru_conv_w, lru_conv_b, lru_w_r, lru_b_r, lru_w_i, lru_b_i, lru_lambda, rwkv_mu, rwkv_w0, rwkv_w_up, rwkv_a0, rwkv_a_up, rwkv_k_k, rwkv_k_a, rwkv_r_k, rwkv_ln_g, rwkv_ln_b, ssd_conv_w, ssd_conv_b, ssd_dt_bias, ssd_a_log, ssd_d, ssd_norm_g, hgrn_lb_logits, hgrn_norm_g, mem_g, mem_w_q, mem_w_k, mem_w_v, mem_w_o):
    depth = w_in.shape[0]
    D = x_prompt.shape[-1]
    c = lru_conv_w.shape[-1]
    nsh = rwkv_mu.shape[-1]
    rank = (nsh - 3 * c) // 2
    ssd_heads = ssd_dt_bias.shape[-1]
    nstate = (ssd_conv_w.shape[-1] - c) // 4
    heads_x = cache_mem_k.shape[-2]
    mem_len = mem_prompt.shape[1]

    o_b = 2 * c
    o_c = o_b + nsh + c
    o_dt = o_c + c + 4 * nstate
    o_z = o_dt + ssd_heads
    o_d = o_z + c

    layers = []
    for l in range(depth):
        w = w_in[l]
        zero = jnp.zeros((rank, c), F32)
        wup = jnp.concatenate([jnp.concatenate([rwkv_w_up[l], zero], axis=1),
                               jnp.concatenate([zero, rwkv_a_up[l]], axis=1)], axis=0)
        rep = lambda v: jnp.repeat(v, c // ssd_heads, axis=-1)
        w_c = jnp.concatenate([_expand_bc(w[:, o_c:o_dt], c, nstate), rep(w[:, o_dt:o_z]), w[:, o_z:o_d]], axis=1)
        layers.append({
            "g_pre": _row(g_pre[l]), "g_post": _row(g_post[l]), "g_pre_x": _row(g_pre_x[l]),
            "g_post_x": _row(g_post_x[l]),
            "w_a": w[:, :o_b].astype(BF16), "w_b": w[:, o_b:o_c].astype(BF16), "w_c": w_c.astype(BF16),
            "w_d": w[:, o_d:].astype(BF16),
            "w_out": w_out[l].astype(BF16), "w_q": mem_w_q[l].astype(BF16), "w_o": mem_w_o[l].astype(BF16),
            "lru_cw": lru_conv_w[l], "lru_cb": _row(lru_conv_b[l]),
            "lru_wg": jnp.concatenate([_blockdiag(lru_w_r[l]), _blockdiag(lru_w_i[l])], axis=1).astype(BF16),
            "lru_bg": _row(jnp.concatenate([lru_b_r[l], lru_b_i[l]])), "lru_lam": _row(lru_lambda[l]),
            "rw_mu": _row(rwkv_mu[l]), "rw_w0": _row(rwkv_w0[l]), "rw_a0": _row(rwkv_a0[l]),
            "rw_wup": wup.astype(BF16), "rw_kk": _row(rwkv_k_k[l]), "rw_ka": _row(rwkv_k_a[l]),
            "rw_rk": _row(rwkv_r_k[l]), "rw_lng": _row(rwkv_ln_g[l]), "rw_lnb": _row(rwkv_ln_b[l]),
            "ssd_cw": _expand_bc(ssd_conv_w[l], c, nstate), "ssd_cb": _row(_expand_bc(ssd_conv_b[l], c, nstate)),
            "ssd_dtb": _row(rep(ssd_dt_bias[l])), "ssd_alog": _row(rep(ssd_a_log[l])), "ssd_d": _row(rep(ssd_d[l])),
            "ssd_ng": _row(ssd_norm_g[l]),
            "hg_lbl": hgrn_lb_logits.astype(F32), "hg_ng": _row(hgrn_norm_g[l]),
        })

    def run(x, mk, mv, conv_a, h_a, shift_b, wkv_b, conv_c, h_c, s_d, nb, tb, L):
        consts = _chunk_consts(L, c)
        B = x.shape[0]
        acc = [[] for _ in range(7)]
        for l in range(depth):
            lp = layers[l]
            ya, nca, nha = _lru_call(x, lp, conv_a[l], h_a[l].reshape(B, 1, c), nb, tb)
            yb, nsb, nwb = _rwkv_call(x, lp, consts, shift_b[l].reshape(B, 1, nsh), _to_pairs(wkv_b[l]), nb, tb, L)
            yc, ncc, nhc = _ssd_call(x, lp, consts, _expand_bc(conv_c[l], c, nstate), _to_pairs(h_c[l]), nb, tb, L)
            yd, nsd = _hgrn_call(x, lp, consts, _to_pairs(jnp.swapaxes(s_d[l], -1, -2)), nb, tb, L, l)
            x = _post_call((ya, yb, yc, yd), x, lp, mk[l].reshape(B, mem_len, D), mv[l].reshape(B, mem_len, D),
                           nb, tb, heads_x)
            for lst, val in zip(acc, (nca, nha.reshape(B, c), nsb.reshape(B, nsh), _from_pairs(nwb),
                                      _shrink_bc(ncc, c, nstate), _from_pairs(nhc),
                                      jnp.swapaxes(_from_pairs(nsd), -1, -2))):
                lst.append(val)
        return (x,) + tuple(jnp.stack(v) for v in acc)

    Bp, Tp = x_prompt.shape[:2]
    Bs, Ts = x_sample.shape[:2]
    kv = [_memkv_call(mem_prompt, _row(mem_g[l]), mem_w_k[l].astype(BF16), mem_w_v[l].astype(BF16))
          for l in range(depth)]
    kv_shape = (Bp, mem_len, heads_x, D // heads_x)
    mem_k_p = jnp.stack([t[0].reshape(kv_shape) for t in kv])
    mem_v_p = jnp.stack([t[1].reshape(kv_shape) for t in kv])
    zeros = lambda *s: jnp.zeros((depth, Bp) + s, F32)
    Lp = CHUNK if Tp % CHUNK == 0 else Tp
    Ls = CHUNK if Ts % CHUNK == 0 else Ts
    tb_p = 4 * Lp if Tp % (4 * Lp) == 0 else Lp
    nb_s = 8 if Bs % 8 == 0 else 1
    (y_prompt, lru_conv_p, lru_h_p, rwkv_shift_p, rwkv_wkv_p, ssd_conv_p, ssd_h_p, hgrn_s_p) = run(
        x_prompt, mem_k_p, mem_v_p,
        zeros(CONV_W - 1, c), zeros(c), zeros(nsh), zeros(c // HEAD, HEAD, HEAD),
        zeros(CONV_W - 1, c + 4 * nstate), zeros(ssd_heads, HEAD, nstate), zeros(c // HEAD, HEAD, HEAD),
        1, tb_p, Lp)
    (y_sample, lru_conv_s, lru_h_s, rwkv_shift_s, rwkv_wkv_s, ssd_conv_s, ssd_h_s, hgrn_s_s) = run(
        x_sample, cache_mem_k, cache_mem_v, state_lru_conv, state_lru_h, state_rwkv_shift, state_rwkv_wkv,
        state_ssd_conv, state_ssd_h, state_hgrn_s, nb_s, Ts, Ls)
    return (y_prompt, y_sample, lru_conv_p, lru_conv_s, lru_h_p, lru_h_s, rwkv_shift_p, rwkv_shift_s,
            rwkv_wkv_p, rwkv_wkv_s, ssd_conv_p, ssd_conv_s, ssd_h_p, ssd_h_s, hgrn_s_p, hgrn_s_s,
            mem_k_p, mem_v_p)
```

```python
import functools

import numpy as np
import jax
import jax.numpy as jnp
from jax import lax
from jax.experimental import pallas as pl
from jax.experimental.pallas import tpu as pltpu

F32 = jnp.float32
BF16 = jnp.bfloat16

SUBLANES = 8
HEAD = 64
PAIR = 2 * HEAD
SUB = 16
SUB_SHIFT = 4
CHUNK = 64
CONV_W = 4
CONV_PAD = 8
LRU_C = 8.0
RWKV_DECAY_SCALE = 0.6065306597126334
RWKV_LN_EPS = 64e-5
NORM_EPS = 1e-6
VMEM_LIMIT = 56 * 1024 * 1024
PROMPT_CHUNKS = 8
SAMPLE_BATCH = 8


def _dot(a, b):
    return jnp.dot(a.astype(BF16), b.astype(BF16), preferred_element_type=F32)


def _dot_nt(a, b):
    return lax.dot_general(a.astype(BF16), b.astype(BF16), (((1,), (1,)), ((), ())),
                           preferred_element_type=F32)


def _dot_tn(a, b):
    return lax.dot_general(a.astype(BF16), b.astype(BF16), (((0,), (0,)), ((), ())),
                           preferred_element_type=F32)


def _split3(x):
    hi = x.astype(BF16)
    r1 = x - hi.astype(F32)
    mid = r1.astype(BF16)
    lo = (r1 - mid.astype(F32)).astype(BF16)
    return hi, mid, lo


def _split2(x):
    hi = x.astype(BF16)
    return hi, (x - hi.astype(F32)).astype(BF16)


def _dot3(a, b):
    ah, al = _split2(a)
    bh, bl = _split2(b)
    d = lambda u, w: jnp.dot(u, w, preferred_element_type=F32)
    return d(ah, bh) + d(ah, bl) + d(al, bh)


def _dot3_nt(a, b):
    ah, al = _split2(a)
    bh, bl = _split2(b)
    d = lambda u, w: lax.dot_general(u, w, (((1,), (1,)), ((), ())), preferred_element_type=F32)
    return d(ah, bh) + d(ah, bl) + d(al, bh)


def _group_size(n):
    return 2 if n % 2 == 0 else 1


def _dot01(m01, x):
    hi, mid, lo = _split3(x)
    d = lambda p: jnp.dot(m01, p, preferred_element_type=F32)
    return d(hi) + d(mid) + d(lo)


def _dot01_nt(m01, x):
    hi, mid, lo = _split3(x)
    d = lambda p: lax.dot_general(m01, p, (((1,), (1,)), ((), ())), preferred_element_type=F32)
    return d(hi) + d(mid) + d(lo)


def _rms(x, g):
    return x * lax.rsqrt(jnp.mean(x * x, axis=-1, keepdims=True) + NORM_EPS) * g


def _sigmoid(x):
    return jax.nn.sigmoid(x)


def _silu(x):
    return x * jax.nn.sigmoid(x)


def _softplus(x):
    return jnp.maximum(x, 0.0) + jnp.log1p(jnp.exp(-jnp.abs(x)))


def _neg_expm1(z):
    return -jnp.tanh(0.5 * z) * (jnp.exp(z) + 1.0)


def _head_masks():
    lane = lax.broadcasted_iota(jnp.int32, (1, PAIR), 1)
    return lane < HEAD, lane >= HEAD


def _stack_heads(x, m0, m1):
    return jnp.concatenate([jnp.where(m0, x, 0.0), jnp.where(m1, x, 0.0)], axis=0)


def _bd_mask():
    r = lax.broadcasted_iota(jnp.int32, (PAIR, PAIR), 0)
    c = lax.broadcasted_iota(jnp.int32, (PAIR, PAIR), 1)
    return (r >= HEAD) == (c >= HEAD)


def _cat_masks(L):
    t = lax.broadcasted_iota(jnp.int32, (L, 2 * L), 0)
    c = lax.broadcasted_iota(jnp.int32, (L, 2 * L), 1)
    s = jnp.where(c >= L, c - L, c)
    return s < t, s <= t, c < L


def _cat_to_blockdiag(m, first):
    return jnp.concatenate([jnp.where(first, m, 0.0), jnp.where(first, 0.0, m)], axis=0)


def _tri_inv(nns, L):
    n = nns[0].shape[0]
    ri = lax.broadcasted_iota(jnp.int32, (n, n), 0)
    ci = lax.broadcasted_iota(jnp.int32, (n, n), 1)
    eye = (ri == ci).astype(F32)
    diag = (ri >> SUB_SHIFT) == (ci >> SUB_SHIFT)
    nd = [jnp.where(diag, t, 0.0) for t in nns]
    no = [t - d for t, d in zip(nns, nd)]
    td = [eye + d for d in nd]
    p = nd
    k = 2
    while k < SUB:
        p = [_dot3(t, t) for t in p]
        td = [t + _dot3(t, q) for t, q in zip(td, p)]
        k *= 2
    w = [_dot(t, o) for t, o in zip(td, no)]
    tm = [eye + t for t in w]
    pw = w
    k = 2
    while k < L // SUB:
        pw = [_dot(t, t) for t in pw]
        tm = [t + _dot(t, q) for t, q in zip(tm, pw)]
        k *= 2
    return [_dot(a, b) for a, b in zip(tm, td)]


def _mid_rows(b, m):
    n, c = b.shape
    size, half = 1 << m, 1 << (m - 1)
    if size >= 2 * SUBLANES:
        return jnp.concatenate(
            [jnp.broadcast_to(b[s + half - 1:s + half, :], (size, c)) for s in range(0, n, size)], axis=0)
    b3 = b.reshape(n // SUBLANES, SUBLANES, c)
    sub = lax.broadcasted_iota(jnp.int32, b3.shape, 1)
    mids = list(range(half - 1, SUBLANES, size))
    out = jnp.broadcast_to(b3[:, mids[-1]:mids[-1] + 1, :], b3.shape)
    for mid in reversed(mids[:-1]):
        out = jnp.where(sub <= mid + half, b3[:, mid:mid + 1, :], out)
    return out.reshape(n, c)


def _shift_rows(x, d, fill):
    row = lax.broadcasted_iota(jnp.int32, x.shape, 0)
    return jnp.where(row >= d, pltpu.roll(x, d, axis=0), fill)


def _affine_scan(a, u):
    n = a.shape[0]
    d = 1
    while d < n:
        a_s = _shift_rows(a, d, 1.0)
        u_s = _shift_rows(u, d, 0.0)
        u = a * u_s + u
        a = a * a_s
        d *= 2
    return a, u


def _causal_conv(cbuf, b, xa, tb, w_ref, bias):
    cbuf[b, CONV_PAD:CONV_PAD + tb, :] = xa
    y = bias + xa * w_ref[CONV_W - 1:CONV_W, :]
    for j in range(1, CONV_W):
        y = y + cbuf[b, CONV_PAD - j:CONV_PAD - j + tb, :] * w_ref[CONV_W - 1 - j:CONV_W - j, :]
    hist = cbuf[b, tb + CONV_PAD - (CONV_W - 1):tb + CONV_PAD, :]
    cbuf[b, CONV_PAD - (CONV_W - 1):CONV_PAD, :] = hist
    return y


def _first_step():
    return pl.program_id(1) == 0


def _last_step():
    return pl.program_id(1) == pl.num_programs(1) - 1


def _lru_kernel(x_ref, g_ref, w_ref, cw_ref, cb_ref, wg_ref, bg_ref, lam_ref, conv0_ref, h0_ref,
                y_ref, convo_ref, ho_ref, cbuf, hbuf, *, nb, tb):
    c = y_ref.shape[-1]

    @pl.when(_first_step())
    def _():
        cbuf[:, CONV_PAD - (CONV_W - 1):CONV_PAD, :] = conv0_ref[...]
        hbuf[...] = h0_ref[...]

    x = x_ref[...].reshape(nb * tb, x_ref.shape[-1])
    p = _dot(_rms(x, g_ref[...]), w_ref[...])
    xa = p[:, :c]
    ga = p[:, c:]
    xc = jnp.concatenate(
        [_causal_conv(cbuf, b, xa[b * tb:(b + 1) * tb], tb, cw_ref, cb_ref[...]) for b in range(nb)],
        axis=0)
    gates = _dot(xc, wg_ref[...]) + bg_ref[...]
    r = _sigmoid(gates[:, :c])
    i = _sigmoid(gates[:, c:])
    log_a = (-LRU_C) * r * _softplus(-lam_ref[...])
    a = jnp.exp(log_a)
    u = jnp.sqrt(_neg_expm1(2.0 * log_a)) * (i * xc)
    hs = []
    for b in range(nb):
        a_c, h_c = _affine_scan(a[b * tb:(b + 1) * tb], u[b * tb:(b + 1) * tb])
        h = h_c + a_c * hbuf[b]
        hbuf[b] = h[tb - 1:tb, :]
        hs.append(h)
    h = jnp.concatenate(hs, axis=0)
    y_ref[...] = (h * _silu(ga)).reshape(nb, tb, c)

    @pl.when(_last_step())
    def _():
        convo_ref[...] = cbuf[:, CONV_PAD - (CONV_W - 1):CONV_PAD, :]
        ho_ref[...] = hbuf[...]


def _rwkv_kernel(x_ref, g_ref, w_ref, mu_ref, w0_ref, a0_ref, wup_ref, kk_ref, ka_ref, rk_ref,
                 lng_ref, lnb_ref, seg_ref, tri_ref, shift0_ref, s0_ref,
                 y_ref, shifto_ref, so_ref,
                 shbuf, sbuf, lw_s, kk_s, kka_s, kp_s, r_s, v_s, y_s, *, nb, tb, L):
    c = y_ref.shape[-1]
    nc = tb // L
    npair = c // PAIR

    @pl.when(_first_step())
    def _():
        shbuf[...] = shift0_ref[...]
        sbuf[...] = s0_ref[...]

    x = x_ref[...].reshape(nb * tb, x_ref.shape[-1])
    p = _dot(_rms(x, g_ref[...]), w_ref[...])
    nsh = shbuf.shape[-1]
    pb = p[:, :nsh]
    gb = p[:, nsh:]
    xms = []
    for b in range(nb):
        slab = pb[b * tb:(b + 1) * tb]
        sh = _shift_rows(slab, 1, shbuf[b])
        shbuf[b] = slab[tb - 1:tb, :]
        xms.append(slab + (sh - slab) * mu_ref[...])
    xm = jnp.concatenate(xms, axis=0)
    r = xm[:, 0:c]
    k = xm[:, c:2 * c]
    v = xm[:, 2 * c:3 * c]
    lowrank = xm[:, 3 * c:]
    lane = lax.broadcasted_iota(jnp.int32, (1, lowrank.shape[-1]), 1)
    lowrank = jnp.where(lane < lowrank.shape[-1] // 2, jnp.tanh(lowrank), lowrank)
    up = _dot(lowrank, wup_ref[...])
    lw = (-RWKV_DECAY_SCALE) * _sigmoid(w0_ref[...] + up[:, :c])
    a = _sigmoid(a0_ref[...] + up[:, c:])
    kk = k * kk_ref[...]
    kk = kk * lax.rsqrt(_dot(kk * kk, seg_ref[...]) + 1e-12)
    kp = k * (1.0 + (a - 1.0) * ka_ref[...])
    lw_s[...] = lw
    kk_s[...] = kk
    kka_s[...] = kk * a
    kp_s[...] = kp
    r_s[...] = r
    v_s[...] = v
    bonus = _dot(r * kp * rk_ref[...], seg_ref[...]) * v

    m0, m1 = _head_masks()
    bd = _bd_mask()
    strict, incl, first = _cat_masks(L)
    tri = tri_ref[...]

    grp = _group_size(nb * nc)
    chains = [(g, j) for g in range(grp) for j in range(npair)]
    lanes = lambda j: slice(j * PAIR, (j + 1) * PAIR)

    def group(i, carry):
        per = []
        for g in range(grp):
            it = i * grp + g
            rows = pl.ds(pl.multiple_of(it * L, L), L)
            lw_c = lw_s[rows, :]
            cl = _dot01(tri, lw_c)
            cl_last = cl[L - 1:L, :]
            e_ip = jnp.exp(-cl)
            e_rel = jnp.exp(cl_last - cl)
            kk_c = kk_s[rows, :]
            kka_c = kka_s[rows, :]
            kp_c = kp_s[rows, :]
            per.append(dict(
                b=it // nc, rows=rows,
                ab=-kk_c * jnp.exp(cl - lw_c), bb=kka_c * e_ip, kb=kp_c * e_ip,
                rb=r_s[rows, :] * jnp.exp(cl), bbl=kka_c * e_rel, kbl=kp_c * e_rel,
                p_last=jnp.exp(cl_last), v=v_s[rows, :]))
        part = lambda key: [per[g][key][:, lanes(j)] for g, j in chains]
        abp, rbp, vp = part("ab"), part("rb"), part("v")
        bblp, kblp = part("bbl"), part("kbl")
        zb = [_stack_heads(t, m0, m1) for t in part("bb")]
        zk = [_stack_heads(t, m0, m1) for t in part("kb")]
        a_b = [_dot3_nt(a, z) for a, z in zip(abp, zb)]
        g1 = [_dot_nt(jnp.concatenate([a, r], axis=0), jnp.concatenate([b_, k_], axis=0))
              for a, r, b_, k_ in zip(abp, rbp, zb, zk)]
        nn = [_cat_to_blockdiag(jnp.where(strict, t, 0.0), first) for t in a_b]
        mk = [_cat_to_blockdiag(jnp.where(strict, t[0:L, 2 * L:4 * L], 0.0), first) for t in g1]
        r_b = [jnp.where(incl, t[L:2 * L, 0:2 * L], 0.0) for t in g1]
        r_k = [jnp.where(incl, t[L:2 * L, 2 * L:4 * L], 0.0) for t in g1]
        tinv = _tri_inv(nn, L)
        vst = [_stack_heads(t, m0, m1) for t in vp]
        mkv = [_dot(m_, v_) for m_, v_ in zip(mk, vst)]
        ykv = [_dot(m_, v_) for m_, v_ in zip(r_k, vst)]
        skv = [_dot_tn(v_, k_) for v_, k_ in zip(vp, kblp)]
        for g in range(grp):
            ns = [n for n, (gg, _) in enumerate(chains) if gg == g]
            b, rows, p_last = per[g]["b"], per[g]["rows"], per[g]["p_last"]
            s = [sbuf[b, j] for j in range(npair)]
            a_s = [_dot_nt(abp[n], s[j]) for j, n in enumerate(ns)]
            r_s_ = [_dot_nt(rbp[n], s[j]) for j, n in enumerate(ns)]
            ust = [_dot(tinv[n], _stack_heads(a_s[j], m0, m1) + mkv[n]) for j, n in enumerate(ns)]
            yv = [r_s_[j] + _dot(r_b[n], ust[j]) + ykv[n] for j, n in enumerate(ns)]
            su = [_dot_tn(ust[j][0:L] + ust[j][L:2 * L], bblp[n]) for j, n in enumerate(ns)]
            for j, n in enumerate(ns):
                s_new = s[j] * p_last[:, lanes(j)] + su[j] + skv[n]
                sbuf[b, j] = jnp.where(bd, s_new, 0.0)
                y_s[rows, lanes(j)] = yv[j]
        return carry

    lax.fori_loop(0, nb * nc // grp, group, 0)

    y = y_s[...]
    inv = 1.0 / HEAD
    mean = _dot(y, seg_ref[...]) * inv
    yc = y - mean
    var = _dot(yc * yc, seg_ref[...]) * inv
    yn = yc * lax.rsqrt(var + RWKV_LN_EPS) * lng_ref[...] + lnb_ref[...]
    y_ref[...] = ((yn + bonus) * _silu(gb)).reshape(nb, tb, c)

    @pl.when(_last_step())
    def _():
        shifto_ref[...] = shbuf[...]
        so_ref[...] = sbuf[...]


def _ssd_kernel(x_ref, g_ref, w_ref, cw_ref, cb_ref, dtb_ref, alog_ref, d_ref, ng_ref, tri_ref,
                pick_ref, conv0_ref, h0_ref,
                y_ref, convo_ref, ho_ref,
                cbuf, hbuf, xs_s, bd_s, cd_s, dt_s, dta_s, y_s, *, nb, tb, L):
    c = y_ref.shape[-1]
    nc = tb // L
    npair = c // PAIR
    nconv = cbuf.shape[-1]

    @pl.when(_first_step())
    def _():
        cbuf[:, CONV_PAD - (CONV_W - 1):CONV_PAD, :] = conv0_ref[...]
        hbuf[...] = h0_ref[...]

    x = x_ref[...].reshape(nb * tb, x_ref.shape[-1])
    p = _dot(_rms(x, g_ref[...]), w_ref[...])
    xbc = p[:, :nconv]
    dtr = p[:, nconv:nconv + c]
    z = p[:, nconv + c:]
    conv = jnp.concatenate(
        [_causal_conv(cbuf, b, xbc[b * tb:(b + 1) * tb], tb, cw_ref, cb_ref[...]) for b in range(nb)],
        axis=0)
    conv = _silu(conv)
    xs = conv[:, :c]
    nbc = (nconv - c) // 2
    dt = _softplus(dtr + dtb_ref[...])
    xs_s[...] = xs
    bd_s[...] = conv[:, c:c + nbc]
    cd_s[...] = conv[:, c + nbc:]
    dt_s[...] = dt
    dta_s[...] = dt * (-jnp.exp(alog_ref[...]))

    m0, m1 = _head_masks()
    bdm = _bd_mask()
    _, incl, _ = _cat_masks(L)
    tri = tri_ref[...]
    pick = pick_ref[...]

    def cat_cols(x):
        if 2 * L == PAIR:
            return x
        return jnp.concatenate([x[:, 0:L], x[:, HEAD:HEAD + L]], axis=1)

    grp = _group_size(nb * nc)
    chains = [(g, j) for g in range(grp) for j in range(npair)]
    lanes = lambda j: slice(j * PAIR, (j + 1) * PAIR)

    def group(i, carry):
        per = []
        for g in range(grp):
            it = i * grp + g
            rows = pl.ds(pl.multiple_of(it * L, L), L)
            acs = _dot01(tri, dta_s[rows, :])
            acs_last = acs[L - 1:L, :]
            dt_c = dt_s[rows, :]
            per.append(dict(b=it // nc, rows=rows, acs=acs, dt=dt_c, xs=xs_s[rows, :], e_acs=jnp.exp(acs),
                            dl=jnp.exp(acs_last - acs) * dt_c, tot=jnp.exp(acs_last),
                            bd=bd_s[rows, :], cd=cd_s[rows, :]))
        part = lambda key: [per[g][key][:, lanes(j)] for g, j in chains]
        group_part = lambda key: [per[g][key][:, lanes(j // 2)] for g, j in chains]
        acs_p, dt_p, xs_p, dl_p = part("acs"), part("dt"), part("xs"), part("dl")
        bd_p, cd_p = group_part("bd"), group_part("cd")
        picked = [_dot01_nt(pick, jnp.concatenate([a, d_], axis=0)) for a, d_ in zip(acs_p, dt_p)]
        cb = [_dot_nt(c_, _stack_heads(b_, m0, m1)) for c_, b_ in zip(cd_p, bd_p)]
        upd = [_dot_tn(x_ * d_, b_) for x_, d_, b_ in zip(xs_p, dl_p, bd_p)]
        scores = []
        for n in range(len(chains)):
            acs_row = jnp.concatenate([picked[n][0:1, 0:L], picked[n][1:2, 0:L]], axis=1)
            dt_row = jnp.concatenate([picked[n][0:1, L:2 * L], picked[n][1:2, L:2 * L]], axis=1)
            decay = jnp.where(incl, jnp.exp(cat_cols(acs_p[n]) - acs_row), 0.0)
            scores.append(cb[n] * decay * dt_row)
        yx = [_dot(s_, _stack_heads(x_, m0, m1)) for s_, x_ in zip(scores, xs_p)]
        for g in range(grp):
            b, rows = per[g]["b"], per[g]["rows"]
            ns = [n for n, (gg, _) in enumerate(chains) if gg == g]
            h = [hbuf[b, j] for j in range(npair)]
            chg = [_dot_nt(cd_p[n], h[j]) for j, n in enumerate(ns)]
            for j, n in enumerate(ns):
                y_s[rows, lanes(j)] = yx[n] + per[g]["e_acs"][:, lanes(j)] * chg[j]
                hbuf[b, j] = jnp.where(bdm, h[j] * per[g]["tot"][:, lanes(j)] + upd[n], 0.0)
        return carry

    lax.fori_loop(0, nb * nc // grp, group, 0)

    yc = y_s[...] + d_ref[...] * xs
    y_ref[...] = _rms(yc * _silu(z), ng_ref[...]).reshape(nb, tb, c)

    @pl.when(_last_step())
    def _():
        convo_ref[...] = cbuf[:, CONV_PAD - (CONV_W - 1):CONV_PAD, :]
        ho_ref[...] = hbuf[...]


def _hgrn_kernel(x_ref, g_ref, w_ref, lbl_ref, ng_ref, tri_ref, lmask_ref, seg_ref, s0_ref,
                 y_ref, so_ref,
                 sbuf, q_s, k_s, v_s, lf_s, y_s, *, nb, tb, L, layer):
    c = y_ref.shape[-1]
    nc = tb // L
    npair = c // PAIR
    nlev = lmask_ref.shape[0]

    @pl.when(_first_step())
    def _():
        sbuf[...] = s0_ref[...]

    x = x_ref[...].reshape(nb * tb, x_ref.shape[-1])
    p = _dot(_rms(x, g_ref[...]), w_ref[...])
    qd = p[:, 0:c]
    fd = p[:, c:2 * c]
    v = p[:, 2 * c:3 * c]
    gd = p[:, 3 * c:]
    logits = lbl_ref[...]
    ex = jnp.exp(logits - jnp.max(logits, axis=0, keepdims=True))
    sm = ex / jnp.sum(ex, axis=0, keepdims=True)
    lb = jnp.zeros_like(sm[0:1])
    for i in range(1, layer + 1):
        lb = lb + sm[i:i + 1]
    log_lb = jnp.log(lb)
    b2 = jnp.log1p(-lb) - _softplus(-fd)
    logf = jnp.maximum(log_lb, b2) + jnp.log1p(jnp.exp(-jnp.abs(log_lb - b2)))
    q = _silu(qd)
    kx = (1.0 - lb) * _sigmoid(-fd)
    q_s[...] = q
    k_s[...] = kx
    v_s[...] = v
    lf_s[...] = logf
    diag = _dot(q * kx, seg_ref[...]) * v

    m0, m1 = _head_masks()
    bdm = _bd_mask()
    tri = tri_ref[...]

    grp = _group_size(nb * nc)
    chains = [(g, j) for g in range(grp) for j in range(npair)]
    lanes = lambda j: slice(j * PAIR, (j + 1) * PAIR)

    def group(i, carry):
        per = []
        for g in range(grp):
            it = i * grp + g
            rows = pl.ds(pl.multiple_of(it * L, L), L)
            bc = _dot01(tri, lf_s[rows, :])
            b_last = bc[L - 1:L, :]
            q_c = q_s[rows, :]
            k_c = k_s[rows, :]
            e_lev = [jnp.exp(-jnp.abs(bc - _mid_rows(bc, m + 1))) for m in range(nlev)]
            per.append(dict(b=it // nc, rows=rows, v=v_s[rows, :],
                            qe=q_c * jnp.exp(bc), kl=k_c * jnp.exp(b_last - bc), tot=jnp.exp(b_last),
                            qn=[q_c * e for e in e_lev], kn=[k_c * e for e in e_lev]))
        part = lambda key: [per[g][key][:, lanes(j)] for g, j in chains]
        vp, qe_p, kl_p = part("v"), part("qe"), part("kl")
        att = [jnp.zeros((L, 2 * L), F32) for _ in chains]
        for m in range(nlev):
            lev = [_dot_nt(per[g]["qn"][m][:, lanes(j)], _stack_heads(per[g]["kn"][m][:, lanes(j)], m0, m1))
                   for g, j in chains]
            att = [a + lmask_ref[m] * t for a, t in zip(att, lev)]
        yv = [_dot(a, _stack_heads(v_, m0, m1)) for a, v_ in zip(att, vp)]
        upd = [_dot_tn(v_, k_) for v_, k_ in zip(vp, kl_p)]
        for g in range(grp):
            b, rows = per[g]["b"], per[g]["rows"]
            ns = [n for n, (gg, _) in enumerate(chains) if gg == g]
            st = [sbuf[b, j] for j in range(npair)]
            ys = [_dot_nt(qe_p[n], st[j]) for j, n in enumerate(ns)]
            for j, n in enumerate(ns):
                y_s[rows, lanes(j)] = ys[j] + yv[n]
                sbuf[b, j] = jnp.where(bdm, st[j] * per[g]["tot"][:, lanes(j)] + upd[n], 0.0)
        return carry

    lax.fori_loop(0, nb * nc // grp, group, 0)

    o = y_s[...] + diag
    ms = _dot(o * o, seg_ref[...]) * (1.0 / HEAD)
    yd = o * lax.rsqrt(ms + NORM_EPS) * ng_ref[...]
    y_ref[...] = (yd * _silu(gd)).reshape(nb, tb, c)

    @pl.when(_last_step())
    def _():
        so_ref[...] = sbuf[...]


def _post_kernel(ya_ref, yb_ref, yc_ref, yd_ref, x_ref, wo_ref, gpost_ref, gprex_ref, wq_ref,
                 mk_ref, mv_ref, wox_ref, gpostx_ref, o_ref, *, nb, tb, heads):
    d = x_ref.shape[-1]
    c = ya_ref.shape[-1]
    hd = d // heads
    rows = nb * tb
    y = None
    for i, ref in enumerate((ya_ref, yb_ref, yc_ref, yd_ref)):
        t = _dot(ref[...].reshape(rows, c), wo_ref[i * c:(i + 1) * c, :])
        y = t if y is None else y + t
    x1 = x_ref[...].reshape(rows, d) + _rms(y, gpost_ref[...])
    q = _dot(_rms(x1, gprex_ref[...]), wq_ref[...])
    scale = hd ** -0.5
    outs = []
    for b in range(nb):
        qb = q[b * tb:(b + 1) * tb]
        heads_o = []
        for h in range(heads):
            hs = slice(h * hd, (h + 1) * hd)
            s = _dot_nt(qb[:, hs], mk_ref[b, :, hs]) * scale
            s = s - jnp.max(s, axis=-1, keepdims=True)
            e = jnp.exp(s)
            pr = e / jnp.sum(e, axis=-1, keepdims=True)
            heads_o.append(_dot(pr, mv_ref[b, :, hs]))
        outs.append(jnp.concatenate(heads_o, axis=1))
    o = jnp.concatenate(outs, axis=0)
    x2 = x1 + _rms(_dot(o, wox_ref[...]), gpostx_ref[...])
    o_ref[...] = x2.reshape(nb, tb, d)


def _memkv_kernel(m_ref, g_ref, wk_ref, wv_ref, k_ref, v_ref):
    m = _rms(m_ref[0], g_ref[...])
    k_ref[0] = _dot(m, wk_ref[...])
    v_ref[0] = _dot(m, wv_ref[...])


def _full(shape):
    nd = len(shape)
    return pl.BlockSpec(shape, lambda b, t: (0,) * nd)


def _per_batch(shape_tail, nb):
    nd = len(shape_tail)
    return pl.BlockSpec((nb,) + shape_tail, lambda b, t: (b,) + (0,) * nd)


def _state_in(shape_tail, nb, layer):
    nd = len(shape_tail)
    return pl.BlockSpec((None, nb) + shape_tail, lambda b, t: (layer, b) + (0,) * nd)


def _tokens(nb, tb, width):
    return pl.BlockSpec((nb, tb, width), lambda b, t: (b, t, 0))


def _params():
    return pltpu.CompilerParams(dimension_semantics=("arbitrary", "arbitrary"),
                                vmem_limit_bytes=VMEM_LIMIT)


def _call(kern, name, grid, in_arrays, in_specs, out_shapes, out_specs, scratch):
    return pl.pallas_call(
        kern, name=name, grid=grid, in_specs=in_specs, out_specs=out_specs,
        out_shape=out_shapes, scratch_shapes=scratch, compiler_params=_params())(*in_arrays)


def _row(v):
    return v.reshape(1, -1).astype(F32)


def _lru_call(x, lp, conv0, h0, layer, nb, tb):
    B, T, D = x.shape
    c = lp["lru_cw"].shape[-1]
    grid = (B // nb, T // tb)
    ins = [x, lp["g_pre"], lp["w_a"], lp["lru_cw"], lp["lru_cb"], lp["lru_wg"], lp["lru_bg"], lp["lru_lam"],
           conv0, h0]
    specs = [_tokens(nb, tb, D)] + [_full(a.shape) for a in ins[1:8]] + [
        _state_in((CONV_W - 1, c), nb, layer), _state_in((1, c), nb, layer)]
    outs = [jax.ShapeDtypeStruct((B, T, c), F32), jax.ShapeDtypeStruct((B, CONV_W - 1, c), F32),
            jax.ShapeDtypeStruct((B, 1, c), F32)]
    ospecs = [_tokens(nb, tb, c), _per_batch((CONV_W - 1, c), nb), _per_batch((1, c), nb)]
    scratch = [pltpu.VMEM((nb, tb + CONV_PAD, c), F32), pltpu.VMEM((nb, 1, c), F32)]
    return _call(functools.partial(_lru_kernel, nb=nb, tb=tb), "mix_lru", grid, ins, specs, outs, ospecs, scratch)


def _rwkv_call(x, lp, consts, shift0, s0, layer, nb, tb, L):
    B, T, D = x.shape
    c = lp["rw_w0"].shape[-1]
    nsh = shift0.shape[-1]
    npair = c // PAIR
    grid = (B // nb, T // tb)
    ins = [x, lp["g_pre"], lp["w_b"], lp["rw_mu"], lp["rw_w0"], lp["rw_a0"], lp["rw_wup"], lp["rw_kk"],
           lp["rw_ka"], lp["rw_rk"], lp["rw_lng"], lp["rw_lnb"], consts["seg"], consts["tri"], shift0, s0]
    specs = [_tokens(nb, tb, D)] + [_full(a.shape) for a in ins[1:14]] + [
        _state_in((1, nsh), nb, layer), _state_in((npair, PAIR, PAIR), nb, layer)]
    outs = [jax.ShapeDtypeStruct((B, T, c), F32), jax.ShapeDtypeStruct((B, 1, nsh), F32),
            jax.ShapeDtypeStruct((B, npair, PAIR, PAIR), F32)]
    ospecs = [_tokens(nb, tb, c), _per_batch((1, nsh), nb), _per_batch((npair, PAIR, PAIR), nb)]
    rows = nb * tb
    scratch = [pltpu.VMEM((nb, 1, nsh), F32), pltpu.VMEM((nb, npair, PAIR, PAIR), F32)] + [
        pltpu.VMEM((rows, c), F32) for _ in range(7)]
    return _call(functools.partial(_rwkv_kernel, nb=nb, tb=tb, L=L), "mix_rwkv", grid, ins, specs, outs,
                 ospecs, scratch)


def _ssd_call(x, lp, consts, conv0, h0, layer, nb, tb, L):
    B, T, D = x.shape
    c = lp["ssd_dtb"].shape[-1]
    nconv = lp["ssd_cw"].shape[-1]
    nbc = (nconv - c) // 2
    npair = c // PAIR
    grid = (B // nb, T // tb)
    ins = [x, lp["g_pre"], lp["w_c"], lp["ssd_cw"], lp["ssd_cb"], lp["ssd_dtb"], lp["ssd_alog"], lp["ssd_d"],
           lp["ssd_ng"], consts["tri"], consts["pick"], conv0, h0]
    specs = [_tokens(nb, tb, D)] + [_full(a.shape) for a in ins[1:11]] + [
        _state_in((CONV_W - 1, nconv), nb, layer), _state_in((npair, PAIR, PAIR), nb, layer)]
    outs = [jax.ShapeDtypeStruct((B, T, c), F32), jax.ShapeDtypeStruct((B, CONV_W - 1, nconv), F32),
            jax.ShapeDtypeStruct((B, npair, PAIR, PAIR), F32)]
    ospecs = [_tokens(nb, tb, c), _per_batch((CONV_W - 1, nconv), nb), _per_batch((npair, PAIR, PAIR), nb)]
    rows = nb * tb
    scratch = [pltpu.VMEM((nb, tb + CONV_PAD, nconv), F32), pltpu.VMEM((nb, npair, PAIR, PAIR), F32),
               pltpu.VMEM((rows, c), F32), pltpu.VMEM((rows, nbc), F32), pltpu.VMEM((rows, nbc), F32),
               pltpu.VMEM((rows, c), F32), pltpu.VMEM((rows, c), F32), pltpu.VMEM((rows, c), F32)]
    return _call(functools.partial(_ssd_kernel, nb=nb, tb=tb, L=L), "mix_ssd", grid, ins, specs, outs,
                 ospecs, scratch)


def _hgrn_call(x, lp, consts, s0, layer, nb, tb, L):
    B, T, D = x.shape
    c = lp["hg_ng"].shape[-1]
    npair = c // PAIR
    grid = (B // nb, T // tb)
    ins = [x, lp["g_pre"], lp["w_d"], lp["hg_lbl"], lp["hg_ng"], consts["tri"], consts["lmask"], consts["seg"], s0]
    specs = [_tokens(nb, tb, D)] + [_full(a.shape) for a in ins[1:8]] + [_state_in((npair, PAIR, PAIR), nb, layer)]
    outs = [jax.ShapeDtypeStruct((B, T, c), F32), jax.ShapeDtypeStruct((B, npair, PAIR, PAIR), F32)]
    ospecs = [_tokens(nb, tb, c), _per_batch((npair, PAIR, PAIR), nb)]
    rows = nb * tb
    scratch = [pltpu.VMEM((nb, npair, PAIR, PAIR), F32)] + [pltpu.VMEM((rows, c), F32) for _ in range(5)]
    return _call(functools.partial(_hgrn_kernel, nb=nb, tb=tb, L=L, layer=layer), "mix_hgrn", grid, ins, specs,
                 outs, ospecs, scratch)


def _post_call(ys, x, lp, mk, mv, layer, nb, tb, heads):
    B, T, D = x.shape
    c = ys[0].shape[-1]
    M = mk.shape[2]
    grid = (B // nb, T // tb)
    ins = list(ys) + [x, lp["w_out"], lp["g_post"], lp["g_pre_x"], lp["w_q"], mk, mv, lp["w_o"], lp["g_post_x"]]
    kv_spec = pl.BlockSpec((None, nb, M, D), lambda b, t: (layer, b, 0, 0))
    specs = [_tokens(nb, tb, c)] * 4 + [_tokens(nb, tb, D)] + [_full(a.shape) for a in ins[5:9]] + [
        kv_spec, kv_spec] + [_full(a.shape) for a in ins[11:13]]
    return _call(functools.partial(_post_kernel, nb=nb, tb=tb, heads=heads), "post_attn", grid, ins, specs,
                 jax.ShapeDtypeStruct((B, T, D), F32), _tokens(nb, tb, D), [])


def _memkv_call(mem, g, wk, wv):
    B, M, D = mem.shape
    depth = wk.shape[0]
    wspec = pl.BlockSpec((None, D, D), lambda l, b: (l, 0, 0))
    ospec = pl.BlockSpec((None, 1, M, D), lambda l, b: (l, b, 0, 0))
    return pl.pallas_call(
        _memkv_kernel, name="mem_kv", grid=(depth, B),
        in_specs=[pl.BlockSpec((1, M, D), lambda l, b: (b, 0, 0)),
                  pl.BlockSpec((None, 1, D), lambda l, b: (l, 0, 0)), wspec, wspec],
        out_specs=[ospec, ospec],
        out_shape=[jax.ShapeDtypeStruct((depth, B, M, D), F32)] * 2,
        compiler_params=_params())(mem, g, wk, wv)


def _chunk_consts(L, c):
    r = np.arange(L)[:, None]
    j = np.arange(L)[None, :]
    tri = (j <= r).astype(np.float32)
    masks = []
    col = np.arange(2 * L)[None, :]
    s = col % L
    m = 1
    while (1 << m) <= L:
        size, half = 1 << m, 1 << (m - 1)
        masks.append(((r // size == s // size) & (r % size >= half) & (s % size < half)).astype(np.float32))
        m += 1
    seg = (np.arange(c)[:, None] // HEAD == np.arange(c)[None, :] // HEAD).astype(np.float32)
    pick = np.zeros((SUBLANES, PAIR), np.float32)
    pick[0, 0] = 1.0
    pick[1, HEAD] = 1.0
    return {
        "tri": jnp.asarray(tri, BF16),
        "lmask": jnp.asarray(np.stack(masks), F32),
        "seg": jnp.asarray(seg, BF16),
        "pick": jnp.asarray(pick, BF16),
    }


def _to_pairs(s):
    lead, (H, a, b) = s.shape[:-3], s.shape[-3:]
    s = s.reshape(lead + (H // 2, 2, a, b))
    z = jnp.zeros_like(s[..., 0, :, :])
    top = jnp.concatenate([s[..., 0, :, :], z], axis=-1)
    bot = jnp.concatenate([z, s[..., 1, :, :]], axis=-1)
    return jnp.concatenate([top, bot], axis=-2)


def _from_pairs(s):
    lead, P = s.shape[:-3], s.shape[-3]
    both = jnp.stack([s[..., :HEAD, :HEAD], s[..., HEAD:, HEAD:]], axis=-3)
    return both.reshape(lead + (2 * P, HEAD, HEAD))


def _expand_bc(t, c, n):
    xs, bm, cm = t[..., :c], t[..., c:c + 2 * n], t[..., c + 2 * n:]
    dup = lambda u: jnp.concatenate([u[..., :n], u[..., :n], u[..., n:], u[..., n:]], axis=-1)
    return jnp.concatenate([xs, dup(bm), dup(cm)], axis=-1)


def _shrink_bc(t, c, n):
    pick = lambda u: jnp.concatenate([u[..., :n], u[..., 2 * n:3 * n]], axis=-1)
    return jnp.concatenate([t[..., :c], pick(t[..., c:c + 4 * n]), pick(t[..., c + 4 * n:])], axis=-1)


def _blockdiag(w):
    n, d, e = w.shape
    eye = jnp.eye(n, dtype=w.dtype)
    return (eye[:, None, :, None] * w[:, :, None, :]).reshape(n * d, n * e)


def kernel(x_prompt, x_sample, state_lru_conv, state_lru_h, state_rwkv_shift, state_rwkv_wkv, state_ssd_conv, state_ssd_h, state_hgrn_s, cache_mem_k, cache_mem_v, mem_prompt, g_pre, g_post, g_pre_x, g_post_x, w_in, w_out, lru_conv_w, lru_conv_b, lru_w_r, lru_b_r, lru_w_i, lru_b_i, lru_lambda, rwkv_mu, rwkv_w0, rwkv_w_up, rwkv_a0, rwkv_a_up, rwkv_k_k, rwkv_k_a, rwkv_r_k, rwkv_ln_g, rwkv_ln_b, ssd_conv_w, ssd_conv_b, ssd_dt_bias, ssd_a_log, ssd_d, ssd_norm_g, hgrn_lb_logits, hgrn_norm_g, mem_g, mem_w_q, mem_w_k, mem_w_v, mem_w_o):
    depth = w_in.shape[0]
    D = x_prompt.shape[-1]
    c = lru_conv_w.shape[-1]
    nsh = rwkv_mu.shape[-1]
    rank = (nsh - 3 * c) // 2
    ssd_heads = ssd_dt_bias.shape[-1]
    nstate = (ssd_conv_w.shape[-1] - c) // 4
    heads_x = cache_mem_k.shape[-2]
    mem_len = mem_prompt.shape[1]

    o_b = 2 * c
    o_c = o_b + nsh + c
    o_dt = o_c + c + 4 * nstate
    o_z = o_dt + ssd_heads
    o_d = o_z + c

    layers = []
    for l in range(depth):
        w = w_in[l]
        zero = jnp.zeros((rank, c), F32)
        wup = jnp.concatenate([jnp.concatenate([rwkv_w_up[l], zero], axis=1),
                               jnp.concatenate([zero, rwkv_a_up[l]], axis=1)], axis=0)
        rep = lambda v: jnp.repeat(v, c // ssd_heads, axis=-1)
        w_c = jnp.concatenate([_expand_bc(w[:, o_c:o_dt], c, nstate), rep(w[:, o_dt:o_z]), w[:, o_z:o_d]], axis=1)
        layers.append({
            "g_pre": _row(g_pre[l]), "g_post": _row(g_post[l]), "g_pre_x": _row(g_pre_x[l]),
            "g_post_x": _row(g_post_x[l]),
            "w_a": w[:, :o_b].astype(BF16), "w_b": w[:, o_b:o_c].astype(BF16), "w_c": w_c.astype(BF16),
            "w_d": w[:, o_d:].astype(BF16),
            "w_out": w_out[l].astype(BF16), "w_q": mem_w_q[l].astype(BF16), "w_o": mem_w_o[l].astype(BF16),
            "lru_cw": lru_conv_w[l], "lru_cb": _row(lru_conv_b[l]),
            "lru_wg": jnp.concatenate([_blockdiag(lru_w_r[l]), _blockdiag(lru_w_i[l])], axis=1).astype(BF16),
            "lru_bg": _row(jnp.concatenate([lru_b_r[l], lru_b_i[l]])), "lru_lam": _row(lru_lambda[l]),
            "rw_mu": _row(rwkv_mu[l]), "rw_w0": _row(rwkv_w0[l]), "rw_a0": _row(rwkv_a0[l]),
            "rw_wup": wup.astype(BF16), "rw_kk": _row(rwkv_k_k[l]), "rw_ka": _row(rwkv_k_a[l]),
            "rw_rk": _row(rwkv_r_k[l]), "rw_lng": _row(rwkv_ln_g[l]), "rw_lnb": _row(rwkv_ln_b[l]),
            "ssd_cw": _expand_bc(ssd_conv_w[l], c, nstate), "ssd_cb": _row(_expand_bc(ssd_conv_b[l], c, nstate)),
            "ssd_dtb": _row(rep(ssd_dt_bias[l])), "ssd_alog": _row(rep(ssd_a_log[l])), "ssd_d": _row(rep(ssd_d[l])),
            "ssd_ng": _row(ssd_norm_g[l]),
            "hg_lbl": hgrn_lb_logits.astype(F32), "hg_ng": _row(hgrn_norm_g[l]),
        })

    def run(x, mk, mv, conv_a, h_a, shift_b, wkv_b, conv_c, h_c, s_d, nb, tb, L):
        consts = _chunk_consts(L, c)
        B = x.shape[0]
        h_a = h_a.reshape(depth, B, 1, c)
        shift_b = shift_b.reshape(depth, B, 1, nsh)
        wkv_b = _to_pairs(wkv_b)
        conv_c = _expand_bc(conv_c, c, nstate)
        h_c = _to_pairs(h_c)
        s_d = _to_pairs(jnp.swapaxes(s_d, -1, -2))
        acc = [[] for _ in range(7)]
        for l in range(depth):
            lp = layers[l]
            ya, nca, nha = _lru_call(x, lp, conv_a, h_a, l, nb, tb)
            yb, nsb, nwb = _rwkv_call(x, lp, consts, shift_b, wkv_b, l, nb, tb, L)
            yc, ncc, nhc = _ssd_call(x, lp, consts, conv_c, h_c, l, nb, tb, L)
            yd, nsd = _hgrn_call(x, lp, consts, s_d, l, nb, tb, L)
            x = _post_call((ya, yb, yc, yd), x, lp, mk, mv, l, nb, tb, heads_x)
            for lst, val in zip(acc, (nca, nha, nsb, nwb, ncc, nhc, nsd)):
                lst.append(val)
        nca, nha, nsb, nwb, ncc, nhc, nsd = (jnp.stack(v) for v in acc)
        return (x, nca, nha.reshape(depth, B, c), nsb.reshape(depth, B, nsh), _from_pairs(nwb),
                _shrink_bc(ncc, c, nstate), _from_pairs(nhc), jnp.swapaxes(_from_pairs(nsd), -1, -2))

    Bp, Tp = x_prompt.shape[:2]
    Bs, Ts = x_sample.shape[:2]
    k_p, v_p = _memkv_call(mem_prompt, mem_g.reshape(depth, 1, D).astype(F32), mem_w_k.astype(BF16),
                           mem_w_v.astype(BF16))
    kv_shape = (depth, Bp, mem_len, heads_x, D // heads_x)
    mem_k_p = k_p.reshape(kv_shape)
    mem_v_p = v_p.reshape(kv_shape)
    zeros = lambda *s: jnp.zeros((depth, Bp) + s, F32)
    Lp = CHUNK if Tp % CHUNK == 0 else Tp
    Ls = CHUNK if Ts % CHUNK == 0 else Ts
    tb_p = PROMPT_CHUNKS * Lp if Tp % (PROMPT_CHUNKS * Lp) == 0 else Lp
    nb_s = SAMPLE_BATCH if Bs % SAMPLE_BATCH == 0 else 1
    (y_prompt, lru_conv_p, lru_h_p, rwkv_shift_p, rwkv_wkv_p, ssd_conv_p, ssd_h_p, hgrn_s_p) = run(
        x_prompt, k_p, v_p,
        zeros(CONV_W - 1, c), zeros(c), zeros(nsh), zeros(c // HEAD, HEAD, HEAD),
        zeros(CONV_W - 1, c + 4 * nstate), zeros(ssd_heads, HEAD, nstate), zeros(c // HEAD, HEAD, HEAD),
        1, tb_p, Lp)
    (y_sample, lru_conv_s, lru_h_s, rwkv_shift_s, rwkv_wkv_s, ssd_conv_s, ssd_h_s, hgrn_s_s) = run(
        x_sample, cache_mem_k.reshape(depth, Bs, mem_len, D), cache_mem_v.reshape(depth, Bs, mem_len, D),
        state_lru_conv, state_lru_h, state_rwkv_shift, state_rwkv_wkv,
        state_ssd_conv, state_ssd_h, state_hgrn_s, nb_s, Ts, Ls)
    return (y_prompt, y_sample, lru_conv_p, lru_conv_s, lru_h_p, lru_h_s, rwkv_shift_p, rwkv_shift_s,
            rwkv_wkv_p, rwkv_wkv_s, ssd_conv_p, ssd_conv_s, ssd_h_p, ssd_h_s, hgrn_s_p, hgrn_s_s,
            mem_k_p, mem_v_p)
```

```python
import functools

import numpy as np
import jax
import jax.numpy as jnp
from jax import lax
from jax.experimental import pallas as pl
from jax.experimental.pallas import tpu as pltpu

F32 = jnp.float32
BF16 = jnp.bfloat16

SUBLANES = 8
HEAD = 64
PAIR = 2 * HEAD
CHUNK = 64
CONV_W = 4
CONV_PAD = 8
LRU_C = 8.0
RWKV_DECAY_SCALE = 0.6065306597126334
RWKV_LN_EPS = 64e-5
NORM_EPS = 1e-6
VMEM_LIMIT = 56 * 1024 * 1024
PROMPT_CHUNKS = 8
SAMPLE_BATCH = 8


def _dot(a, b):
    return jnp.dot(a.astype(BF16), b.astype(BF16), preferred_element_type=F32)


def _dot_nt(a, b):
    return lax.dot_general(a.astype(BF16), b.astype(BF16), (((1,), (1,)), ((), ())),
                           preferred_element_type=F32)


def _dot_tn(a, b):
    return lax.dot_general(a.astype(BF16), b.astype(BF16), (((0,), (0,)), ((), ())),
                           preferred_element_type=F32)


def _split3(x):
    hi = x.astype(BF16)
    r1 = x - hi.astype(F32)
    mid = r1.astype(BF16)
    lo = (r1 - mid.astype(F32)).astype(BF16)
    return hi, mid, lo


def _group_size(n):
    for g in (8, 4, 2):
        if n % g == 0:
            return g
    return 1


def _dot01(m01, x):
    hi, mid, lo = _split3(x)
    d = lambda p: jnp.dot(m01, p, preferred_element_type=F32)
    return d(hi) + d(mid) + d(lo)


def _dot01_nt(m01, x):
    hi, mid, lo = _split3(x)
    d = lambda p: lax.dot_general(m01, p, (((1,), (1,)), ((), ())), preferred_element_type=F32)
    return d(hi) + d(mid) + d(lo)


def _rms(x, g):
    return x * lax.rsqrt(jnp.mean(x * x, axis=-1, keepdims=True) + NORM_EPS) * g


def _sigmoid(x):
    return jax.nn.sigmoid(x)


def _silu(x):
    return x * jax.nn.sigmoid(x)


def _softplus(x):
    return jnp.maximum(x, 0.0) + jnp.log1p(jnp.exp(-jnp.abs(x)))


def _neg_expm1(z):
    return -jnp.tanh(0.5 * z) * (jnp.exp(z) + 1.0)


def _head_masks():
    lane = lax.broadcasted_iota(jnp.int32, (1, PAIR), 1)
    return lane < HEAD, lane >= HEAD


def _stack_heads(x, m0, m1):
    return jnp.concatenate([jnp.where(m0, x, 0.0), jnp.where(m1, x, 0.0)], axis=0)


def _bd_mask():
    r = lax.broadcasted_iota(jnp.int32, (PAIR, PAIR), 0)
    c = lax.broadcasted_iota(jnp.int32, (PAIR, PAIR), 1)
    return (r >= HEAD) == (c >= HEAD)


def _cat_masks(L):
    t = lax.broadcasted_iota(jnp.int32, (L, 2 * L), 0)
    c = lax.broadcasted_iota(jnp.int32, (L, 2 * L), 1)
    s = jnp.where(c >= L, c - L, c)
    return s < t, s <= t, c < L


def _cat_to_blockdiag(m, first):
    return jnp.concatenate([jnp.where(first, m, 0.0), jnp.where(first, 0.0, m)], axis=0)


def _tri_inv(nns, L):
    n = nns[0].shape[0]
    ri = lax.broadcasted_iota(jnp.int32, (n, n), 0)
    ci = lax.broadcasted_iota(jnp.int32, (n, n), 1)
    tinv = [jnp.where((ri >> 1) == (ci >> 1), t, 0.0) + (ri == ci).astype(F32) for t in nns]
    shift = 1
    while (1 << shift) < L:
        join = ((ri >> (shift + 1)) == (ci >> (shift + 1))) & ((ri >> shift) != (ci >> shift))
        w = [_dot(t, jnp.where(join, x, 0.0)) for t, x in zip(tinv, nns)]
        tinv = [t + _dot(w_, t) for t, w_ in zip(tinv, w)]
        shift += 1
    return tinv


def _mid_rows(b, m):
    n, c = b.shape
    size, half = 1 << m, 1 << (m - 1)
    if size >= 2 * SUBLANES:
        return jnp.concatenate(
            [jnp.broadcast_to(b[s + half - 1:s + half, :], (size, c)) for s in range(0, n, size)], axis=0)
    b3 = b.reshape(n // SUBLANES, SUBLANES, c)
    sub = lax.broadcasted_iota(jnp.int32, b3.shape, 1)
    mids = list(range(half - 1, SUBLANES, size))
    out = jnp.broadcast_to(b3[:, mids[-1]:mids[-1] + 1, :], b3.shape)
    for mid in reversed(mids[:-1]):
        out = jnp.where(sub <= mid + half, b3[:, mid:mid + 1, :], out)
    return out.reshape(n, c)


def _shift_rows(x, d, fill):
    row = lax.broadcasted_iota(jnp.int32, x.shape, 0)
    return jnp.where(row >= d, pltpu.roll(x, d, axis=0), fill)


def _affine_scan(a, u, h_prev):
    n, c = a.shape
    sub = lax.broadcasted_iota(jnp.int32, (n, c), 0) & (SUBLANES - 1)
    d = 1
    while d < SUBLANES:
        keep = sub >= d
        a_s = jnp.where(keep, pltpu.roll(a, d, axis=0), 1.0)
        u_s = jnp.where(keep, pltpu.roll(u, d, axis=0), 0.0)
        u = a * u_s + u
        a = a * a_s
        d *= 2
    nt = n // SUBLANES
    a3 = a.reshape(nt, SUBLANES, c)
    u3 = u.reshape(nt, SUBLANES, c)
    ta = jnp.broadcast_to(a3[:, SUBLANES - 1:SUBLANES, :], a3.shape)
    tu = jnp.broadcast_to(u3[:, SUBLANES - 1:SUBLANES, :], u3.shape)
    carry = jnp.broadcast_to(h_prev, (SUBLANES, c))
    carries = []
    for k in range(nt):
        carries.append(carry)
        carry = ta[k] * carry + tu[k]
    return u + a * jnp.concatenate(carries, axis=0)


def _causal_conv(cbuf, b, xa, tb, w_ref, bias):
    cbuf[b, CONV_PAD:CONV_PAD + tb, :] = xa
    y = bias + xa * w_ref[CONV_W - 1:CONV_W, :]
    for j in range(1, CONV_W):
        y = y + cbuf[b, CONV_PAD - j:CONV_PAD - j + tb, :] * w_ref[CONV_W - 1 - j:CONV_W - j, :]
    hist = cbuf[b, tb + CONV_PAD - (CONV_W - 1):tb + CONV_PAD, :]
    cbuf[b, CONV_PAD - (CONV_W - 1):CONV_PAD, :] = hist
    return y


def _first_step():
    return pl.program_id(1) == 0


def _last_step():
    return pl.program_id(1) == pl.num_programs(1) - 1


def _lru_kernel(x_ref, g_ref, w_ref, cw_ref, cb_ref, wg_ref, bg_ref, lam_ref, conv0_ref, h0_ref,
                y_ref, convo_ref, ho_ref, cbuf, hbuf, *, nb, tb):
    c = y_ref.shape[-1]

    @pl.when(_first_step())
    def _():
        cbuf[:, CONV_PAD - (CONV_W - 1):CONV_PAD, :] = conv0_ref[...]
        hbuf[...] = h0_ref[...]

    x = x_ref[...].reshape(nb * tb, x_ref.shape[-1])
    p = _dot(_rms(x, g_ref[...]), w_ref[...])
    xa = p[:, :c]
    ga = p[:, c:]
    xc = jnp.concatenate(
        [_causal_conv(cbuf, b, xa[b * tb:(b + 1) * tb], tb, cw_ref, cb_ref[...]) for b in range(nb)],
        axis=0)
    gates = _dot(xc, wg_ref[...]) + bg_ref[...]
    r = _sigmoid(gates[:, :c])
    i = _sigmoid(gates[:, c:])
    log_a = (-LRU_C) * r * _softplus(-lam_ref[...])
    a = jnp.exp(log_a)
    u = jnp.sqrt(_neg_expm1(2.0 * log_a)) * (i * xc)
    hs = []
    for b in range(nb):
        h = _affine_scan(a[b * tb:(b + 1) * tb], u[b * tb:(b + 1) * tb], hbuf[b])
        hbuf[b] = h[tb - 1:tb, :]
        hs.append(h)
    h = jnp.concatenate(hs, axis=0)
    y_ref[...] = (h * _silu(ga)).reshape(nb, tb, c)

    @pl.when(_last_step())
    def _():
        convo_ref[...] = cbuf[:, CONV_PAD - (CONV_W - 1):CONV_PAD, :]
        ho_ref[...] = hbuf[...]


def _rwkv_kernel(x_ref, g_ref, w_ref, mu_ref, w0_ref, a0_ref, wup_ref, kk_ref, ka_ref, rk_ref,
                 lng_ref, lnb_ref, seg_ref, tri_ref, shift0_ref, s0_ref,
                 y_ref, shifto_ref, so_ref,
                 shbuf, sbuf, lw_s, kk_s, kka_s, kp_s, r_s, v_s, y_s, *, nb, tb, L):
    c = y_ref.shape[-1]
    nc = tb // L
    npair = c // PAIR

    @pl.when(_first_step())
    def _():
        shbuf[...] = shift0_ref[...]
        sbuf[...] = s0_ref[...]

    x = x_ref[...].reshape(nb * tb, x_ref.shape[-1])
    p = _dot(_rms(x, g_ref[...]), w_ref[...])
    nsh = shbuf.shape[-1]
    pb = p[:, :nsh]
    gb = p[:, nsh:]
    xms = []
    for b in range(nb):
        slab = pb[b * tb:(b + 1) * tb]
        sh = _shift_rows(slab, 1, shbuf[b])
        shbuf[b] = slab[tb - 1:tb, :]
        xms.append(slab + (sh - slab) * mu_ref[...])
    xm = jnp.concatenate(xms, axis=0)
    r = xm[:, 0:c]
    k = xm[:, c:2 * c]
    v = xm[:, 2 * c:3 * c]
    lowrank = xm[:, 3 * c:]
    lane = lax.broadcasted_iota(jnp.int32, (1, lowrank.shape[-1]), 1)
    lowrank = jnp.where(lane < lowrank.shape[-1] // 2, jnp.tanh(lowrank), lowrank)
    up = _dot(lowrank, wup_ref[...])
    lw = (-RWKV_DECAY_SCALE) * _sigmoid(w0_ref[...] + up[:, :c])
    a = _sigmoid(a0_ref[...] + up[:, c:])
    kk = k * kk_ref[...]
    kk = kk * lax.rsqrt(_dot(kk * kk, seg_ref[...]) + 1e-12)
    kp = k * (1.0 + (a - 1.0) * ka_ref[...])
    lw_s[...] = lw
    kk_s[...] = kk
    kka_s[...] = kk * a
    kp_s[...] = kp
    r_s[...] = r
    v_s[...] = v
    bonus = _dot(r * kp * rk_ref[...], seg_ref[...]) * v

    m0, m1 = _head_masks()
    bd = _bd_mask()
    strict, incl, first = _cat_masks(L)
    tri = tri_ref[...]

    grp = _group_size(nb * nc)
    chains = [(g, j) for g in range(grp) for j in range(npair)]
    lanes = lambda j: slice(j * PAIR, (j + 1) * PAIR)

    def group(i, carry):
        per = []
        for g in range(grp):
            it = i * grp + g
            rows = pl.ds(pl.multiple_of(it * L, L), L)
            lw_c = lw_s[rows, :]
            cl = _dot01(tri, lw_c)
            cl_last = cl[L - 1:L, :]
            e_ip = jnp.exp(-cl)
            e_rel = jnp.exp(cl_last - cl)
            kk_c = kk_s[rows, :]
            kka_c = kka_s[rows, :]
            kp_c = kp_s[rows, :]
            per.append(dict(
                b=it // nc, rows=rows,
                ab=-kk_c * jnp.exp(cl - lw_c), bb=kka_c * e_ip, kb=kp_c * e_ip,
                rb=r_s[rows, :] * jnp.exp(cl), bbl=kka_c * e_rel, kbl=kp_c * e_rel,
                p_last=jnp.exp(cl_last), v=v_s[rows, :]))
        part = lambda key: [per[g][key][:, lanes(j)] for g, j in chains]
        abp, rbp, vp = part("ab"), part("rb"), part("v")
        bblp, kblp = part("bbl"), part("kbl")
        zb = [_stack_heads(t, m0, m1) for t in part("bb")]
        zk = [_stack_heads(t, m0, m1) for t in part("kb")]
        g1 = [_dot_nt(jnp.concatenate([a, r], axis=0), jnp.concatenate([b_, k_], axis=0))
              for a, r, b_, k_ in zip(abp, rbp, zb, zk)]
        nn = [_cat_to_blockdiag(jnp.where(strict, t[0:L, 0:2 * L], 0.0), first) for t in g1]
        mk = [_cat_to_blockdiag(jnp.where(strict, t[0:L, 2 * L:4 * L], 0.0), first) for t in g1]
        r_b = [jnp.where(incl, t[L:2 * L, 0:2 * L], 0.0) for t in g1]
        r_k = [jnp.where(incl, t[L:2 * L, 2 * L:4 * L], 0.0) for t in g1]
        tinv = _tri_inv(nn, L)
        vst = [_stack_heads(t, m0, m1) for t in vp]
        mkv = [_dot(m_, v_) for m_, v_ in zip(mk, vst)]
        ykv = [_dot(m_, v_) for m_, v_ in zip(r_k, vst)]
        skv = [_dot_tn(v_, k_) for v_, k_ in zip(vp, kblp)]
        for g in range(grp):
            ns = [n for n, (gg, _) in enumerate(chains) if gg == g]
            b, rows, p_last = per[g]["b"], per[g]["rows"], per[g]["p_last"]
            s = [sbuf[b, j] for j in range(npair)]
            a_s = [_dot_nt(abp[n], s[j]) for j, n in enumerate(ns)]
            r_s_ = [_dot_nt(rbp[n], s[j]) for j, n in enumerate(ns)]
            ust = [_dot(tinv[n], _stack_heads(a_s[j], m0, m1) + mkv[n]) for j, n in enumerate(ns)]
            yv = [r_s_[j] + _dot(r_b[n], ust[j]) + ykv[n] for j, n in enumerate(ns)]
            su = [_dot_tn(ust[j][0:L] + ust[j][L:2 * L], bblp[n]) for j, n in enumerate(ns)]
            for j, n in enumerate(ns):
                s_new = s[j] * p_last[:, lanes(j)] + su[j] + skv[n]
                sbuf[b, j] = jnp.where(bd, s_new, 0.0)
                y_s[rows, lanes(j)] = yv[j]
        return carry

    lax.fori_loop(0, nb * nc // grp, group, 0)

    y = y_s[...]
    inv = 1.0 / HEAD
    mean = _dot(y, seg_ref[...]) * inv
    yc = y - mean
    var = _dot(yc * yc, seg_ref[...]) * inv
    yn = yc * lax.rsqrt(var + RWKV_LN_EPS) * lng_ref[...] + lnb_ref[...]
    y_ref[...] = ((yn + bonus) * _silu(gb)).reshape(nb, tb, c)

    @pl.when(_last_step())
    def _():
        shifto_ref[...] = shbuf[...]
        so_ref[...] = sbuf[...]


def _ssd_kernel(x_ref, g_ref, w_ref, cw_ref, cb_ref, dtb_ref, alog_ref, d_ref, ng_ref, tri_ref,
                pick_ref, conv0_ref, h0_ref,
                y_ref, convo_ref, ho_ref,
                cbuf, hbuf, xs_s, bd_s, cd_s, dt_s, dta_s, y_s, *, nb, tb, L):
    c = y_ref.shape[-1]
    nc = tb // L
    npair = c // PAIR
    nconv = cbuf.shape[-1]

    @pl.when(_first_step())
    def _():
        cbuf[:, CONV_PAD - (CONV_W - 1):CONV_PAD, :] = conv0_ref[...]
        hbuf[...] = h0_ref[...]

    x = x_ref[...].reshape(nb * tb, x_ref.shape[-1])
    p = _dot(_rms(x, g_ref[...]), w_ref[...])
    xbc = p[:, :nconv]
    dtr = p[:, nconv:nconv + c]
    z = p[:, nconv + c:]
    conv = jnp.concatenate(
        [_causal_conv(cbuf, b, xbc[b * tb:(b + 1) * tb], tb, cw_ref, cb_ref[...]) for b in range(nb)],
        axis=0)
    conv = _silu(conv)
    xs = conv[:, :c]
    nbc = (nconv - c) // 2
    dt = _softplus(dtr + dtb_ref[...])
    xs_s[...] = xs
    bd_s[...] = conv[:, c:c + nbc]
    cd_s[...] = conv[:, c + nbc:]
    dt_s[...] = dt
    dta_s[...] = dt * (-jnp.exp(alog_ref[...]))

    m0, m1 = _head_masks()
    bdm = _bd_mask()
    _, incl, _ = _cat_masks(L)
    tri = tri_ref[...]
    pick = pick_ref[...]

    def cat_cols(x):
        if 2 * L == PAIR:
            return x
        return jnp.concatenate([x[:, 0:L], x[:, HEAD:HEAD + L]], axis=1)

    grp = _group_size(nb * nc)
    chains = [(g, j) for g in range(grp) for j in range(npair)]
    lanes = lambda j: slice(j * PAIR, (j + 1) * PAIR)

    def group(i, carry):
        per = []
        for g in range(grp):
            it = i * grp + g
            rows = pl.ds(pl.multiple_of(it * L, L), L)
            acs = _dot01(tri, dta_s[rows, :])
            acs_last = acs[L - 1:L, :]
            dt_c = dt_s[rows, :]
            per.append(dict(b=it // nc, rows=rows, acs=acs, dt=dt_c, xs=xs_s[rows, :], e_acs=jnp.exp(acs),
                            dl=jnp.exp(acs_last - acs) * dt_c, tot=jnp.exp(acs_last),
                            bd=bd_s[rows, :], cd=cd_s[rows, :]))
        part = lambda key: [per[g][key][:, lanes(j)] for g, j in chains]
        group_part = lambda key: [per[g][key][:, lanes(j // 2)] for g, j in chains]
        acs_p, dt_p, xs_p, dl_p = part("acs"), part("dt"), part("xs"), part("dl")
        bd_p, cd_p = group_part("bd"), group_part("cd")
        picked = [_dot01_nt(pick, jnp.concatenate([a, d_], axis=0)) for a, d_ in zip(acs_p, dt_p)]
        cb = [_dot_nt(c_, _stack_heads(b_, m0, m1)) for c_, b_ in zip(cd_p, bd_p)]
        upd = [_dot_tn(x_ * d_, b_) for x_, d_, b_ in zip(xs_p, dl_p, bd_p)]
        scores = []
        for n in range(len(chains)):
            acs_row = jnp.concatenate([picked[n][0:1, 0:L], picked[n][1:2, 0:L]], axis=1)
            dt_row = jnp.concatenate([picked[n][0:1, L:2 * L], picked[n][1:2, L:2 * L]], axis=1)
            decay = jnp.where(incl, jnp.exp(cat_cols(acs_p[n]) - acs_row), 0.0)
            scores.append(cb[n] * decay * dt_row)
        yx = [_dot(s_, _stack_heads(x_, m0, m1)) for s_, x_ in zip(scores, xs_p)]
        for g in range(grp):
            b, rows = per[g]["b"], per[g]["rows"]
            ns = [n for n, (gg, _) in enumerate(chains) if gg == g]
            h = [hbuf[b, j] for j in range(npair)]
            chg = [_dot_nt(cd_p[n], h[j]) for j, n in enumerate(ns)]
            for j, n in enumerate(ns):
                y_s[rows, lanes(j)] = yx[n] + per[g]["e_acs"][:, lanes(j)] * chg[j]
                hbuf[b, j] = jnp.where(bdm, h[j] * per[g]["tot"][:, lanes(j)] + upd[n], 0.0)
        return carry

    lax.fori_loop(0, nb * nc // grp, group, 0)

    yc = y_s[...] + d_ref[...] * xs
    y_ref[...] = _rms(yc * _silu(z), ng_ref[...]).reshape(nb, tb, c)

    @pl.when(_last_step())
    def _():
        convo_ref[...] = cbuf[:, CONV_PAD - (CONV_W - 1):CONV_PAD, :]
        ho_ref[...] = hbuf[...]


def _hgrn_kernel(x_ref, g_ref, w_ref, lbl_ref, ng_ref, tri_ref, lmask_ref, seg_ref, s0_ref,
                 y_ref, so_ref,
                 sbuf, q_s, k_s, v_s, lf_s, y_s, *, nb, tb, L, layer):
    c = y_ref.shape[-1]
    nc = tb // L
    npair = c // PAIR
    nlev = lmask_ref.shape[0]

    @pl.when(_first_step())
    def _():
        sbuf[...] = s0_ref[...]

    x = x_ref[...].reshape(nb * tb, x_ref.shape[-1])
    p = _dot(_rms(x, g_ref[...]), w_ref[...])
    qd = p[:, 0:c]
    fd = p[:, c:2 * c]
    v = p[:, 2 * c:3 * c]
    gd = p[:, 3 * c:]
    logits = lbl_ref[...]
    ex = jnp.exp(logits - jnp.max(logits, axis=0, keepdims=True))
    sm = ex / jnp.sum(ex, axis=0, keepdims=True)
    lb = jnp.zeros_like(sm[0:1])
    for i in range(1, layer + 1):
        lb = lb + sm[i:i + 1]
    log_lb = jnp.log(lb)
    b2 = jnp.log1p(-lb) - _softplus(-fd)
    logf = jnp.maximum(log_lb, b2) + jnp.log1p(jnp.exp(-jnp.abs(log_lb - b2)))
    q = _silu(qd)
    kx = (1.0 - lb) * _sigmoid(-fd)
    q_s[...] = q
    k_s[...] = kx
    v_s[...] = v
    lf_s[...] = logf
    diag = _dot(q * kx, seg_ref[...]) * v

    m0, m1 = _head_masks()
    bdm = _bd_mask()
    tri = tri_ref[...]

    grp = _group_size(nb * nc)
    chains = [(g, j) for g in range(grp) for j in range(npair)]
    lanes = lambda j: slice(j * PAIR, (j + 1) * PAIR)

    def group(i, carry):
        per = []
        for g in range(grp):
            it = i * grp + g
            rows = pl.ds(pl.multiple_of(it * L, L), L)
            bc = _dot01(tri, lf_s[rows, :])
            b_last = bc[L - 1:L, :]
            q_c = q_s[rows, :]
            k_c = k_s[rows, :]
            e_lev = [jnp.exp(-jnp.abs(bc - _mid_rows(bc, m + 1))) for m in range(nlev)]
            per.append(dict(b=it // nc, rows=rows, v=v_s[rows, :],
                            qe=q_c * jnp.exp(bc), kl=k_c * jnp.exp(b_last - bc), tot=jnp.exp(b_last),
                            qn=[q_c * e for e in e_lev], kn=[k_c * e for e in e_lev]))
        part = lambda key: [per[g][key][:, lanes(j)] for g, j in chains]
        vp, qe_p, kl_p = part("v"), part("qe"), part("kl")
        att = [jnp.zeros((L, 2 * L), F32) for _ in chains]
        for m in range(nlev):
            lev = [_dot_nt(per[g]["qn"][m][:, lanes(j)], _stack_heads(per[g]["kn"][m][:, lanes(j)], m0, m1))
                   for g, j in chains]
            att = [a + lmask_ref[m] * t for a, t in zip(att, lev)]
        yv = [_dot(a, _stack_heads(v_, m0, m1)) for a, v_ in zip(att, vp)]
        upd = [_dot_tn(v_, k_) for v_, k_ in zip(vp, kl_p)]
        for g in range(grp):
            b, rows = per[g]["b"], per[g]["rows"]
            ns = [n for n, (gg, _) in enumerate(chains) if gg == g]
            st = [sbuf[b, j] for j in range(npair)]
            ys = [_dot_nt(qe_p[n], st[j]) for j, n in enumerate(ns)]
            for j, n in enumerate(ns):
                y_s[rows, lanes(j)] = ys[j] + yv[n]
                sbuf[b, j] = jnp.where(bdm, st[j] * per[g]["tot"][:, lanes(j)] + upd[n], 0.0)
        return carry

    lax.fori_loop(0, nb * nc // grp, group, 0)

    o = y_s[...] + diag
    ms = _dot(o * o, seg_ref[...]) * (1.0 / HEAD)
    yd = o * lax.rsqrt(ms + NORM_EPS) * ng_ref[...]
    y_ref[...] = (yd * _silu(gd)).reshape(nb, tb, c)

    @pl.when(_last_step())
    def _():
        so_ref[...] = sbuf[...]


def _post_kernel(ya_ref, yb_ref, yc_ref, yd_ref, x_ref, wo_ref, gpost_ref, gprex_ref, wq_ref,
                 mk_ref, mv_ref, wox_ref, gpostx_ref, o_ref, *, nb, tb, heads):
    d = x_ref.shape[-1]
    c = ya_ref.shape[-1]
    hd = d // heads
    rows = nb * tb
    y = None
    for i, ref in enumerate((ya_ref, yb_ref, yc_ref, yd_ref)):
        t = _dot(ref[...].reshape(rows, c), wo_ref[i * c:(i + 1) * c, :])
        y = t if y is None else y + t
    x1 = x_ref[...].reshape(rows, d) + _rms(y, gpost_ref[...])
    q = _dot(_rms(x1, gprex_ref[...]), wq_ref[...])
    scale = hd ** -0.5
    outs = []
    for b in range(nb):
        qb = q[b * tb:(b + 1) * tb]
        heads_o = []
        for h in range(heads):
            hs = slice(h * hd, (h + 1) * hd)
            s = _dot_nt(qb[:, hs], mk_ref[b, :, hs]) * scale
            s = s - jnp.max(s, axis=-1, keepdims=True)
            e = jnp.exp(s)
            pr = e / jnp.sum(e, axis=-1, keepdims=True)
            heads_o.append(_dot(pr, mv_ref[b, :, hs]))
        outs.append(jnp.concatenate(heads_o, axis=1))
    o = jnp.concatenate(outs, axis=0)
    x2 = x1 + _rms(_dot(o, wox_ref[...]), gpostx_ref[...])
    o_ref[...] = x2.reshape(nb, tb, d)


def _memkv_kernel(m_ref, g_ref, wk_ref, wv_ref, k_ref, v_ref):
    m = _rms(m_ref[0], g_ref[...])
    k_ref[0] = _dot(m, wk_ref[...])
    v_ref[0] = _dot(m, wv_ref[...])


def _full(shape):
    nd = len(shape)
    return pl.BlockSpec(shape, lambda b, t: (0,) * nd)


def _per_batch(shape_tail, nb):
    nd = len(shape_tail)
    return pl.BlockSpec((nb,) + shape_tail, lambda b, t: (b,) + (0,) * nd)


def _state_in(shape_tail, nb, layer):
    nd = len(shape_tail)
    return pl.BlockSpec((None, nb) + shape_tail, lambda b, t: (layer, b) + (0,) * nd)


def _tokens(nb, tb, width):
    return pl.BlockSpec((nb, tb, width), lambda b, t: (b, t, 0))


def _params():
    return pltpu.CompilerParams(dimension_semantics=("arbitrary", "arbitrary"),
                                vmem_limit_bytes=VMEM_LIMIT)


def _call(kern, name, grid, in_arrays, in_specs, out_shapes, out_specs, scratch):
    return pl.pallas_call(
        kern, name=name, grid=grid, in_specs=in_specs, out_specs=out_specs,
        out_shape=out_shapes, scratch_shapes=scratch, compiler_params=_params())(*in_arrays)


def _row(v):
    return v.reshape(1, -1).astype(F32)


def _lru_call(x, lp, conv0, h0, layer, nb, tb):
    B, T, D = x.shape
    c = lp["lru_cw"].shape[-1]
    grid = (B // nb, T // tb)
    ins = [x, lp["g_pre"], lp["w_a"], lp["lru_cw"], lp["lru_cb"], lp["lru_wg"], lp["lru_bg"], lp["lru_lam"],
           conv0, h0]
    specs = [_tokens(nb, tb, D)] + [_full(a.shape) for a in ins[1:8]] + [
        _state_in((CONV_W - 1, c), nb, layer), _state_in((1, c), nb, layer)]
    outs = [jax.ShapeDtypeStruct((B, T, c), F32), jax.ShapeDtypeStruct((B, CONV_W - 1, c), F32),
            jax.ShapeDtypeStruct((B, 1, c), F32)]
    ospecs = [_tokens(nb, tb, c), _per_batch((CONV_W - 1, c), nb), _per_batch((1, c), nb)]
    scratch = [pltpu.VMEM((nb, tb + CONV_PAD, c), F32), pltpu.VMEM((nb, 1, c), F32)]
    return _call(functools.partial(_lru_kernel, nb=nb, tb=tb), "mix_lru", grid, ins, specs, outs, ospecs, scratch)


def _rwkv_call(x, lp, consts, shift0, s0, layer, nb, tb, L):
    B, T, D = x.shape
    c = lp["rw_w0"].shape[-1]
    nsh = shift0.shape[-1]
    npair = c // PAIR
    grid = (B // nb, T // tb)
    ins = [x, lp["g_pre"], lp["w_b"], lp["rw_mu"], lp["rw_w0"], lp["rw_a0"], lp["rw_wup"], lp["rw_kk"],
           lp["rw_ka"], lp["rw_rk"], lp["rw_lng"], lp["rw_lnb"], consts["seg"], consts["tri"], shift0, s0]
    specs = [_tokens(nb, tb, D)] + [_full(a.shape) for a in ins[1:14]] + [
        _state_in((1, nsh), nb, layer), _state_in((npair, PAIR, PAIR), nb, layer)]
    outs = [jax.ShapeDtypeStruct((B, T, c), F32), jax.ShapeDtypeStruct((B, 1, nsh), F32),
            jax.ShapeDtypeStruct((B, npair, PAIR, PAIR), F32)]
    ospecs = [_tokens(nb, tb, c), _per_batch((1, nsh), nb), _per_batch((npair, PAIR, PAIR), nb)]
    rows = nb * tb
    scratch = [pltpu.VMEM((nb, 1, nsh), F32), pltpu.VMEM((nb, npair, PAIR, PAIR), F32)] + [
        pltpu.VMEM((rows, c), F32) for _ in range(7)]
    return _call(functools.partial(_rwkv_kernel, nb=nb, tb=tb, L=L), "mix_rwkv", grid, ins, specs, outs,
                 ospecs, scratch)


def _ssd_call(x, lp, consts, conv0, h0, layer, nb, tb, L):
    B, T, D = x.shape
    c = lp["ssd_dtb"].shape[-1]
    nconv = lp["ssd_cw"].shape[-1]
    nbc = (nconv - c) // 2
    npair = c // PAIR
    grid = (B // nb, T // tb)
    ins = [x, lp["g_pre"], lp["w_c"], lp["ssd_cw"], lp["ssd_cb"], lp["ssd_dtb"], lp["ssd_alog"], lp["ssd_d"],
           lp["ssd_ng"], consts["tri"], consts["pick"], conv0, h0]
    specs = [_tokens(nb, tb, D)] + [_full(a.shape) for a in ins[1:11]] + [
        _state_in((CONV_W - 1, nconv), nb, layer), _state_in((npair, PAIR, PAIR), nb, layer)]
    outs = [jax.ShapeDtypeStruct((B, T, c), F32), jax.ShapeDtypeStruct((B, CONV_W - 1, nconv), F32),
            jax.ShapeDtypeStruct((B, npair, PAIR, PAIR), F32)]
    ospecs = [_tokens(nb, tb, c), _per_batch((CONV_W - 1, nconv), nb), _per_batch((npair, PAIR, PAIR), nb)]
    rows = nb * tb
    scratch = [pltpu.VMEM((nb, tb + CONV_PAD, nconv), F32), pltpu.VMEM((nb, npair, PAIR, PAIR), F32),
               pltpu.VMEM((rows, c), F32), pltpu.VMEM((rows, nbc), F32), pltpu.VMEM((rows, nbc), F32),
               pltpu.VMEM((rows, c), F32), pltpu.VMEM((rows, c), F32), pltpu.VMEM((rows, c), F32)]
    return _call(functools.partial(_ssd_kernel, nb=nb, tb=tb, L=L), "mix_ssd", grid, ins, specs, outs,
                 ospecs, scratch)


def _hgrn_call(x, lp, consts, s0, layer, nb, tb, L):
    B, T, D = x.shape
    c = lp["hg_ng"].shape[-1]
    npair = c // PAIR
    grid = (B // nb, T // tb)
    ins = [x, lp["g_pre"], lp["w_d"], lp["hg_lbl"], lp["hg_ng"], consts["tri"], consts["lmask"], consts["seg"], s0]
    specs = [_tokens(nb, tb, D)] + [_full(a.shape) for a in ins[1:8]] + [_state_in((npair, PAIR, PAIR), nb, layer)]
    outs = [jax.ShapeDtypeStruct((B, T, c), F32), jax.ShapeDtypeStruct((B, npair, PAIR, PAIR), F32)]
    ospecs = [_tokens(nb, tb, c), _per_batch((npair, PAIR, PAIR), nb)]
    rows = nb * tb
    scratch = [pltpu.VMEM((nb, npair, PAIR, PAIR), F32)] + [pltpu.VMEM((rows, c), F32) for _ in range(5)]
    return _call(functools.partial(_hgrn_kernel, nb=nb, tb=tb, L=L, layer=layer), "mix_hgrn", grid, ins, specs,
                 outs, ospecs, scratch)


def _post_call(ys, x, lp, mk, mv, layer, nb, tb, heads):
    B, T, D = x.shape
    c = ys[0].shape[-1]
    M = mk.shape[2]
    grid = (B // nb, T // tb)
    ins = list(ys) + [x, lp["w_out"], lp["g_post"], lp["g_pre_x"], lp["w_q"], mk, mv, lp["w_o"], lp["g_post_x"]]
    kv_spec = pl.BlockSpec((None, nb, M, D), lambda b, t: (layer, b, 0, 0))
    specs = [_tokens(nb, tb, c)] * 4 + [_tokens(nb, tb, D)] + [_full(a.shape) for a in ins[5:9]] + [
        kv_spec, kv_spec] + [_full(a.shape) for a in ins[11:13]]
    return _call(functools.partial(_post_kernel, nb=nb, tb=tb, heads=heads), "post_attn", grid, ins, specs,
                 jax.ShapeDtypeStruct((B, T, D), F32), _tokens(nb, tb, D), [])


def _memkv_call(mem, g, wk, wv):
    B, M, D = mem.shape
    depth = wk.shape[0]
    wspec = pl.BlockSpec((None, D, D), lambda l, b: (l, 0, 0))
    ospec = pl.BlockSpec((None, 1, M, D), lambda l, b: (l, b, 0, 0))
    return pl.pallas_call(
        _memkv_kernel, name="mem_kv", grid=(depth, B),
        in_specs=[pl.BlockSpec((1, M, D), lambda l, b: (b, 0, 0)),
                  pl.BlockSpec((None, 1, D), lambda l, b: (l, 0, 0)), wspec, wspec],
        out_specs=[ospec, ospec],
        out_shape=[jax.ShapeDtypeStruct((depth, B, M, D), F32)] * 2,
        compiler_params=_params())(mem, g, wk, wv)


def _chunk_consts(L, c):
    r = np.arange(L)[:, None]
    j = np.arange(L)[None, :]
    tri = (j <= r).astype(np.float32)
    masks = []
    col = np.arange(2 * L)[None, :]
    s = col % L
    m = 1
    while (1 << m) <= L:
        size, half = 1 << m, 1 << (m - 1)
        masks.append(((r // size == s // size) & (r % size >= half) & (s % size < half)).astype(np.float32))
        m += 1
    seg = (np.arange(c)[:, None] // HEAD == np.arange(c)[None, :] // HEAD).astype(np.float32)
    pick = np.zeros((SUBLANES, PAIR), np.float32)
    pick[0, 0] = 1.0
    pick[1, HEAD] = 1.0
    return {
        "tri": jnp.asarray(tri, BF16),
        "lmask": jnp.asarray(np.stack(masks), F32),
        "seg": jnp.asarray(seg, BF16),
        "pick": jnp.asarray(pick, BF16),
    }


def _to_pairs(s):
    lead, (H, a, b) = s.shape[:-3], s.shape[-3:]
    s = s.reshape(lead + (H // 2, 2, a, b))
    z = jnp.zeros_like(s[..., 0, :, :])
    top = jnp.concatenate([s[..., 0, :, :], z], axis=-1)
    bot = jnp.concatenate([z, s[..., 1, :, :]], axis=-1)
    return jnp.concatenate([top, bot], axis=-2)


def _from_pairs(s):
    lead, P = s.shape[:-3], s.shape[-3]
    both = jnp.stack([s[..., :HEAD, :HEAD], s[..., HEAD:, HEAD:]], axis=-3)
    return both.reshape(lead + (2 * P, HEAD, HEAD))


def _expand_bc(t, c, n):
    xs, bm, cm = t[..., :c], t[..., c:c + 2 * n], t[..., c + 2 * n:]
    dup = lambda u: jnp.concatenate([u[..., :n], u[..., :n], u[..., n:], u[..., n:]], axis=-1)
    return jnp.concatenate([xs, dup(bm), dup(cm)], axis=-1)


def _shrink_bc(t, c, n):
    pick = lambda u: jnp.concatenate([u[..., :n], u[..., 2 * n:3 * n]], axis=-1)
    return jnp.concatenate([t[..., :c], pick(t[..., c:c + 4 * n]), pick(t[..., c + 4 * n:])], axis=-1)


def _blockdiag(w):
    n, d, e = w.shape
    eye = jnp.eye(n, dtype=w.dtype)
    return (eye[:, None, :, None] * w[:, :, None, :]).reshape(n * d, n * e)


def kernel(x_prompt, x_sample, state_lru_conv, state_lru_h, state_rwkv_shift, state_rwkv_wkv, state_ssd_conv, state_ssd_h, state_hgrn_s, cache_mem_k, cache_mem_v, mem_prompt, g_pre, g_post, g_pre_x, g_post_x, w_in, w_out, lru_conv_w, lru_conv_b, lru_w_r, lru_b_r, lru_w_i, lru_b_i, lru_lambda, rwkv_mu, rwkv_w0, rwkv_w_up, rwkv_a0, rwkv_a_up, rwkv_k_k, rwkv_k_a, rwkv_r_k, rwkv_ln_g, rwkv_ln_b, ssd_conv_w, ssd_conv_b, ssd_dt_bias, ssd_a_log, ssd_d, ssd_norm_g, hgrn_lb_logits, hgrn_norm_g, mem_g, mem_w_q, mem_w_k, mem_w_v, mem_w_o):
    depth = w_in.shape[0]
    D = x_prompt.shape[-1]
    c = lru_conv_w.shape[-1]
    nsh = rwkv_mu.shape[-1]
    rank = (nsh - 3 * c) // 2
    ssd_heads = ssd_dt_bias.shape[-1]
    nstate = (ssd_conv_w.shape[-1] - c) // 4
    heads_x = cache_mem_k.shape[-2]
    mem_len = mem_prompt.shape[1]

    o_b = 2 * c
    o_c = o_b + nsh + c
    o_dt = o_c + c + 4 * nstate
    o_z = o_dt + ssd_heads
    o_d = o_z + c

    layers = []
    for l in range(depth):
        w = w_in[l]
        zero = jnp.zeros((rank, c), F32)
        wup = jnp.concatenate([jnp.concatenate([rwkv_w_up[l], zero], axis=1),
                               jnp.concatenate([zero, rwkv_a_up[l]], axis=1)], axis=0)
        rep = lambda v: jnp.repeat(v, c // ssd_heads, axis=-1)
        w_c = jnp.concatenate([_expand_bc(w[:, o_c:o_dt], c, nstate), rep(w[:, o_dt:o_z]), w[:, o_z:o_d]], axis=1)
        layers.append({
            "g_pre": _row(g_pre[l]), "g_post": _row(g_post[l]), "g_pre_x": _row(g_pre_x[l]),
            "g_post_x": _row(g_post_x[l]),
            "w_a": w[:, :o_b].astype(BF16), "w_b": w[:, o_b:o_c].astype(BF16), "w_c": w_c.astype(BF16),
            "w_d": w[:, o_d:].astype(BF16),
            "w_out": w_out[l].astype(BF16), "w_q": mem_w_q[l].astype(BF16), "w_o": mem_w_o[l].astype(BF16),
            "lru_cw": lru_conv_w[l], "lru_cb": _row(lru_conv_b[l]),
            "lru_wg": jnp.concatenate([_blockdiag(lru_w_r[l]), _blockdiag(lru_w_i[l])], axis=1).astype(BF16),
            "lru_bg": _row(jnp.concatenate([lru_b_r[l], lru_b_i[l]])), "lru_lam": _row(lru_lambda[l]),
            "rw_mu": _row(rwkv_mu[l]), "rw_w0": _row(rwkv_w0[l]), "rw_a0": _row(rwkv_a0[l]),
            "rw_wup": wup.astype(BF16), "rw_kk": _row(rwkv_k_k[l]), "rw_ka": _row(rwkv_k_a[l]),
            "rw_rk": _row(rwkv_r_k[l]), "rw_lng": _row(rwkv_ln_g[l]), "rw_lnb": _row(rwkv_ln_b[l]),
            "ssd_cw": _expand_bc(ssd_conv_w[l], c, nstate), "ssd_cb": _row(_expand_bc(ssd_conv_b[l], c, nstate)),
            "ssd_dtb": _row(rep(ssd_dt_bias[l])), "ssd_alog": _row(rep(ssd_a_log[l])), "ssd_d": _row(rep(ssd_d[l])),
            "ssd_ng": _row(ssd_norm_g[l]),
            "hg_lbl": hgrn_lb_logits.astype(F32), "hg_ng": _row(hgrn_norm_g[l]),
        })

    def run(x, mk, mv, conv_a, h_a, shift_b, wkv_b, conv_c, h_c, s_d, nb, tb, L):
        consts = _chunk_consts(L, c)
        B = x.shape[0]
        h_a = h_a.reshape(depth, B, 1, c)
        shift_b = shift_b.reshape(depth, B, 1, nsh)
        wkv_b = _to_pairs(wkv_b)
        conv_c = _expand_bc(conv_c, c, nstate)
        h_c = _to_pairs(h_c)
        s_d = _to_pairs(jnp.swapaxes(s_d, -1, -2))
        acc = [[] for _ in range(7)]
        for l in range(depth):
            lp = layers[l]
            ya, nca, nha = _lru_call(x, lp, conv_a, h_a, l, nb, tb)
            yb, nsb, nwb = _rwkv_call(x, lp, consts, shift_b, wkv_b, l, nb, tb, L)
            yc, ncc, nhc = _ssd_call(x, lp, consts, conv_c, h_c, l, nb, tb, L)
            yd, nsd = _hgrn_call(x, lp, consts, s_d, l, nb, tb, L)
            x = _post_call((ya, yb, yc, yd), x, lp, mk, mv, l, nb, tb, heads_x)
            for lst, val in zip(acc, (nca, nha, nsb, nwb, ncc, nhc, nsd)):
                lst.append(val)
        nca, nha, nsb, nwb, ncc, nhc, nsd = (jnp.stack(v) for v in acc)
        return (x, nca, nha.reshape(depth, B, c), nsb.reshape(depth, B, nsh), _from_pairs(nwb),
                _shrink_bc(ncc, c, nstate), _from_pairs(nhc), jnp.swapaxes(_from_pairs(nsd), -1, -2))

    Bp, Tp = x_prompt.shape[:2]
    Bs, Ts = x_sample.shape[:2]
    k_p, v_p = _memkv_call(mem_prompt, mem_g.reshape(depth, 1, D).astype(F32), mem_w_k.astype(BF16),
                           mem_w_v.astype(BF16))
    kv_shape = (depth, Bp, mem_len, heads_x, D // heads_x)
    mem_k_p = k_p.reshape(kv_shape)
    mem_v_p = v_p.reshape(kv_shape)
    zeros = lambda *s: jnp.zeros((depth, Bp) + s, F32)
    Lp = CHUNK if Tp % CHUNK == 0 else Tp
    Ls = CHUNK if Ts % CHUNK == 0 else Ts
    tb_p = PROMPT_CHUNKS * Lp if Tp % (PROMPT_CHUNKS * Lp) == 0 else Lp
    nb_s = SAMPLE_BATCH if Bs % SAMPLE_BATCH == 0 else 1
    (y_prompt, lru_conv_p, lru_h_p, rwkv_shift_p, rwkv_wkv_p, ssd_conv_p, ssd_h_p, hgrn_s_p) = run(
        x_prompt, k_p, v_p,
        zeros(CONV_W - 1, c), zeros(c), zeros(nsh), zeros(c // HEAD, HEAD, HEAD),
        zeros(CONV_W - 1, c + 4 * nstate), zeros(ssd_heads, HEAD, nstate), zeros(c // HEAD, HEAD, HEAD),
        1, tb_p, Lp)
    (y_sample, lru_conv_s, lru_h_s, rwkv_shift_s, rwkv_wkv_s, ssd_conv_s, ssd_h_s, hgrn_s_s) = run(
        x_sample, cache_mem_k.reshape(depth, Bs, mem_len, D), cache_mem_v.reshape(depth, Bs, mem_len, D),
        state_lru_conv, state_lru_h, state_rwkv_shift, state_rwkv_wkv,
        state_ssd_conv, state_ssd_h, state_hgrn_s, nb_s, Ts, Ls)
    return (y_prompt, y_sample, lru_conv_p, lru_conv_s, lru_h_p, lru_h_s, rwkv_shift_p, rwkv_shift_s,
            rwkv_wkv_p, rwkv_wkv_s, ssd_conv_p, ssd_conv_s, ssd_h_p, ssd_h_s, hgrn_s_p, hgrn_s_s,
            mem_k_p, mem_v_p)
```

```python
import functools

import numpy as np
import jax
import jax.numpy as jnp
from jax import lax
from jax.experimental import pallas as pl
from jax.experimental.pallas import tpu as pltpu

F32 = jnp.float32
BF16 = jnp.bfloat16

SUBLANES = 8
MXU_TILE = 256
HEAD = 64
PAIR = 2 * HEAD
CHUNK = 64
CONV_W = 4
CONV_PAD = 8
LRU_C = 8.0
RWKV_DECAY_SCALE = 0.6065306597126334
RWKV_LN_EPS = 64e-5
NORM_EPS = 1e-6
VMEM_LIMIT = 56 * 1024 * 1024
PROMPT_CHUNKS = 8
SAMPLE_BATCH = 8


def _dot(a, b):
    return jnp.dot(a.astype(BF16), b.astype(BF16), preferred_element_type=F32)


def _dot_nt(a, b):
    return lax.dot_general(a.astype(BF16), b.astype(BF16), (((1,), (1,)), ((), ())),
                           preferred_element_type=F32)


def _dot_tn(a, b):
    return lax.dot_general(a.astype(BF16), b.astype(BF16), (((0,), (0,)), ((), ())),
                           preferred_element_type=F32)


def _split3(x):
    hi = x.astype(BF16)
    r1 = x - hi.astype(F32)
    mid = r1.astype(BF16)
    lo = (r1 - mid.astype(F32)).astype(BF16)
    return hi, mid, lo


def _group_size(n):
    for g in (8, 4, 2):
        if n % g == 0:
            return g
    return 1


def _dot01(m01, x):
    hi, mid, lo = _split3(x)
    d = lambda p: jnp.dot(m01, p, preferred_element_type=F32)
    return d(hi) + d(mid) + d(lo)


def _dot01_nt(m01, x):
    hi, mid, lo = _split3(x)
    d = lambda p: lax.dot_general(m01, p, (((1,), (1,)), ((), ())), preferred_element_type=F32)
    return d(hi) + d(mid) + d(lo)


def _rms(x, g):
    return x * lax.rsqrt(jnp.mean(x * x, axis=-1, keepdims=True) + NORM_EPS) * g


def _sigmoid(x):
    return jax.nn.sigmoid(x)


def _silu(x):
    return x * jax.nn.sigmoid(x)


def _log1pexp_neg_abs(x):
    return jnp.log(1.0 + jnp.exp(-jnp.abs(x)))


def _softplus(x):
    return jnp.maximum(x, 0.0) + _log1pexp_neg_abs(x)


def _segsum(x, seg_ref):
    w = min(MXU_TILE, x.shape[-1])
    seg = seg_ref[0:w, 0:w]
    return jnp.concatenate([_dot(x[:, s:s + w], seg) for s in range(0, x.shape[-1], w)], axis=1)


def _neg_expm1(z):
    return -jnp.tanh(0.5 * z) * (jnp.exp(z) + 1.0)


def _head_masks():
    lane = lax.broadcasted_iota(jnp.int32, (1, PAIR), 1)
    return lane < HEAD, lane >= HEAD


def _stack_heads(x, m0, m1):
    return jnp.concatenate([jnp.where(m0, x, 0.0), jnp.where(m1, x, 0.0)], axis=0)


def _bd_mask():
    r = lax.broadcasted_iota(jnp.int32, (PAIR, PAIR), 0)
    c = lax.broadcasted_iota(jnp.int32, (PAIR, PAIR), 1)
    return (r >= HEAD) == (c >= HEAD)


def _cat_masks(L):
    t = lax.broadcasted_iota(jnp.int32, (L, 2 * L), 0)
    c = lax.broadcasted_iota(jnp.int32, (L, 2 * L), 1)
    s = jnp.where(c >= L, c - L, c)
    return s < t, s <= t, c < L


def _cat_to_blockdiag(m, first):
    return jnp.concatenate([jnp.where(first, m, 0.0), jnp.where(first, 0.0, m)], axis=0)


def _tri_inv(nns, L):
    n = nns[0].shape[0]
    ri = lax.broadcasted_iota(jnp.int32, (n, n), 0)
    ci = lax.broadcasted_iota(jnp.int32, (n, n), 1)
    tinv = [jnp.where((ri >> 1) == (ci >> 1), t, 0.0) + (ri == ci).astype(F32) for t in nns]
    shift = 1
    while (1 << shift) < L:
        join = ((ri >> (shift + 1)) == (ci >> (shift + 1))) & ((ri >> shift) != (ci >> shift))
        w = [_dot(t, jnp.where(join, x, 0.0)) for t, x in zip(tinv, nns)]
        tinv = [t + _dot(w_, t) for t, w_ in zip(tinv, w)]
        shift += 1
    return tinv


def _mid_rows(b, m):
    n, c = b.shape
    size, half = 1 << m, 1 << (m - 1)
    if size >= 2 * SUBLANES:
        return jnp.concatenate(
            [jnp.broadcast_to(b[s + half - 1:s + half, :], (size, c)) for s in range(0, n, size)], axis=0)
    b3 = b.reshape(n // SUBLANES, SUBLANES, c)
    sub = lax.broadcasted_iota(jnp.int32, b3.shape, 1)
    mids = list(range(half - 1, SUBLANES, size))
    out = jnp.broadcast_to(b3[:, mids[-1]:mids[-1] + 1, :], b3.shape)
    for mid in reversed(mids[:-1]):
        out = jnp.where(sub <= mid + half, b3[:, mid:mid + 1, :], out)
    return out.reshape(n, c)


def _shift_rows(x, d, fill):
    row = lax.broadcasted_iota(jnp.int32, x.shape, 0)
    return jnp.where(row >= d, pltpu.roll(x, d, axis=0), fill)


def _affine_scan(a, u, h_prev):
    n, c = a.shape
    sub = lax.broadcasted_iota(jnp.int32, (n, c), 0) & (SUBLANES - 1)
    d = 1
    while d < SUBLANES:
        keep = sub >= d
        a_s = jnp.where(keep, pltpu.roll(a, d, axis=0), 1.0)
        u_s = jnp.where(keep, pltpu.roll(u, d, axis=0), 0.0)
        u = a * u_s + u
        a = a * a_s
        d *= 2
    nt = n // SUBLANES
    a3 = a.reshape(nt, SUBLANES, c)
    u3 = u.reshape(nt, SUBLANES, c)
    ta = jnp.broadcast_to(a3[:, SUBLANES - 1:SUBLANES, :], a3.shape)
    tu = jnp.broadcast_to(u3[:, SUBLANES - 1:SUBLANES, :], u3.shape)
    carry = jnp.broadcast_to(h_prev, (SUBLANES, c))
    carries = []
    for k in range(nt):
        carries.append(carry)
        carry = ta[k] * carry + tu[k]
    return u + a * jnp.concatenate(carries, axis=0)


def _causal_conv(cbuf, b, xa, tb, w_ref, bias):
    cbuf[b, CONV_PAD:CONV_PAD + tb, :] = xa
    y = bias + xa * w_ref[CONV_W - 1:CONV_W, :]
    for j in range(1, CONV_W):
        y = y + cbuf[b, CONV_PAD - j:CONV_PAD - j + tb, :] * w_ref[CONV_W - 1 - j:CONV_W - j, :]
    hist = cbuf[b, tb + CONV_PAD - (CONV_W - 1):tb + CONV_PAD, :]
    cbuf[b, CONV_PAD - (CONV_W - 1):CONV_PAD, :] = hist
    return y


def _first_step():
    return pl.program_id(1) == 0


def _last_step():
    return pl.program_id(1) == pl.num_programs(1) - 1


def _lru_kernel(x_ref, g_ref, w_ref, cw_ref, cb_ref, wg_ref, bg_ref, lam_ref, conv0_ref, h0_ref,
                y_ref, convo_ref, ho_ref, cbuf, hbuf, *, nb, tb):
    c = y_ref.shape[-1]

    @pl.when(_first_step())
    def _():
        cbuf[:, CONV_PAD - (CONV_W - 1):CONV_PAD, :] = conv0_ref[...]
        hbuf[...] = h0_ref[...]

    x = x_ref[...].reshape(nb * tb, x_ref.shape[-1])
    p = _dot(_rms(x, g_ref[...]), w_ref[...])
    xa = p[:, :c]
    ga = p[:, c:]
    xc = jnp.concatenate(
        [_causal_conv(cbuf, b, xa[b * tb:(b + 1) * tb], tb, cw_ref, cb_ref[...]) for b in range(nb)],
        axis=0)
    gates = _dot(xc, wg_ref[...]) + bg_ref[...]
    r = _sigmoid(gates[:, :c])
    i = _sigmoid(gates[:, c:])
    log_a = (-LRU_C) * r * _softplus(-lam_ref[...])
    a = jnp.exp(log_a)
    u = jnp.sqrt(_neg_expm1(2.0 * log_a)) * (i * xc)
    hs = []
    for b in range(nb):
        h = _affine_scan(a[b * tb:(b + 1) * tb], u[b * tb:(b + 1) * tb], hbuf[b])
        hbuf[b] = h[tb - 1:tb, :]
        hs.append(h)
    h = jnp.concatenate(hs, axis=0)
    y_ref[...] = (h * _silu(ga)).reshape(nb, tb, c)

    @pl.when(_last_step())
    def _():
        convo_ref[...] = cbuf[:, CONV_PAD - (CONV_W - 1):CONV_PAD, :]
        ho_ref[...] = hbuf[...]


def _rwkv_kernel(x_ref, g_ref, w_ref, mu_ref, w0_ref, a0_ref, wup_ref, kk_ref, ka_ref, rk_ref,
                 lng_ref, lnb_ref, seg_ref, tri_ref, shift0_ref, s0_ref,
                 y_ref, shifto_ref, so_ref,
                 shbuf, sbuf, lw_s, kk_s, kka_s, kp_s, r_s, v_s, y_s, *, nb, tb, L):
    c = y_ref.shape[-1]
    nc = tb // L
    npair = c // PAIR

    @pl.when(_first_step())
    def _():
        shbuf[...] = shift0_ref[...]
        sbuf[...] = s0_ref[...]

    x = x_ref[...].reshape(nb * tb, x_ref.shape[-1])
    p = _dot(_rms(x, g_ref[...]), w_ref[...])
    nsh = shbuf.shape[-1]
    pb = p[:, :nsh]
    gb = p[:, nsh:]
    xms = []
    for b in range(nb):
        slab = pb[b * tb:(b + 1) * tb]
        sh = _shift_rows(slab, 1, shbuf[b])
        shbuf[b] = slab[tb - 1:tb, :]
        xms.append(slab + (sh - slab) * mu_ref[...])
    xm = jnp.concatenate(xms, axis=0)
    r = xm[:, 0:c]
    k = xm[:, c:2 * c]
    v = xm[:, 2 * c:3 * c]
    lowrank = xm[:, 3 * c:]
    lane = lax.broadcasted_iota(jnp.int32, (1, lowrank.shape[-1]), 1)
    lowrank = jnp.where(lane < lowrank.shape[-1] // 2, jnp.tanh(lowrank), lowrank)
    up = _dot(lowrank, wup_ref[...])
    lw = (-RWKV_DECAY_SCALE) * _sigmoid(w0_ref[...] + up[:, :c])
    a = _sigmoid(a0_ref[...] + up[:, c:])
    kk = k * kk_ref[...]
    kk = kk * lax.rsqrt(_segsum(kk * kk, seg_ref) + 1e-12)
    kp = k * (1.0 + (a - 1.0) * ka_ref[...])
    lw_s[...] = lw
    kk_s[...] = kk
    kka_s[...] = kk * a
    kp_s[...] = kp
    r_s[...] = r
    v_s[...] = v
    bonus = _segsum(r * kp * rk_ref[...], seg_ref) * v

    m0, m1 = _head_masks()
    bd = _bd_mask()
    strict, incl, first = _cat_masks(L)
    tri = tri_ref[...]

    grp = _group_size(nb * nc)
    chains = [(g, j) for g in range(grp) for j in range(npair)]
    lanes = lambda j: slice(j * PAIR, (j + 1) * PAIR)

    def group(i, carry):
        per = []
        for g in range(grp):
            it = i * grp + g
            rows = pl.ds(pl.multiple_of(it * L, L), L)
            lw_c = lw_s[rows, :]
            cl = _dot01(tri, lw_c)
            cl_last = cl[L - 1:L, :]
            e_ip = jnp.exp(-cl)
            e_rel = jnp.exp(cl_last - cl)
            kk_c = kk_s[rows, :]
            kka_c = kka_s[rows, :]
            kp_c = kp_s[rows, :]
            per.append(dict(
                b=it // nc, rows=rows,
                ab=-kk_c * jnp.exp(cl - lw_c), bb=kka_c * e_ip, kb=kp_c * e_ip,
                rb=r_s[rows, :] * jnp.exp(cl), bbl=kka_c * e_rel, kbl=kp_c * e_rel,
                p_last=jnp.exp(cl_last), v=v_s[rows, :]))
        part = lambda key: [per[g][key][:, lanes(j)] for g, j in chains]
        abp, rbp, vp = part("ab"), part("rb"), part("v")
        bblp, kblp = part("bbl"), part("kbl")
        zb = [_stack_heads(t, m0, m1) for t in part("bb")]
        zk = [_stack_heads(t, m0, m1) for t in part("kb")]
        arp = [jnp.concatenate([a, r], axis=0) for a, r in zip(abp, rbp)]
        g1 = [_dot_nt(x_, jnp.concatenate([b_, k_], axis=0))
              for x_, b_, k_ in zip(arp, zb, zk)]
        nn = [_cat_to_blockdiag(jnp.where(strict, t[0:L, 0:2 * L], 0.0), first) for t in g1]
        mk = [_cat_to_blockdiag(jnp.where(strict, t[0:L, 2 * L:4 * L], 0.0), first) for t in g1]
        r_b = [jnp.where(incl, t[L:2 * L, 0:2 * L], 0.0) for t in g1]
        r_k = [jnp.where(incl, t[L:2 * L, 2 * L:4 * L], 0.0) for t in g1]
        tinv = _tri_inv(nn, L)
        vst = [_stack_heads(t, m0, m1) for t in vp]
        kv = [_dot(jnp.concatenate([m_, k_], axis=0), v_) for m_, k_, v_ in zip(mk, r_k, vst)]
        bkl = [jnp.concatenate([b_, k_], axis=0) for b_, k_ in zip(bblp, kblp)]
        for g in range(grp):
            ns = [n for n, (gg, _) in enumerate(chains) if gg == g]
            b, rows, p_last = per[g]["b"], per[g]["rows"], per[g]["p_last"]
            s = [sbuf[b, j] for j in range(npair)]
            ars = [_dot_nt(arp[n], s[j]) for j, n in enumerate(ns)]
            ust = [_dot(tinv[n], _stack_heads(ars[j][0:L], m0, m1) + kv[n][0:2 * L]) for j, n in enumerate(ns)]
            yv = [ars[j][L:2 * L] + _dot(r_b[n], ust[j]) + kv[n][2 * L:3 * L] for j, n in enumerate(ns)]
            su = [_dot_tn(jnp.concatenate([ust[j][0:L] + ust[j][L:2 * L], vp[n]], axis=0), bkl[n])
                  for j, n in enumerate(ns)]
            for j, n in enumerate(ns):
                sbuf[b, j] = jnp.where(bd, s[j] * p_last[:, lanes(j)] + su[j], 0.0)
                y_s[rows, lanes(j)] = yv[j]
        return carry

    lax.fori_loop(0, nb * nc // grp, group, 0)

    y = y_s[...]
    inv = 1.0 / HEAD
    mean = _segsum(y, seg_ref) * inv
    yc = y - mean
    var = _segsum(yc * yc, seg_ref) * inv
    yn = yc * lax.rsqrt(var + RWKV_LN_EPS) * lng_ref[...] + lnb_ref[...]
    y_ref[...] = ((yn + bonus) * _silu(gb)).reshape(nb, tb, c)

    @pl.when(_last_step())
    def _():
        shifto_ref[...] = shbuf[...]
        so_ref[...] = sbuf[...]


def _ssd_kernel(x_ref, g_ref, w_ref, cw_ref, cb_ref, dtb_ref, alog_ref, d_ref, ng_ref, tri_ref,
                pick_ref, conv0_ref, h0_ref,
                y_ref, convo_ref, ho_ref,
                cbuf, hbuf, xs_s, bd_s, cd_s, dt_s, dta_s, y_s, *, nb, tb, L):
    c = y_ref.shape[-1]
    nc = tb // L
    npair = c // PAIR
    nconv = cbuf.shape[-1]

    @pl.when(_first_step())
    def _():
        cbuf[:, CONV_PAD - (CONV_W - 1):CONV_PAD, :] = conv0_ref[...]
        hbuf[...] = h0_ref[...]

    x = x_ref[...].reshape(nb * tb, x_ref.shape[-1])
    p = _dot(_rms(x, g_ref[...]), w_ref[...])
    xbc = p[:, :nconv]
    dtr = p[:, nconv:nconv + c]
    z = p[:, nconv + c:]
    conv = jnp.concatenate(
        [_causal_conv(cbuf, b, xbc[b * tb:(b + 1) * tb], tb, cw_ref, cb_ref[...]) for b in range(nb)],
        axis=0)
    conv = _silu(conv)
    xs = conv[:, :c]
    nbc = (nconv - c) // 2
    dt = _softplus(dtr + dtb_ref[...])
    xs_s[...] = xs
    bd_s[...] = conv[:, c:c + nbc]
    cd_s[...] = conv[:, c + nbc:]
    dt_s[...] = dt
    dta_s[...] = dt * (-jnp.exp(alog_ref[...]))

    m0, m1 = _head_masks()
    bdm = _bd_mask()
    _, incl, _ = _cat_masks(L)
    tri = tri_ref[...]
    pick = pick_ref[...]

    def cat_cols(x):
        if 2 * L == PAIR:
            return x
        return jnp.concatenate([x[:, 0:L], x[:, HEAD:HEAD + L]], axis=1)

    grp = _group_size(nb * nc)
    chains = [(g, j) for g in range(grp) for j in range(npair)]
    lanes = lambda j: slice(j * PAIR, (j + 1) * PAIR)

    def group(i, carry):
        per = []
        for g in range(grp):
            it = i * grp + g
            rows = pl.ds(pl.multiple_of(it * L, L), L)
            acs = _dot01(tri, dta_s[rows, :])
            acs_last = acs[L - 1:L, :]
            dt_c = dt_s[rows, :]
            per.append(dict(b=it // nc, rows=rows, acs=acs, dt=dt_c, xs=xs_s[rows, :], e_acs=jnp.exp(acs),
                            dl=jnp.exp(acs_last - acs) * dt_c, tot=jnp.exp(acs_last),
                            bd=bd_s[rows, :], cd=cd_s[rows, :]))
        part = lambda key: [per[g][key][:, lanes(j)] for g, j in chains]
        group_part = lambda key: [per[g][key][:, lanes(j // 2)] for g, j in chains]
        acs_p, dt_p, xs_p, dl_p = part("acs"), part("dt"), part("xs"), part("dl")
        bd_p, cd_p = group_part("bd"), group_part("cd")
        picked = [_dot01_nt(pick, jnp.concatenate([a, d_], axis=0)) for a, d_ in zip(acs_p, dt_p)]
        cb = [_dot_nt(c_, _stack_heads(b_, m0, m1)) for c_, b_ in zip(cd_p, bd_p)]
        upd = [_dot_tn(x_ * d_, b_) for x_, d_, b_ in zip(xs_p, dl_p, bd_p)]
        scores = []
        for n in range(len(chains)):
            acs_row = jnp.concatenate([picked[n][0:1, 0:L], picked[n][1:2, 0:L]], axis=1)
            dt_row = jnp.concatenate([picked[n][0:1, L:2 * L], picked[n][1:2, L:2 * L]], axis=1)
            decay = jnp.where(incl, jnp.exp(cat_cols(acs_p[n]) - acs_row), 0.0)
            scores.append(cb[n] * decay * dt_row)
        yx = [_dot(s_, _stack_heads(x_, m0, m1)) for s_, x_ in zip(scores, xs_p)]
        for g in range(grp):
            b, rows = per[g]["b"], per[g]["rows"]
            ns = [n for n, (gg, _) in enumerate(chains) if gg == g]
            h = [hbuf[b, j] for j in range(npair)]
            chg = [_dot_nt(cd_p[n], h[j]) for j, n in enumerate(ns)]
            for j, n in enumerate(ns):
                y_s[rows, lanes(j)] = yx[n] + per[g]["e_acs"][:, lanes(j)] * chg[j]
                hbuf[b, j] = jnp.where(bdm, h[j] * per[g]["tot"][:, lanes(j)] + upd[n], 0.0)
        return carry

    lax.fori_loop(0, nb * nc // grp, group, 0)

    yc = y_s[...] + d_ref[...] * xs
    y_ref[...] = _rms(yc * _silu(z), ng_ref[...]).reshape(nb, tb, c)

    @pl.when(_last_step())
    def _():
        convo_ref[...] = cbuf[:, CONV_PAD - (CONV_W - 1):CONV_PAD, :]
        ho_ref[...] = hbuf[...]


def _hgrn_kernel(x_ref, g_ref, w_ref, lbl_ref, ng_ref, tri_ref, lmask_ref, seg_ref, s0_ref,
                 y_ref, so_ref,
                 sbuf, q_s, k_s, v_s, lf_s, y_s, *, nb, tb, L, layer):
    c = y_ref.shape[-1]
    nc = tb // L
    npair = c // PAIR
    nlev = lmask_ref.shape[0]

    @pl.when(_first_step())
    def _():
        sbuf[...] = s0_ref[...]

    x = x_ref[...].reshape(nb * tb, x_ref.shape[-1])
    p = _dot(_rms(x, g_ref[...]), w_ref[...])
    qd = p[:, 0:c]
    fd = p[:, c:2 * c]
    v = p[:, 2 * c:3 * c]
    gd = p[:, 3 * c:]
    logits = lbl_ref[...]
    ex = jnp.exp(logits - jnp.max(logits, axis=0, keepdims=True))
    sm = ex / jnp.sum(ex, axis=0, keepdims=True)
    lb = jnp.zeros_like(sm[0:1])
    for i in range(1, layer + 1):
        lb = lb + sm[i:i + 1]
    log_lb = jnp.log(lb)
    b2 = jnp.log1p(-lb) - _softplus(-fd)
    logf = jnp.maximum(log_lb, b2) + _log1pexp_neg_abs(log_lb - b2)
    q = _silu(qd)
    kx = (1.0 - lb) * _sigmoid(-fd)
    q_s[...] = q
    k_s[...] = kx
    v_s[...] = v
    lf_s[...] = logf
    diag = _segsum(q * kx, seg_ref) * v

    m0, m1 = _head_masks()
    bdm = _bd_mask()
    tri = tri_ref[...]
    level = [lmask_ref[m] > 0.5 for m in range(nlev)]

    grp = _group_size(nb * nc)
    chains = [(g, j) for g in range(grp) for j in range(npair)]
    lanes = lambda j: slice(j * PAIR, (j + 1) * PAIR)

    def group(i, carry):
        per = []
        for g in range(grp):
            it = i * grp + g
            rows = pl.ds(pl.multiple_of(it * L, L), L)
            bc = _dot01(tri, lf_s[rows, :])
            b_last = bc[L - 1:L, :]
            q_c = q_s[rows, :]
            k_c = k_s[rows, :]
            e_lev = [jnp.exp(-jnp.abs(bc - _mid_rows(bc, m + 1))) for m in range(nlev)]
            per.append(dict(b=it // nc, rows=rows, v=v_s[rows, :],
                            qe=q_c * jnp.exp(bc), kl=k_c * jnp.exp(b_last - bc), tot=jnp.exp(b_last),
                            qn=[q_c * e for e in e_lev], kn=[k_c * e for e in e_lev]))
        part = lambda key: [per[g][key][:, lanes(j)] for g, j in chains]
        vp, qe_p, kl_p = part("v"), part("qe"), part("kl")
        att = [jnp.zeros((L, 2 * L), F32) for _ in chains]
        for m in range(nlev):
            lev = [_dot_nt(per[g]["qn"][m][:, lanes(j)], _stack_heads(per[g]["kn"][m][:, lanes(j)], m0, m1))
                   for g, j in chains]
            att = [jnp.where(level[m], t, a) for a, t in zip(att, lev)]
        yv = [_dot(a, _stack_heads(v_, m0, m1)) for a, v_ in zip(att, vp)]
        upd = [_dot_tn(v_, k_) for v_, k_ in zip(vp, kl_p)]
        for g in range(grp):
            b, rows = per[g]["b"], per[g]["rows"]
            ns = [n for n, (gg, _) in enumerate(chains) if gg == g]
            st = [sbuf[b, j] for j in range(npair)]
            ys = [_dot_nt(qe_p[n], st[j]) for j, n in enumerate(ns)]
            for j, n in enumerate(ns):
                y_s[rows, lanes(j)] = ys[j] + yv[n]
                sbuf[b, j] = jnp.where(bdm, st[j] * per[g]["tot"][:, lanes(j)] + upd[n], 0.0)
        return carry

    lax.fori_loop(0, nb * nc // grp, group, 0)

    o = y_s[...] + diag
    ms = _segsum(o * o, seg_ref) * (1.0 / HEAD)
    yd = o * lax.rsqrt(ms + NORM_EPS) * ng_ref[...]
    y_ref[...] = (yd * _silu(gd)).reshape(nb, tb, c)

    @pl.when(_last_step())
    def _():
        so_ref[...] = sbuf[...]


def _post_kernel(ya_ref, yb_ref, yc_ref, yd_ref, x_ref, wo_ref, gpost_ref, gprex_ref, wq_ref,
                 mk_ref, mv_ref, wox_ref, gpostx_ref, o_ref, *, nb, tb, heads):
    d = x_ref.shape[-1]
    c = ya_ref.shape[-1]
    hd = d // heads
    rows = nb * tb
    y = None
    for i, ref in enumerate((ya_ref, yb_ref, yc_ref, yd_ref)):
        t = _dot(ref[...].reshape(rows, c), wo_ref[i * c:(i + 1) * c, :])
        y = t if y is None else y + t
    x1 = x_ref[...].reshape(rows, d) + _rms(y, gpost_ref[...])
    q = _dot(_rms(x1, gprex_ref[...]), wq_ref[...])
    scale = hd ** -0.5
    outs = []
    for b in range(nb):
        qb = q[b * tb:(b + 1) * tb]
        heads_o = []
        for h in range(heads):
            hs = slice(h * hd, (h + 1) * hd)
            s = _dot_nt(qb[:, hs], mk_ref[b, :, hs]) * scale
            s = s - jnp.max(s, axis=-1, keepdims=True)
            e = jnp.exp(s)
            pr = e * (1.0 / jnp.sum(e, axis=-1, keepdims=True))
            heads_o.append(_dot(pr, mv_ref[b, :, hs]))
        outs.append(jnp.concatenate(heads_o, axis=1))
    o = jnp.concatenate(outs, axis=0)
    x2 = x1 + _rms(_dot(o, wox_ref[...]), gpostx_ref[...])
    o_ref[...] = x2.reshape(nb, tb, d)


def _memkv_kernel(m_ref, g_ref, wk_ref, wv_ref, k_ref, v_ref):
    m = _rms(m_ref[0], g_ref[...])
    k_ref[0] = _dot(m, wk_ref[...])
    v_ref[0] = _dot(m, wv_ref[...])


def _full(shape):
    nd = len(shape)
    return pl.BlockSpec(shape, lambda b, t: (0,) * nd)


def _per_batch(shape_tail, nb):
    nd = len(shape_tail)
    return pl.BlockSpec((nb,) + shape_tail, lambda b, t: (b,) + (0,) * nd)


def _state_in(shape_tail, nb, layer):
    nd = len(shape_tail)
    return pl.BlockSpec((None, nb) + shape_tail, lambda b, t: (layer, b) + (0,) * nd)


def _tokens(nb, tb, width):
    return pl.BlockSpec((nb, tb, width), lambda b, t: (b, t, 0))


def _params():
    return pltpu.CompilerParams(dimension_semantics=("arbitrary", "arbitrary"),
                                vmem_limit_bytes=VMEM_LIMIT)


def _call(kern, name, grid, in_arrays, in_specs, out_shapes, out_specs, scratch):
    return pl.pallas_call(
        kern, name=name, grid=grid, in_specs=in_specs, out_specs=out_specs,
        out_shape=out_shapes, scratch_shapes=scratch, compiler_params=_params())(*in_arrays)


def _row(v):
    return v.reshape(1, -1).astype(F32)


def _lru_call(x, lp, conv0, h0, layer, nb, tb):
    B, T, D = x.shape
    c = lp["lru_cw"].shape[-1]
    grid = (B // nb, T // tb)
    ins = [x, lp["g_pre"], lp["w_a"], lp["lru_cw"], lp["lru_cb"], lp["lru_wg"], lp["lru_bg"], lp["lru_lam"],
           conv0, h0]
    specs = [_tokens(nb, tb, D)] + [_full(a.shape) for a in ins[1:8]] + [
        _state_in((CONV_W - 1, c), nb, layer), _state_in((1, c), nb, layer)]
    outs = [jax.ShapeDtypeStruct((B, T, c), F32), jax.ShapeDtypeStruct((B, CONV_W - 1, c), F32),
            jax.ShapeDtypeStruct((B, 1, c), F32)]
    ospecs = [_tokens(nb, tb, c), _per_batch((CONV_W - 1, c), nb), _per_batch((1, c), nb)]
    scratch = [pltpu.VMEM((nb, tb + CONV_PAD, c), F32), pltpu.VMEM((nb, 1, c), F32)]
    return _call(functools.partial(_lru_kernel, nb=nb, tb=tb), "mix_lru", grid, ins, specs, outs, ospecs, scratch)


def _rwkv_call(x, lp, consts, shift0, s0, layer, nb, tb, L):
    B, T, D = x.shape
    c = lp["rw_w0"].shape[-1]
    nsh = shift0.shape[-1]
    npair = c // PAIR
    grid = (B // nb, T // tb)
    ins = [x, lp["g_pre"], lp["w_b"], lp["rw_mu"], lp["rw_w0"], lp["rw_a0"], lp["rw_wup"], lp["rw_kk"],
           lp["rw_ka"], lp["rw_rk"], lp["rw_lng"], lp["rw_lnb"], consts["seg"], consts["tri"], shift0, s0]
    specs = [_tokens(nb, tb, D)] + [_full(a.shape) for a in ins[1:14]] + [
        _state_in((1, nsh), nb, layer), _state_in((npair, PAIR, PAIR), nb, layer)]
    outs = [jax.ShapeDtypeStruct((B, T, c), F32), jax.ShapeDtypeStruct((B, 1, nsh), F32),
            jax.ShapeDtypeStruct((B, npair, PAIR, PAIR), F32)]
    ospecs = [_tokens(nb, tb, c), _per_batch((1, nsh), nb), _per_batch((npair, PAIR, PAIR), nb)]
    rows = nb * tb
    scratch = [pltpu.VMEM((nb, 1, nsh), F32), pltpu.VMEM((nb, npair, PAIR, PAIR), F32)] + [
        pltpu.VMEM((rows, c), F32) for _ in range(7)]
    return _call(functools.partial(_rwkv_kernel, nb=nb, tb=tb, L=L), "mix_rwkv", grid, ins, specs, outs,
                 ospecs, scratch)


def _ssd_call(x, lp, consts, conv0, h0, layer, nb, tb, L):
    B, T, D = x.shape
    c = lp["ssd_dtb"].shape[-1]
    nconv = lp["ssd_cw"].shape[-1]
    nbc = (nconv - c) // 2
    npair = c // PAIR
    grid = (B // nb, T // tb)
    ins = [x, lp["g_pre"], lp["w_c"], lp["ssd_cw"], lp["ssd_cb"], lp["ssd_dtb"], lp["ssd_alog"], lp["ssd_d"],
           lp["ssd_ng"], consts["tri"], consts["pick"], conv0, h0]
    specs = [_tokens(nb, tb, D)] + [_full(a.shape) for a in ins[1:11]] + [
        _state_in((CONV_W - 1, nconv), nb, layer), _state_in((npair, PAIR, PAIR), nb, layer)]
    outs = [jax.ShapeDtypeStruct((B, T, c), F32), jax.ShapeDtypeStruct((B, CONV_W - 1, nconv), F32),
            jax.ShapeDtypeStruct((B, npair, PAIR, PAIR), F32)]
    ospecs = [_tokens(nb, tb, c), _per_batch((CONV_W - 1, nconv), nb), _per_batch((npair, PAIR, PAIR), nb)]
    rows = nb * tb
    scratch = [pltpu.VMEM((nb, tb + CONV_PAD, nconv), F32), pltpu.VMEM((nb, npair, PAIR, PAIR), F32),
               pltpu.VMEM((rows, c), F32), pltpu.VMEM((rows, nbc), F32), pltpu.VMEM((rows, nbc), F32),
               pltpu.VMEM((rows, c), F32), pltpu.VMEM((rows, c), F32), pltpu.VMEM((rows, c), F32)]
    return _call(functools.partial(_ssd_kernel, nb=nb, tb=tb, L=L), "mix_ssd", grid, ins, specs, outs,
                 ospecs, scratch)


def _hgrn_call(x, lp, consts, s0, layer, nb, tb, L):
    B, T, D = x.shape
    c = lp["hg_ng"].shape[-1]
    npair = c // PAIR
    grid = (B // nb, T // tb)
    ins = [x, lp["g_pre"], lp["w_d"], lp["hg_lbl"], lp["hg_ng"], consts["tri"], consts["lmask"], consts["seg"], s0]
    specs = [_tokens(nb, tb, D)] + [_full(a.shape) for a in ins[1:8]] + [_state_in((npair, PAIR, PAIR), nb, layer)]
    outs = [jax.ShapeDtypeStruct((B, T, c), F32), jax.ShapeDtypeStruct((B, npair, PAIR, PAIR), F32)]
    ospecs = [_tokens(nb, tb, c), _per_batch((npair, PAIR, PAIR), nb)]
    rows = nb * tb
    scratch = [pltpu.VMEM((nb, npair, PAIR, PAIR), F32)] + [pltpu.VMEM((rows, c), F32) for _ in range(5)]
    return _call(functools.partial(_hgrn_kernel, nb=nb, tb=tb, L=L, layer=layer), "mix_hgrn", grid, ins, specs,
                 outs, ospecs, scratch)


def _post_call(ys, x, lp, mk, mv, layer, nb, tb, heads):
    B, T, D = x.shape
    c = ys[0].shape[-1]
    M = mk.shape[2]
    grid = (B // nb, T // tb)
    ins = list(ys) + [x, lp["w_out"], lp["g_post"], lp["g_pre_x"], lp["w_q"], mk, mv, lp["w_o"], lp["g_post_x"]]
    kv_spec = pl.BlockSpec((None, nb, M, D), lambda b, t: (layer, b, 0, 0))
    specs = [_tokens(nb, tb, c)] * 4 + [_tokens(nb, tb, D)] + [_full(a.shape) for a in ins[5:9]] + [
        kv_spec, kv_spec] + [_full(a.shape) for a in ins[11:13]]
    return _call(functools.partial(_post_kernel, nb=nb, tb=tb, heads=heads), "post_attn", grid, ins, specs,
                 jax.ShapeDtypeStruct((B, T, D), F32), _tokens(nb, tb, D), [])


def _memkv_call(mem, g, wk, wv):
    B, M, D = mem.shape
    depth = wk.shape[0]
    wspec = pl.BlockSpec((None, D, D), lambda l, b: (l, 0, 0))
    ospec = pl.BlockSpec((None, 1, M, D), lambda l, b: (l, b, 0, 0))
    return pl.pallas_call(
        _memkv_kernel, name="mem_kv", grid=(depth, B),
        in_specs=[pl.BlockSpec((1, M, D), lambda l, b: (b, 0, 0)),
                  pl.BlockSpec((None, 1, D), lambda l, b: (l, 0, 0)), wspec, wspec],
        out_specs=[ospec, ospec],
        out_shape=[jax.ShapeDtypeStruct((depth, B, M, D), F32)] * 2,
        compiler_params=_params())(mem, g, wk, wv)


def _chunk_consts(L, c):
    r = np.arange(L)[:, None]
    j = np.arange(L)[None, :]
    tri = (j <= r).astype(np.float32)
    masks = []
    col = np.arange(2 * L)[None, :]
    s = col % L
    m = 1
    while (1 << m) <= L:
        size, half = 1 << m, 1 << (m - 1)
        masks.append(((r // size == s // size) & (r % size >= half) & (s % size < half)).astype(np.float32))
        m += 1
    seg = (np.arange(c)[:, None] // HEAD == np.arange(c)[None, :] // HEAD).astype(np.float32)
    pick = np.zeros((SUBLANES, PAIR), np.float32)
    pick[0, 0] = 1.0
    pick[1, HEAD] = 1.0
    return {
        "tri": jnp.asarray(tri, BF16),
        "lmask": jnp.asarray(np.stack(masks), F32),
        "seg": jnp.asarray(seg, BF16),
        "pick": jnp.asarray(pick, BF16),
    }


def _to_pairs(s):
    lead, (H, a, b) = s.shape[:-3], s.shape[-3:]
    s = s.reshape(lead + (H // 2, 2, a, b))
    z = jnp.zeros_like(s[..., 0, :, :])
    top = jnp.concatenate([s[..., 0, :, :], z], axis=-1)
    bot = jnp.concatenate([z, s[..., 1, :, :]], axis=-1)
    return jnp.concatenate([top, bot], axis=-2)


def _from_pairs(s):
    lead, P = s.shape[:-3], s.shape[-3]
    both = jnp.stack([s[..., :HEAD, :HEAD], s[..., HEAD:, HEAD:]], axis=-3)
    return both.reshape(lead + (2 * P, HEAD, HEAD))


def _expand_bc(t, c, n):
    xs, bm, cm = t[..., :c], t[..., c:c + 2 * n], t[..., c + 2 * n:]
    dup = lambda u: jnp.concatenate([u[..., :n], u[..., :n], u[..., n:], u[..., n:]], axis=-1)
    return jnp.concatenate([xs, dup(bm), dup(cm)], axis=-1)


def _shrink_bc(t, c, n):
    pick = lambda u: jnp.concatenate([u[..., :n], u[..., 2 * n:3 * n]], axis=-1)
    return jnp.concatenate([t[..., :c], pick(t[..., c:c + 4 * n]), pick(t[..., c + 4 * n:])], axis=-1)


def _blockdiag(w):
    n, d, e = w.shape
    eye = jnp.eye(n, dtype=w.dtype)
    return (eye[:, None, :, None] * w[:, :, None, :]).reshape(n * d, n * e)


def kernel(x_prompt, x_sample, state_lru_conv, state_lru_h, state_rwkv_shift, state_rwkv_wkv, state_ssd_conv, state_ssd_h, state_hgrn_s, cache_mem_k, cache_mem_v, mem_prompt, g_pre, g_post, g_pre_x, g_post_x, w_in, w_out, lru_conv_w, lru_conv_b, lru_w_r, lru_b_r, lru_w_i, lru_b_i, lru_lambda, rwkv_mu, rwkv_w0, rwkv_w_up, rwkv_a0, rwkv_a_up, rwkv_k_k, rwkv_k_a, rwkv_r_k, rwkv_ln_g, rwkv_ln_b, ssd_conv_w, ssd_conv_b, ssd_dt_bias, ssd_a_log, ssd_d, ssd_norm_g, hgrn_lb_logits, hgrn_norm_g, mem_g, mem_w_q, mem_w_k, mem_w_v, mem_w_o):
    depth = w_in.shape[0]
    D = x_prompt.shape[-1]
    c = lru_conv_w.shape[-1]
    nsh = rwkv_mu.shape[-1]
    rank = (nsh - 3 * c) // 2
    ssd_heads = ssd_dt_bias.shape[-1]
    nstate = (ssd_conv_w.shape[-1] - c) // 4
    heads_x = cache_mem_k.shape[-2]
    mem_len = mem_prompt.shape[1]

    o_b = 2 * c
    o_c = o_b + nsh + c
    o_dt = o_c + c + 4 * nstate
    o_z = o_dt + ssd_heads
    o_d = o_z + c

    layers = []
    for l in range(depth):
        w = w_in[l]
        zero = jnp.zeros((rank, c), F32)
        wup = jnp.concatenate([jnp.concatenate([rwkv_w_up[l], zero], axis=1),
                               jnp.concatenate([zero, rwkv_a_up[l]], axis=1)], axis=0)
        rep = lambda v: jnp.repeat(v, c // ssd_heads, axis=-1)
        w_c = jnp.concatenate([_expand_bc(w[:, o_c:o_dt], c, nstate), rep(w[:, o_dt:o_z]), w[:, o_z:o_d]], axis=1)
        layers.append({
            "g_pre": _row(g_pre[l]), "g_post": _row(g_post[l]), "g_pre_x": _row(g_pre_x[l]),
            "g_post_x": _row(g_post_x[l]),
            "w_a": w[:, :o_b].astype(BF16), "w_b": w[:, o_b:o_c].astype(BF16), "w_c": w_c.astype(BF16),
            "w_d": w[:, o_d:].astype(BF16),
            "w_out": w_out[l].astype(BF16), "w_q": mem_w_q[l].astype(BF16), "w_o": mem_w_o[l].astype(BF16),
            "lru_cw": lru_conv_w[l], "lru_cb": _row(lru_conv_b[l]),
            "lru_wg": jnp.concatenate([_blockdiag(lru_w_r[l]), _blockdiag(lru_w_i[l])], axis=1).astype(BF16),
            "lru_bg": _row(jnp.concatenate([lru_b_r[l], lru_b_i[l]])), "lru_lam": _row(lru_lambda[l]),
            "rw_mu": _row(rwkv_mu[l]), "rw_w0": _row(rwkv_w0[l]), "rw_a0": _row(rwkv_a0[l]),
            "rw_wup": wup.astype(BF16), "rw_kk": _row(rwkv_k_k[l]), "rw_ka": _row(rwkv_k_a[l]),
            "rw_rk": _row(rwkv_r_k[l]), "rw_lng": _row(rwkv_ln_g[l]), "rw_lnb": _row(rwkv_ln_b[l]),
            "ssd_cw": _expand_bc(ssd_conv_w[l], c, nstate), "ssd_cb": _row(_expand_bc(ssd_conv_b[l], c, nstate)),
            "ssd_dtb": _row(rep(ssd_dt_bias[l])), "ssd_alog": _row(rep(ssd_a_log[l])), "ssd_d": _row(rep(ssd_d[l])),
            "ssd_ng": _row(ssd_norm_g[l]),
            "hg_lbl": hgrn_lb_logits.astype(F32), "hg_ng": _row(hgrn_norm_g[l]),
        })

    def run(x, mk, mv, conv_a, h_a, shift_b, wkv_b, conv_c, h_c, s_d, nb, tb, L):
        consts = _chunk_consts(L, c)
        B = x.shape[0]
        h_a = h_a.reshape(depth, B, 1, c)
        shift_b = shift_b.reshape(depth, B, 1, nsh)
        wkv_b = _to_pairs(wkv_b)
        conv_c = _expand_bc(conv_c, c, nstate)
        h_c = _to_pairs(h_c)
        s_d = _to_pairs(jnp.swapaxes(s_d, -1, -2))
        acc = [[] for _ in range(7)]
        for l in range(depth):
            lp = layers[l]
            ya, nca, nha = _lru_call(x, lp, conv_a, h_a, l, nb, tb)
            yb, nsb, nwb = _rwkv_call(x, lp, consts, shift_b, wkv_b, l, nb, tb, L)
            yc, ncc, nhc = _ssd_call(x, lp, consts, conv_c, h_c, l, nb, tb, L)
            yd, nsd = _hgrn_call(x, lp, consts, s_d, l, nb, tb, L)
            x = _post_call((ya, yb, yc, yd), x, lp, mk, mv, l, nb, tb, heads_x)
            for lst, val in zip(acc, (nca, nha, nsb, nwb, ncc, nhc, nsd)):
                lst.append(val)
        nca, nha, nsb, nwb, ncc, nhc, nsd = (jnp.stack(v) for v in acc)
        return (x, nca, nha.reshape(depth, B, c), nsb.reshape(depth, B, nsh), _from_pairs(nwb),
                _shrink_bc(ncc, c, nstate), _from_pairs(nhc), jnp.swapaxes(_from_pairs(nsd), -1, -2))

    Bp, Tp = x_prompt.shape[:2]
    Bs, Ts = x_sample.shape[:2]
    k_p, v_p = _memkv_call(mem_prompt, mem_g.reshape(depth, 1, D).astype(F32), mem_w_k.astype(BF16),
                           mem_w_v.astype(BF16))
    kv_shape = (depth, Bp, mem_len, heads_x, D // heads_x)
    mem_k_p = k_p.reshape(kv_shape)
    mem_v_p = v_p.reshape(kv_shape)
    zeros = lambda *s: jnp.zeros((depth, Bp) + s, F32)
    Lp = CHUNK if Tp % CHUNK == 0 else Tp
    Ls = CHUNK if Ts % CHUNK == 0 else Ts
    tb_p = PROMPT_CHUNKS * Lp if Tp % (PROMPT_CHUNKS * Lp) == 0 else Lp
    nb_s = SAMPLE_BATCH if Bs % SAMPLE_BATCH == 0 else 1
    (y_prompt, lru_conv_p, lru_h_p, rwkv_shift_p, rwkv_wkv_p, ssd_conv_p, ssd_h_p, hgrn_s_p) = run(
        x_prompt, k_p, v_p,
        zeros(CONV_W - 1, c), zeros(c), zeros(nsh), zeros(c // HEAD, HEAD, HEAD),
        zeros(CONV_W - 1, c + 4 * nstate), zeros(ssd_heads, HEAD, nstate), zeros(c // HEAD, HEAD, HEAD),
        1, tb_p, Lp)
    (y_sample, lru_conv_s, lru_h_s, rwkv_shift_s, rwkv_wkv_s, ssd_conv_s, ssd_h_s, hgrn_s_s) = run(
        x_sample, cache_mem_k.reshape(depth, Bs, mem_len, D), cache_mem_v.reshape(depth, Bs, mem_len, D),
        state_lru_conv, state_lru_h, state_rwkv_shift, state_rwkv_wkv,
        state_ssd_conv, state_ssd_h, state_hgrn_s, nb_s, Ts, Ls)
    return (y_prompt, y_sample, lru_conv_p, lru_conv_s, lru_h_p, lru_h_s, rwkv_shift_p, rwkv_shift_s,
            rwkv_wkv_p, rwkv_wkv_s, ssd_conv_p, ssd_conv_s, ssd_h_p, ssd_h_s, hgrn_s_p, hgrn_s_s,
            mem_k_p, mem_v_p)
```

```python
import functools

import numpy as np
import jax
import jax.numpy as jnp
from jax import lax
from jax.experimental import pallas as pl
from jax.experimental.pallas import tpu as pltpu

F32 = jnp.float32
BF16 = jnp.bfloat16

SUBLANES = 8
MXU_TILE = 256
HEAD = 64
PAIR = 2 * HEAD
CHUNK = 64
CONV_W = 4
CONV_PAD = 8
LRU_C = 8.0
RWKV_DECAY_SCALE = 0.6065306597126334
RWKV_LN_EPS = 64e-5
NORM_EPS = 1e-6
VMEM_LIMIT = 56 * 1024 * 1024
PROMPT_CHUNKS = 8
SAMPLE_BATCH = 8


def _dot(a, b):
    return jnp.dot(a.astype(BF16), b.astype(BF16), preferred_element_type=F32)


def _dot_nt(a, b):
    return lax.dot_general(a.astype(BF16), b.astype(BF16), (((1,), (1,)), ((), ())),
                           preferred_element_type=F32)


def _dot_tn(a, b):
    return lax.dot_general(a.astype(BF16), b.astype(BF16), (((0,), (0,)), ((), ())),
                           preferred_element_type=F32)


def _split3(x):
    hi = x.astype(BF16)
    r1 = x - hi.astype(F32)
    mid = r1.astype(BF16)
    lo = (r1 - mid.astype(F32)).astype(BF16)
    return hi, mid, lo


def _group_size(n):
    for g in (8, 4, 2):
        if n % g == 0:
            return g
    return 1


def _dot01(m01, x):
    hi, mid, lo = _split3(x)
    d = lambda p: jnp.dot(m01, p, preferred_element_type=F32)
    return d(hi) + d(mid) + d(lo)


def _dot01_nt(m01, x):
    hi, mid, lo = _split3(x)
    d = lambda p: lax.dot_general(m01, p, (((1,), (1,)), ((), ())), preferred_element_type=F32)
    return d(hi) + d(mid) + d(lo)


def _rms(x, g):
    return x * lax.rsqrt(jnp.mean(x * x, axis=-1, keepdims=True) + NORM_EPS) * g


def _sigmoid(x):
    return jax.nn.sigmoid(x)


def _silu(x):
    return x * jax.nn.sigmoid(x)


def _log1pexp_neg_abs(x):
    return jnp.log(1.0 + jnp.exp(-jnp.abs(x)))


def _softplus(x):
    return jnp.maximum(x, 0.0) + _log1pexp_neg_abs(x)


def _segsum(x, seg_ref):
    w = min(MXU_TILE, x.shape[-1])
    seg = seg_ref[0:w, 0:w]
    return jnp.concatenate([_dot(x[:, s:s + w], seg) for s in range(0, x.shape[-1], w)], axis=1)


def _neg_expm1(z):
    return -jnp.tanh(0.5 * z) * (jnp.exp(z) + 1.0)


def _head_masks():
    lane = lax.broadcasted_iota(jnp.int32, (1, PAIR), 1)
    return lane < HEAD, lane >= HEAD


def _stack_heads(x, m0, m1):
    return jnp.concatenate([jnp.where(m0, x, 0.0), jnp.where(m1, x, 0.0)], axis=0)


def _bd_mask():
    r = lax.broadcasted_iota(jnp.int32, (PAIR, PAIR), 0)
    c = lax.broadcasted_iota(jnp.int32, (PAIR, PAIR), 1)
    return (r >= HEAD) == (c >= HEAD)


def _cat_masks(L):
    t = lax.broadcasted_iota(jnp.int32, (L, 2 * L), 0)
    c = lax.broadcasted_iota(jnp.int32, (L, 2 * L), 1)
    s = jnp.where(c >= L, c - L, c)
    return s < t, s <= t, c < L


def _cat_to_blockdiag(m, first):
    return jnp.concatenate([jnp.where(first, m, 0.0), jnp.where(first, 0.0, m)], axis=0)


def _tri_inv(nns, L):
    n = nns[0].shape[0]
    ri = lax.broadcasted_iota(jnp.int32, (n, n), 0)
    ci = lax.broadcasted_iota(jnp.int32, (n, n), 1)
    tinv = [jnp.where((ri >> 1) == (ci >> 1), t, 0.0) + (ri == ci).astype(F32) for t in nns]
    shift = 1
    while (1 << shift) < L:
        join = ((ri >> (shift + 1)) == (ci >> (shift + 1))) & ((ri >> shift) != (ci >> shift))
        w = [_dot(t, jnp.where(join, x, 0.0)) for t, x in zip(tinv, nns)]
        tinv = [t + _dot(w_, t) for t, w_ in zip(tinv, w)]
        shift += 1
    return tinv


def _mid_rows(b, m):
    n, c = b.shape
    size, half = 1 << m, 1 << (m - 1)
    if size >= 2 * SUBLANES:
        return jnp.concatenate(
            [jnp.broadcast_to(b[s + half - 1:s + half, :], (size, c)) for s in range(0, n, size)], axis=0)
    b3 = b.reshape(n // SUBLANES, SUBLANES, c)
    sub = lax.broadcasted_iota(jnp.int32, b3.shape, 1)
    mids = list(range(half - 1, SUBLANES, size))
    out = jnp.broadcast_to(b3[:, mids[-1]:mids[-1] + 1, :], b3.shape)
    for mid in reversed(mids[:-1]):
        out = jnp.where(sub <= mid + half, b3[:, mid:mid + 1, :], out)
    return out.reshape(n, c)


def _shift_rows(x, d, fill):
    row = lax.broadcasted_iota(jnp.int32, x.shape, 0)
    return jnp.where(row >= d, pltpu.roll(x, d, axis=0), fill)


def _affine_scan(a, u, h_prev):
    n, c = a.shape
    sub = lax.broadcasted_iota(jnp.int32, (n, c), 0) & (SUBLANES - 1)
    d = 1
    while d < SUBLANES:
        keep = sub >= d
        a_s = jnp.where(keep, pltpu.roll(a, d, axis=0), 1.0)
        u_s = jnp.where(keep, pltpu.roll(u, d, axis=0), 0.0)
        u = a * u_s + u
        a = a * a_s
        d *= 2
    nt = n // SUBLANES
    a3 = a.reshape(nt, SUBLANES, c)
    u3 = u.reshape(nt, SUBLANES, c)
    ta = jnp.broadcast_to(a3[:, SUBLANES - 1:SUBLANES, :], a3.shape)
    tu = jnp.broadcast_to(u3[:, SUBLANES - 1:SUBLANES, :], u3.shape)
    carry = jnp.broadcast_to(h_prev, (SUBLANES, c))
    carries = []
    for k in range(nt):
        carries.append(carry)
        carry = ta[k] * carry + tu[k]
    return u + a * jnp.concatenate(carries, axis=0)


def _causal_conv(cbuf, b, xa, tb, w_ref, bias):
    cbuf[b, CONV_PAD:CONV_PAD + tb, :] = xa
    y = bias + xa * w_ref[CONV_W - 1:CONV_W, :]
    for j in range(1, CONV_W):
        y = y + cbuf[b, CONV_PAD - j:CONV_PAD - j + tb, :] * w_ref[CONV_W - 1 - j:CONV_W - j, :]
    hist = cbuf[b, tb + CONV_PAD - (CONV_W - 1):tb + CONV_PAD, :]
    cbuf[b, CONV_PAD - (CONV_W - 1):CONV_PAD, :] = hist
    return y


def _first_step():
    return pl.program_id(1) == 0


def _last_step():
    return pl.program_id(1) == pl.num_programs(1) - 1


def _lru_kernel(x_ref, g_ref, w_ref, cw_ref, cb_ref, wg_ref, bg_ref, lam_ref, conv0_ref, h0_ref,
                y_ref, convo_ref, ho_ref, cbuf, hbuf, *, nb, tb):
    c = y_ref.shape[-1]

    @pl.when(_first_step())
    def _():
        cbuf[:, CONV_PAD - (CONV_W - 1):CONV_PAD, :] = conv0_ref[...]
        hbuf[...] = h0_ref[...]

    x = x_ref[...].reshape(nb * tb, x_ref.shape[-1])
    p = _dot(_rms(x, g_ref[...]), w_ref[...])
    xa = p[:, :c]
    ga = p[:, c:]
    xc = jnp.concatenate(
        [_causal_conv(cbuf, b, xa[b * tb:(b + 1) * tb], tb, cw_ref, cb_ref[...]) for b in range(nb)],
        axis=0)
    gates = _dot(xc, wg_ref[...]) + bg_ref[...]
    r = _sigmoid(gates[:, :c])
    i = _sigmoid(gates[:, c:])
    log_a = (-LRU_C) * r * _softplus(-lam_ref[...])
    a = jnp.exp(log_a)
    u = jnp.sqrt(_neg_expm1(2.0 * log_a)) * (i * xc)
    hs = []
    for b in range(nb):
        h = _affine_scan(a[b * tb:(b + 1) * tb], u[b * tb:(b + 1) * tb], hbuf[b])
        hbuf[b] = h[tb - 1:tb, :]
        hs.append(h)
    h = jnp.concatenate(hs, axis=0)
    y_ref[...] = (h * _silu(ga)).reshape(nb, tb, c)

    @pl.when(_last_step())
    def _():
        convo_ref[...] = cbuf[:, CONV_PAD - (CONV_W - 1):CONV_PAD, :]
        ho_ref[...] = hbuf[...]


def _rwkv_kernel(x_ref, g_ref, w_ref, mu_ref, w0_ref, a0_ref, wup_ref, kk_ref, ka_ref, rk_ref,
                 lng_ref, lnb_ref, seg_ref, tri_ref, shift0_ref, s0_ref,
                 y_ref, shifto_ref, so_ref,
                 shbuf, sbuf, lw_s, kk_s, kka_s, kp_s, r_s, v_s, y_s, *, nb, tb, L):
    c = y_ref.shape[-1]
    nc = tb // L
    npair = c // PAIR

    @pl.when(_first_step())
    def _():
        shbuf[...] = shift0_ref[...]
        sbuf[...] = s0_ref[...]

    x = x_ref[...].reshape(nb * tb, x_ref.shape[-1])
    p = _dot(_rms(x, g_ref[...]), w_ref[...])
    nsh = shbuf.shape[-1]
    pb = p[:, :nsh]
    gb = p[:, nsh:]
    xms = []
    for b in range(nb):
        slab = pb[b * tb:(b + 1) * tb]
        sh = _shift_rows(slab, 1, shbuf[b])
        shbuf[b] = slab[tb - 1:tb, :]
        xms.append(slab + (sh - slab) * mu_ref[...])
    xm = jnp.concatenate(xms, axis=0)
    r = xm[:, 0:c]
    k = xm[:, c:2 * c]
    v = xm[:, 2 * c:3 * c]
    lowrank = xm[:, 3 * c:]
    lane = lax.broadcasted_iota(jnp.int32, (1, lowrank.shape[-1]), 1)
    lowrank = jnp.where(lane < lowrank.shape[-1] // 2, jnp.tanh(lowrank), lowrank)
    up = _dot(lowrank, wup_ref[...])
    lw = (-RWKV_DECAY_SCALE) * _sigmoid(w0_ref[...] + up[:, :c])
    a = _sigmoid(a0_ref[...] + up[:, c:])
    kk = k * kk_ref[...]
    kk = kk * lax.rsqrt(_segsum(kk * kk, seg_ref) + 1e-12)
    kp = k * (1.0 + (a - 1.0) * ka_ref[...])
    lw_s[...] = lw
    kk_s[...] = kk
    kka_s[...] = kk * a
    kp_s[...] = kp
    r_s[...] = r
    v_s[...] = v
    bonus = _segsum(r * kp * rk_ref[...], seg_ref) * v

    m0, m1 = _head_masks()
    bd = _bd_mask()
    strict, incl, first = _cat_masks(L)
    tri = tri_ref[...]

    grp = _group_size(nb * nc)
    chains = [(g, j) for g in range(grp) for j in range(npair)]
    lanes = lambda j: slice(j * PAIR, (j + 1) * PAIR)

    def group(i, carry):
        per = []
        for g in range(grp):
            it = i * grp + g
            rows = pl.ds(pl.multiple_of(it * L, L), L)
            lw_c = lw_s[rows, :]
            cl = _dot01(tri, lw_c)
            cl_last = cl[L - 1:L, :]
            e_ip = jnp.exp(-cl)
            e_rel = jnp.exp(cl_last - cl)
            kk_c = kk_s[rows, :]
            kka_c = kka_s[rows, :]
            kp_c = kp_s[rows, :]
            per.append(dict(
                b=it // nc, rows=rows,
                ab=-kk_c * jnp.exp(cl - lw_c), bb=kka_c * e_ip, kb=kp_c * e_ip,
                rb=r_s[rows, :] * jnp.exp(cl), bbl=kka_c * e_rel, kbl=kp_c * e_rel,
                p_last=jnp.exp(cl_last), v=v_s[rows, :]))
        part = lambda key: [per[g][key][:, lanes(j)] for g, j in chains]
        abp, rbp, vp = part("ab"), part("rb"), part("v")
        bblp, kblp = part("bbl"), part("kbl")
        zb = [_stack_heads(t, m0, m1) for t in part("bb")]
        zk = [_stack_heads(t, m0, m1) for t in part("kb")]
        arp = [jnp.concatenate([a, r], axis=0) for a, r in zip(abp, rbp)]
        g1 = [_dot_nt(x_, jnp.concatenate([b_, k_], axis=0))
              for x_, b_, k_ in zip(arp, zb, zk)]
        nn = [_cat_to_blockdiag(jnp.where(strict, t[0:L, 0:2 * L], 0.0), first) for t in g1]
        mk = [_cat_to_blockdiag(jnp.where(strict, t[0:L, 2 * L:4 * L], 0.0), first) for t in g1]
        r_b = [jnp.where(incl, t[L:2 * L, 0:2 * L], 0.0) for t in g1]
        r_k = [jnp.where(incl, t[L:2 * L, 2 * L:4 * L], 0.0) for t in g1]
        tinv = _tri_inv(nn, L)
        vst = [_stack_heads(t, m0, m1) for t in vp]
        kv = [_dot(jnp.concatenate([m_, k_], axis=0), v_) for m_, k_, v_ in zip(mk, r_k, vst)]
        bkl = [jnp.concatenate([b_, k_], axis=0) for b_, k_ in zip(bblp, kblp)]
        for g in range(grp):
            ns = [n for n, (gg, _) in enumerate(chains) if gg == g]
            b, rows, p_last = per[g]["b"], per[g]["rows"], per[g]["p_last"]
            s = [sbuf[b, j] for j in range(npair)]
            ars = [_dot_nt(arp[n], s[j]) for j, n in enumerate(ns)]
            ust = [_dot(tinv[n], _stack_heads(ars[j][0:L], m0, m1) + kv[n][0:2 * L]) for j, n in enumerate(ns)]
            yv = [ars[j][L:2 * L] + _dot(r_b[n], ust[j]) + kv[n][2 * L:3 * L] for j, n in enumerate(ns)]
            su = [_dot_tn(jnp.concatenate([ust[j][0:L] + ust[j][L:2 * L], vp[n]], axis=0), bkl[n])
                  for j, n in enumerate(ns)]
            for j, n in enumerate(ns):
                sbuf[b, j] = jnp.where(bd, s[j] * p_last[:, lanes(j)] + su[j], 0.0)
                y_s[rows, lanes(j)] = yv[j]
        return carry

    lax.fori_loop(0, nb * nc // grp, group, 0)

    y = y_s[...]
    inv = 1.0 / HEAD
    mean = _segsum(y, seg_ref) * inv
    yc = y - mean
    var = _segsum(yc * yc, seg_ref) * inv
    yn = yc * lax.rsqrt(var + RWKV_LN_EPS) * lng_ref[...] + lnb_ref[...]
    y_ref[...] = ((yn + bonus) * _silu(gb)).reshape(nb, tb, c)

    @pl.when(_last_step())
    def _():
        shifto_ref[...] = shbuf[...]
        so_ref[...] = sbuf[...]


def _ssd_kernel(x_ref, g_ref, w_ref, cw_ref, cb_ref, dtb_ref, alog_ref, d_ref, ng_ref, tri_ref,
                pick_ref, conv0_ref, h0_ref,
                y_ref, convo_ref, ho_ref,
                cbuf, hbuf, xs_s, bd_s, cd_s, dt_s, dta_s, y_s, *, nb, tb, L):
    c = y_ref.shape[-1]
    nc = tb // L
    npair = c // PAIR
    nconv = cbuf.shape[-1]

    @pl.when(_first_step())
    def _():
        cbuf[:, CONV_PAD - (CONV_W - 1):CONV_PAD, :] = conv0_ref[...]
        hbuf[...] = h0_ref[...]

    x = x_ref[...].reshape(nb * tb, x_ref.shape[-1])
    p = _dot(_rms(x, g_ref[...]), w_ref[...])
    xbc = p[:, :nconv]
    dtr = p[:, nconv:nconv + c]
    z = p[:, nconv + c:]
    conv = jnp.concatenate(
        [_causal_conv(cbuf, b, xbc[b * tb:(b + 1) * tb], tb, cw_ref, cb_ref[...]) for b in range(nb)],
        axis=0)
    conv = _silu(conv)
    xs = conv[:, :c]
    nbc = (nconv - c) // 2
    dt = _softplus(dtr + dtb_ref[...])
    xs_s[...] = xs
    bd_s[...] = conv[:, c:c + nbc]
    cd_s[...] = conv[:, c + nbc:]
    dt_s[...] = dt
    dta_s[...] = dt * (-jnp.exp(alog_ref[...]))

    m0, m1 = _head_masks()
    bdm = _bd_mask()
    _, incl, _ = _cat_masks(L)
    tri = tri_ref[...]
    pick = pick_ref[...]

    def cat_cols(x):
        if 2 * L == PAIR:
            return x
        return jnp.concatenate([x[:, 0:L], x[:, HEAD:HEAD + L]], axis=1)

    grp = _group_size(nb * nc)
    chains = [(g, j) for g in range(grp) for j in range(npair)]
    lanes = lambda j: slice(j * PAIR, (j + 1) * PAIR)

    def group(i, carry):
        per = []
        for g in range(grp):
            it = i * grp + g
            rows = pl.ds(pl.multiple_of(it * L, L), L)
            acs = _dot01(tri, dta_s[rows, :])
            acs_last = acs[L - 1:L, :]
            dt_c = dt_s[rows, :]
            per.append(dict(b=it // nc, rows=rows, acs=acs, dt=dt_c, xs=xs_s[rows, :], e_acs=jnp.exp(acs),
                            dl=jnp.exp(acs_last - acs) * dt_c, tot=jnp.exp(acs_last),
                            bd=bd_s[rows, :], cd=cd_s[rows, :]))
        part = lambda key: [per[g][key][:, lanes(j)] for g, j in chains]
        group_part = lambda key: [per[g][key][:, lanes(j // 2)] for g, j in chains]
        acs_p, dt_p, xs_p, dl_p = part("acs"), part("dt"), part("xs"), part("dl")
        bd_p, cd_p = group_part("bd"), group_part("cd")
        picked = [_dot01_nt(pick, jnp.concatenate([a, d_], axis=0)) for a, d_ in zip(acs_p, dt_p)]
        cb = [_dot_nt(c_, _stack_heads(b_, m0, m1)) for c_, b_ in zip(cd_p, bd_p)]
        upd = [_dot_tn(x_ * d_, b_) for x_, d_, b_ in zip(xs_p, dl_p, bd_p)]
        scores = []
        for n in range(len(chains)):
            acs_row = jnp.concatenate([picked[n][0:1, 0:L], picked[n][1:2, 0:L]], axis=1)
            dt_row = jnp.concatenate([picked[n][0:1, L:2 * L], picked[n][1:2, L:2 * L]], axis=1)
            decay = jnp.where(incl, jnp.exp(cat_cols(acs_p[n]) - acs_row), 0.0)
            scores.append(cb[n] * decay * dt_row)
        yx = [_dot(s_, _stack_heads(x_, m0, m1)) for s_, x_ in zip(scores, xs_p)]
        for g in range(grp):
            b, rows = per[g]["b"], per[g]["rows"]
            ns = [n for n, (gg, _) in enumerate(chains) if gg == g]
            h = [hbuf[b, j] for j in range(npair)]
            chg = [_dot_nt(cd_p[n], h[j]) for j, n in enumerate(ns)]
            for j, n in enumerate(ns):
                y_s[rows, lanes(j)] = yx[n] + per[g]["e_acs"][:, lanes(j)] * chg[j]
                hbuf[b, j] = jnp.where(bdm, h[j] * per[g]["tot"][:, lanes(j)] + upd[n], 0.0)
        return carry

    lax.fori_loop(0, nb * nc // grp, group, 0)

    yc = y_s[...] + d_ref[...] * xs
    y_ref[...] = _rms(yc * _silu(z), ng_ref[...]).reshape(nb, tb, c)

    @pl.when(_last_step())
    def _():
        convo_ref[...] = cbuf[:, CONV_PAD - (CONV_W - 1):CONV_PAD, :]
        ho_ref[...] = hbuf[...]


def _hgrn_kernel(x_ref, g_ref, w_ref, lbl_ref, ng_ref, tri_ref, lmask_ref, seg_ref, s0_ref,
                 y_ref, so_ref,
                 sbuf, q_s, k_s, v_s, lf_s, y_s, *, nb, tb, L, layer):
    c = y_ref.shape[-1]
    nc = tb // L
    npair = c // PAIR
    nlev = lmask_ref.shape[0]

    @pl.when(_first_step())
    def _():
        sbuf[...] = s0_ref[...]

    x = x_ref[...].reshape(nb * tb, x_ref.shape[-1])
    p = _dot(_rms(x, g_ref[...]), w_ref[...])
    qd = p[:, 0:c]
    fd = p[:, c:2 * c]
    v = p[:, 2 * c:3 * c]
    gd = p[:, 3 * c:]
    logits = lbl_ref[...]
    ex = jnp.exp(logits - jnp.max(logits, axis=0, keepdims=True))
    sm = ex / jnp.sum(ex, axis=0, keepdims=True)
    lb = jnp.zeros_like(sm[0:1])
    for i in range(1, layer + 1):
        lb = lb + sm[i:i + 1]
    log_lb = jnp.log(lb)
    b2 = jnp.log1p(-lb) - _softplus(-fd)
    logf = jnp.maximum(log_lb, b2) + _log1pexp_neg_abs(log_lb - b2)
    q = _silu(qd)
    kx = (1.0 - lb) * _sigmoid(-fd)
    q_s[...] = q
    k_s[...] = kx
    v_s[...] = v
    lf_s[...] = logf
    diag = _segsum(q * kx, seg_ref) * v

    m0, m1 = _head_masks()
    bdm = _bd_mask()
    tri = tri_ref[...]
    level = [lmask_ref[m] > 0.5 for m in range(nlev)]

    grp = _group_size(nb * nc)
    chains = [(g, j) for g in range(grp) for j in range(npair)]
    lanes = lambda j: slice(j * PAIR, (j + 1) * PAIR)

    def group(i, carry):
        per = []
        for g in range(grp):
            it = i * grp + g
            rows = pl.ds(pl.multiple_of(it * L, L), L)
            bc = _dot01(tri, lf_s[rows, :])
            b_last = bc[L - 1:L, :]
            q_c = q_s[rows, :]
            k_c = k_s[rows, :]
            e_lev = [jnp.exp(-jnp.abs(bc - _mid_rows(bc, m + 1))) for m in range(nlev)]
            per.append(dict(b=it // nc, rows=rows, v=v_s[rows, :],
                            qe=q_c * jnp.exp(bc), kl=k_c * jnp.exp(b_last - bc), tot=jnp.exp(b_last),
                            qn=[q_c * e for e in e_lev], kn=[k_c * e for e in e_lev]))
        part = lambda key: [per[g][key][:, lanes(j)] for g, j in chains]
        vp, qe_p, kl_p = part("v"), part("qe"), part("kl")
        att = [jnp.zeros((L, 2 * L), F32) for _ in chains]
        for m in range(nlev):
            lev = [_dot_nt(per[g]["qn"][m][:, lanes(j)], _stack_heads(per[g]["kn"][m][:, lanes(j)], m0, m1))
                   for g, j in chains]
            att = [jnp.where(level[m], t, a) for a, t in zip(att, lev)]
        yv = [_dot(a, _stack_heads(v_, m0, m1)) for a, v_ in zip(att, vp)]
        upd = [_dot_tn(v_, k_) for v_, k_ in zip(vp, kl_p)]
        for g in range(grp):
            b, rows = per[g]["b"], per[g]["rows"]
            ns = [n for n, (gg, _) in enumerate(chains) if gg == g]
            st = [sbuf[b, j] for j in range(npair)]
            ys = [_dot_nt(qe_p[n], st[j]) for j, n in enumerate(ns)]
            for j, n in enumerate(ns):
                y_s[rows, lanes(j)] = ys[j] + yv[n]
                sbuf[b, j] = jnp.where(bdm, st[j] * per[g]["tot"][:, lanes(j)] + upd[n], 0.0)
        return carry

    lax.fori_loop(0, nb * nc // grp, group, 0)

    o = y_s[...] + diag
    ms = _segsum(o * o, seg_ref) * (1.0 / HEAD)
    yd = o * lax.rsqrt(ms + NORM_EPS) * ng_ref[...]
    y_ref[...] = (yd * _silu(gd)).reshape(nb, tb, c)

    @pl.when(_last_step())
    def _():
        so_ref[...] = sbuf[...]


def _post_kernel(ya_ref, yb_ref, yc_ref, yd_ref, x_ref, wo_ref, gpost_ref, gprex_ref, wq_ref,
                 mk_ref, mv_ref, wox_ref, gpostx_ref, o_ref, *, nb, tb, heads):
    d = x_ref.shape[-1]
    c = ya_ref.shape[-1]
    hd = d // heads
    rows = nb * tb
    y = None
    for i, ref in enumerate((ya_ref, yb_ref, yc_ref, yd_ref)):
        t = _dot(ref[...].reshape(rows, c), wo_ref[i * c:(i + 1) * c, :])
        y = t if y is None else y + t
    x1 = x_ref[...].reshape(rows, d) + _rms(y, gpost_ref[...])
    q = _dot(_rms(x1, gprex_ref[...]), wq_ref[...])
    scale = hd ** -0.5
    pairs = [(b, h) for b in range(nb) for h in range(heads)]
    cols = lambda h: slice(h * hd, (h + 1) * hd)
    scores = [_dot_nt(q[b * tb:(b + 1) * tb, cols(h)], mk_ref[b, :, cols(h)]) * scale for b, h in pairs]
    probs = []
    for s in scores:
        e = jnp.exp(s - jnp.max(s, axis=-1, keepdims=True))
        probs.append(e * (1.0 / jnp.sum(e, axis=-1, keepdims=True)))
    outs = [_dot(pr, mv_ref[b, :, cols(h)]) for pr, (b, h) in zip(probs, pairs)]
    o = jnp.concatenate(
        [jnp.concatenate(outs[b * heads:(b + 1) * heads], axis=1) for b in range(nb)], axis=0)
    x2 = x1 + _rms(_dot(o, wox_ref[...]), gpostx_ref[...])
    o_ref[...] = x2.reshape(nb, tb, d)


def _memkv_kernel(m_ref, g_ref, wk_ref, wv_ref, k_ref, v_ref):
    m = _rms(m_ref[0], g_ref[...])
    k_ref[0] = _dot(m, wk_ref[...])
    v_ref[0] = _dot(m, wv_ref[...])


def _full(shape):
    nd = len(shape)
    return pl.BlockSpec(shape, lambda b, t: (0,) * nd)


def _per_batch(shape_tail, nb):
    nd = len(shape_tail)
    return pl.BlockSpec((nb,) + shape_tail, lambda b, t: (b,) + (0,) * nd)


def _state_in(shape_tail, nb, layer):
    nd = len(shape_tail)
    return pl.BlockSpec((None, nb) + shape_tail, lambda b, t: (layer, b) + (0,) * nd)


def _tokens(nb, tb, width):
    return pl.BlockSpec((nb, tb, width), lambda b, t: (b, t, 0))


def _params():
    return pltpu.CompilerParams(dimension_semantics=("arbitrary", "arbitrary"),
                                vmem_limit_bytes=VMEM_LIMIT)


def _call(kern, name, grid, in_arrays, in_specs, out_shapes, out_specs, scratch):
    return pl.pallas_call(
        kern, name=name, grid=grid, in_specs=in_specs, out_specs=out_specs,
        out_shape=out_shapes, scratch_shapes=scratch, compiler_params=_params())(*in_arrays)


def _row(v):
    return v.reshape(1, -1).astype(F32)


def _lru_call(x, lp, conv0, h0, layer, nb, tb):
    B, T, D = x.shape
    c = lp["lru_cw"].shape[-1]
    grid = (B // nb, T // tb)
    ins = [x, lp["g_pre"], lp["w_a"], lp["lru_cw"], lp["lru_cb"], lp["lru_wg"], lp["lru_bg"], lp["lru_lam"],
           conv0, h0]
    specs = [_tokens(nb, tb, D)] + [_full(a.shape) for a in ins[1:8]] + [
        _state_in((CONV_W - 1, c), nb, layer), _state_in((1, c), nb, layer)]
    outs = [jax.ShapeDtypeStruct((B, T, c), F32), jax.ShapeDtypeStruct((B, CONV_W - 1, c), F32),
            jax.ShapeDtypeStruct((B, 1, c), F32)]
    ospecs = [_tokens(nb, tb, c), _per_batch((CONV_W - 1, c), nb), _per_batch((1, c), nb)]
    scratch = [pltpu.VMEM((nb, tb + CONV_PAD, c), F32), pltpu.VMEM((nb, 1, c), F32)]
    return _call(functools.partial(_lru_kernel, nb=nb, tb=tb), "mix_lru", grid, ins, specs, outs, ospecs, scratch)


def _rwkv_call(x, lp, consts, shift0, s0, layer, nb, tb, L):
    B, T, D = x.shape
    c = lp["rw_w0"].shape[-1]
    nsh = shift0.shape[-1]
    npair = c // PAIR
    grid = (B // nb, T // tb)
    ins = [x, lp["g_pre"], lp["w_b"], lp["rw_mu"], lp["rw_w0"], lp["rw_a0"], lp["rw_wup"], lp["rw_kk"],
           lp["rw_ka"], lp["rw_rk"], lp["rw_lng"], lp["rw_lnb"], consts["seg"], consts["tri"], shift0, s0]
    specs = [_tokens(nb, tb, D)] + [_full(a.shape) for a in ins[1:14]] + [
        _state_in((1, nsh), nb, layer), _state_in((npair, PAIR, PAIR), nb, layer)]
    outs = [jax.ShapeDtypeStruct((B, T, c), F32), jax.ShapeDtypeStruct((B, 1, nsh), F32),
            jax.ShapeDtypeStruct((B, npair, PAIR, PAIR), F32)]
    ospecs = [_tokens(nb, tb, c), _per_batch((1, nsh), nb), _per_batch((npair, PAIR, PAIR), nb)]
    rows = nb * tb
    scratch = [pltpu.VMEM((nb, 1, nsh), F32), pltpu.VMEM((nb, npair, PAIR, PAIR), F32)] + [
        pltpu.VMEM((rows, c), F32) for _ in range(7)]
    return _call(functools.partial(_rwkv_kernel, nb=nb, tb=tb, L=L), "mix_rwkv", grid, ins, specs, outs,
                 ospecs, scratch)


def _ssd_call(x, lp, consts, conv0, h0, layer, nb, tb, L):
    B, T, D = x.shape
    c = lp["ssd_dtb"].shape[-1]
    nconv = lp["ssd_cw"].shape[-1]
    nbc = (nconv - c) // 2
    npair = c // PAIR
    grid = (B // nb, T // tb)
    ins = [x, lp["g_pre"], lp["w_c"], lp["ssd_cw"], lp["ssd_cb"], lp["ssd_dtb"], lp["ssd_alog"], lp["ssd_d"],
           lp["ssd_ng"], consts["tri"], consts["pick"], conv0, h0]
    specs = [_tokens(nb, tb, D)] + [_full(a.shape) for a in ins[1:11]] + [
        _state_in((CONV_W - 1, nconv), nb, layer), _state_in((npair, PAIR, PAIR), nb, layer)]
    outs = [jax.ShapeDtypeStruct((B, T, c), F32), jax.ShapeDtypeStruct((B, CONV_W - 1, nconv), F32),
            jax.ShapeDtypeStruct((B, npair, PAIR, PAIR), F32)]
    ospecs = [_tokens(nb, tb, c), _per_batch((CONV_W - 1, nconv), nb), _per_batch((npair, PAIR, PAIR), nb)]
    rows = nb * tb
    scratch = [pltpu.VMEM((nb, tb + CONV_PAD, nconv), F32), pltpu.VMEM((nb, npair, PAIR, PAIR), F32),
               pltpu.VMEM((rows, c), F32), pltpu.VMEM((rows, nbc), F32), pltpu.VMEM((rows, nbc), F32),
               pltpu.VMEM((rows, c), F32), pltpu.VMEM((rows, c), F32), pltpu.VMEM((rows, c), F32)]
    return _call(functools.partial(_ssd_kernel, nb=nb, tb=tb, L=L), "mix_ssd", grid, ins, specs, outs,
                 ospecs, scratch)


def _hgrn_call(x, lp, consts, s0, layer, nb, tb, L):
    B, T, D = x.shape
    c = lp["hg_ng"].shape[-1]
    npair = c // PAIR
    grid = (B // nb, T // tb)
    ins = [x, lp["g_pre"], lp["w_d"], lp["hg_lbl"], lp["hg_ng"], consts["tri"], consts["lmask"], consts["seg"], s0]
    specs = [_tokens(nb, tb, D)] + [_full(a.shape) for a in ins[1:8]] + [_state_in((npair, PAIR, PAIR), nb, layer)]
    outs = [jax.ShapeDtypeStruct((B, T, c), F32), jax.ShapeDtypeStruct((B, npair, PAIR, PAIR), F32)]
    ospecs = [_tokens(nb, tb, c), _per_batch((npair, PAIR, PAIR), nb)]
    rows = nb * tb
    scratch = [pltpu.VMEM((nb, npair, PAIR, PAIR), F32)] + [pltpu.VMEM((rows, c), F32) for _ in range(5)]
    return _call(functools.partial(_hgrn_kernel, nb=nb, tb=tb, L=L, layer=layer), "mix_hgrn", grid, ins, specs,
                 outs, ospecs, scratch)


def _post_call(ys, x, lp, mk, mv, layer, nb, tb, heads):
    B, T, D = x.shape
    c = ys[0].shape[-1]
    M = mk.shape[2]
    grid = (B // nb, T // tb)
    ins = list(ys) + [x, lp["w_out"], lp["g_post"], lp["g_pre_x"], lp["w_q"], mk, mv, lp["w_o"], lp["g_post_x"]]
    kv_spec = pl.BlockSpec((None, nb, M, D), lambda b, t: (layer, b, 0, 0))
    specs = [_tokens(nb, tb, c)] * 4 + [_tokens(nb, tb, D)] + [_full(a.shape) for a in ins[5:9]] + [
        kv_spec, kv_spec] + [_full(a.shape) for a in ins[11:13]]
    return _call(functools.partial(_post_kernel, nb=nb, tb=tb, heads=heads), "post_attn", grid, ins, specs,
                 jax.ShapeDtypeStruct((B, T, D), F32), _tokens(nb, tb, D), [])


def _memkv_call(mem, g, wk, wv):
    B, M, D = mem.shape
    depth = wk.shape[0]
    wspec = pl.BlockSpec((None, D, D), lambda l, b: (l, 0, 0))
    ospec = pl.BlockSpec((None, 1, M, D), lambda l, b: (l, b, 0, 0))
    return pl.pallas_call(
        _memkv_kernel, name="mem_kv", grid=(depth, B),
        in_specs=[pl.BlockSpec((1, M, D), lambda l, b: (b, 0, 0)),
                  pl.BlockSpec((None, 1, D), lambda l, b: (l, 0, 0)), wspec, wspec],
        out_specs=[ospec, ospec],
        out_shape=[jax.ShapeDtypeStruct((depth, B, M, D), F32)] * 2,
        compiler_params=_params())(mem, g, wk, wv)


def _chunk_consts(L, c):
    r = np.arange(L)[:, None]
    j = np.arange(L)[None, :]
    tri = (j <= r).astype(np.float32)
    masks = []
    col = np.arange(2 * L)[None, :]
    s = col % L
    m = 1
    while (1 << m) <= L:
        size, half = 1 << m, 1 << (m - 1)
        masks.append(((r // size == s // size) & (r % size >= half) & (s % size < half)).astype(np.float32))
        m += 1
    seg = (np.arange(c)[:, None] // HEAD == np.arange(c)[None, :] // HEAD).astype(np.float32)
    pick = np.zeros((SUBLANES, PAIR), np.float32)
    pick[0, 0] = 1.0
    pick[1, HEAD] = 1.0
    return {
        "tri": jnp.asarray(tri, BF16),
        "lmask": jnp.asarray(np.stack(masks), F32),
        "seg": jnp.asarray(seg, BF16),
        "pick": jnp.asarray(pick, BF16),
    }


def _to_pairs(s):
    lead, (H, a, b) = s.shape[:-3], s.shape[-3:]
    s = s.reshape(lead + (H // 2, 2, a, b))
    z = jnp.zeros_like(s[..., 0, :, :])
    top = jnp.concatenate([s[..., 0, :, :], z], axis=-1)
    bot = jnp.concatenate([z, s[..., 1, :, :]], axis=-1)
    return jnp.concatenate([top, bot], axis=-2)


def _from_pairs(s):
    lead, P = s.shape[:-3], s.shape[-3]
    both = jnp.stack([s[..., :HEAD, :HEAD], s[..., HEAD:, HEAD:]], axis=-3)
    return both.reshape(lead + (2 * P, HEAD, HEAD))


def _expand_bc(t, c, n):
    xs, bm, cm = t[..., :c], t[..., c:c + 2 * n], t[..., c + 2 * n:]
    dup = lambda u: jnp.concatenate([u[..., :n], u[..., :n], u[..., n:], u[..., n:]], axis=-1)
    return jnp.concatenate([xs, dup(bm), dup(cm)], axis=-1)


def _shrink_bc(t, c, n):
    pick = lambda u: jnp.concatenate([u[..., :n], u[..., 2 * n:3 * n]], axis=-1)
    return jnp.concatenate([t[..., :c], pick(t[..., c:c + 4 * n]), pick(t[..., c + 4 * n:])], axis=-1)


def _blockdiag(w):
    n, d, e = w.shape
    eye = jnp.eye(n, dtype=w.dtype)
    return (eye[:, None, :, None] * w[:, :, None, :]).reshape(n * d, n * e)


def kernel(x_prompt, x_sample, state_lru_conv, state_lru_h, state_rwkv_shift, state_rwkv_wkv, state_ssd_conv, state_ssd_h, state_hgrn_s, cache_mem_k, cache_mem_v, mem_prompt, g_pre, g_post, g_pre_x, g_post_x, w_in, w_out, lru_conv_w, lru_conv_b, lru_w_r, lru_b_r, lru_w_i, lru_b_i, lru_lambda, rwkv_mu, rwkv_w0, rwkv_w_up, rwkv_a0, rwkv_a_up, rwkv_k_k, rwkv_k_a, rwkv_r_k, rwkv_ln_g, rwkv_ln_b, ssd_conv_w, ssd_conv_b, ssd_dt_bias, ssd_a_log, ssd_d, ssd_norm_g, hgrn_lb_logits, hgrn_norm_g, mem_g, mem_w_q, mem_w_k, mem_w_v, mem_w_o):
    depth = w_in.shape[0]
    D = x_prompt.shape[-1]
    c = lru_conv_w.shape[-1]
    nsh = rwkv_mu.shape[-1]
    rank = (nsh - 3 * c) // 2
    ssd_heads = ssd_dt_bias.shape[-1]
    nstate = (ssd_conv_w.shape[-1] - c) // 4
    heads_x = cache_mem_k.shape[-2]
    mem_len = mem_prompt.shape[1]

    o_b = 2 * c
    o_c = o_b + nsh + c
    o_dt = o_c + c + 4 * nstate
    o_z = o_dt + ssd_heads
    o_d = o_z + c

    layers = []
    for l in range(depth):
        w = w_in[l]
        zero = jnp.zeros((rank, c), F32)
        wup = jnp.concatenate([jnp.concatenate([rwkv_w_up[l], zero], axis=1),
                               jnp.concatenate([zero, rwkv_a_up[l]], axis=1)], axis=0)
        rep = lambda v: jnp.repeat(v, c // ssd_heads, axis=-1)
        w_c = jnp.concatenate([_expand_bc(w[:, o_c:o_dt], c, nstate), rep(w[:, o_dt:o_z]), w[:, o_z:o_d]], axis=1)
        layers.append({
            "g_pre": _row(g_pre[l]), "g_post": _row(g_post[l]), "g_pre_x": _row(g_pre_x[l]),
            "g_post_x": _row(g_post_x[l]),
            "w_a": w[:, :o_b].astype(BF16), "w_b": w[:, o_b:o_c].astype(BF16), "w_c": w_c.astype(BF16),
            "w_d": w[:, o_d:].astype(BF16),
            "w_out": w_out[l].astype(BF16), "w_q": mem_w_q[l].astype(BF16), "w_o": mem_w_o[l].astype(BF16),
            "lru_cw": lru_conv_w[l], "lru_cb": _row(lru_conv_b[l]),
            "lru_wg": jnp.concatenate([_blockdiag(lru_w_r[l]), _blockdiag(lru_w_i[l])], axis=1).astype(BF16),
            "lru_bg": _row(jnp.concatenate([lru_b_r[l], lru_b_i[l]])), "lru_lam": _row(lru_lambda[l]),
            "rw_mu": _row(rwkv_mu[l]), "rw_w0": _row(rwkv_w0[l]), "rw_a0": _row(rwkv_a0[l]),
            "rw_wup": wup.astype(BF16), "rw_kk": _row(rwkv_k_k[l]), "rw_ka": _row(rwkv_k_a[l]),
            "rw_rk": _row(rwkv_r_k[l]), "rw_lng": _row(rwkv_ln_g[l]), "rw_lnb": _row(rwkv_ln_b[l]),
            "ssd_cw": _expand_bc(ssd_conv_w[l], c, nstate), "ssd_cb": _row(_expand_bc(ssd_conv_b[l], c, nstate)),
            "ssd_dtb": _row(rep(ssd_dt_bias[l])), "ssd_alog": _row(rep(ssd_a_log[l])), "ssd_d": _row(rep(ssd_d[l])),
            "ssd_ng": _row(ssd_norm_g[l]),
            "hg_lbl": hgrn_lb_logits.astype(F32), "hg_ng": _row(hgrn_norm_g[l]),
        })

    def run(x, mk, mv, conv_a, h_a, shift_b, wkv_b, conv_c, h_c, s_d, nb, tb, L):
        consts = _chunk_consts(L, c)
        B = x.shape[0]
        h_a = h_a.reshape(depth, B, 1, c)
        shift_b = shift_b.reshape(depth, B, 1, nsh)
        wkv_b = _to_pairs(wkv_b)
        conv_c = _expand_bc(conv_c, c, nstate)
        h_c = _to_pairs(h_c)
        s_d = _to_pairs(jnp.swapaxes(s_d, -1, -2))
        acc = [[] for _ in range(7)]
        for l in range(depth):
            lp = layers[l]
            ya, nca, nha = _lru_call(x, lp, conv_a, h_a, l, nb, tb)
            yb, nsb, nwb = _rwkv_call(x, lp, consts, shift_b, wkv_b, l, nb, tb, L)
            yc, ncc, nhc = _ssd_call(x, lp, consts, conv_c, h_c, l, nb, tb, L)
            yd, nsd = _hgrn_call(x, lp, consts, s_d, l, nb, tb, L)
            x = _post_call((ya, yb, yc, yd), x, lp, mk, mv, l, nb, tb, heads_x)
            for lst, val in zip(acc, (nca, nha, nsb, nwb, ncc, nhc, nsd)):
                lst.append(val)
        nca, nha, nsb, nwb, ncc, nhc, nsd = (jnp.stack(v) for v in acc)
        return (x, nca, nha.reshape(depth, B, c), nsb.reshape(depth, B, nsh), _from_pairs(nwb),
                _shrink_bc(ncc, c, nstate), _from_pairs(nhc), jnp.swapaxes(_from_pairs(nsd), -1, -2))

    Bp, Tp = x_prompt.shape[:2]
    Bs, Ts = x_sample.shape[:2]
    k_p, v_p = _memkv_call(mem_prompt, mem_g.reshape(depth, 1, D).astype(F32), mem_w_k.astype(BF16),
                           mem_w_v.astype(BF16))
    kv_shape = (depth, Bp, mem_len, heads_x, D // heads_x)
    mem_k_p = k_p.reshape(kv_shape)
    mem_v_p = v_p.reshape(kv_shape)
    zeros = lambda *s: jnp.zeros((depth, Bp) + s, F32)
    Lp = CHUNK if Tp % CHUNK == 0 else Tp
    Ls = CHUNK if Ts % CHUNK == 0 else Ts
    tb_p = PROMPT_CHUNKS * Lp if Tp % (PROMPT_CHUNKS * Lp) == 0 else Lp
    nb_s = SAMPLE_BATCH if Bs % SAMPLE_BATCH == 0 else 1
    (y_prompt, lru_conv_p, lru_h_p, rwkv_shift_p, rwkv_wkv_p, ssd_conv_p, ssd_h_p, hgrn_s_p) = run(
        x_prompt, k_p, v_p,
        zeros(CONV_W - 1, c), zeros(c), zeros(nsh), zeros(c // HEAD, HEAD, HEAD),
        zeros(CONV_W - 1, c + 4 * nstate), zeros(ssd_heads, HEAD, nstate), zeros(c // HEAD, HEAD, HEAD),
        1, tb_p, Lp)
    (y_sample, lru_conv_s, lru_h_s, rwkv_shift_s, rwkv_wkv_s, ssd_conv_s, ssd_h_s, hgrn_s_s) = run(
        x_sample, cache_mem_k.reshape(depth, Bs, mem_len, D).astype(BF16),
        cache_mem_v.reshape(depth, Bs, mem_len, D).astype(BF16),
        state_lru_conv, state_lru_h, state_rwkv_shift, state_rwkv_wkv,
        state_ssd_conv, state_ssd_h, state_hgrn_s, nb_s, Ts, Ls)
    return (y_prompt, y_sample, lru_conv_p, lru_conv_s, lru_h_p, lru_h_s, rwkv_shift_p, rwkv_shift_s,
            rwkv_wkv_p, rwkv_wkv_s, ssd_conv_p, ssd_conv_s, ssd_h_p, ssd_h_s, hgrn_s_p, hgrn_s_s,
            mem_k_p, mem_v_p)
```

```python
import functools

import numpy as np
import jax
import jax.numpy as jnp
from jax import lax
from jax.experimental import pallas as pl
from jax.experimental.pallas import tpu as pltpu

F32 = jnp.float32
BF16 = jnp.bfloat16

SUBLANES = 8
MXU_TILE = 256
HEAD = 64
PAIR = 2 * HEAD
CHUNK = 64
CONV_W = 4
CONV_PAD = 8
LRU_C = 8.0
RWKV_DECAY_SCALE = 0.6065306597126334
RWKV_LN_EPS = 64e-5
NORM_EPS = 1e-6
VMEM_LIMIT = 56 * 1024 * 1024
PROMPT_CHUNKS = 8
SAMPLE_BATCH = 8


def _dot(a, b):
    return jnp.dot(a.astype(BF16), b.astype(BF16), preferred_element_type=F32)


def _dot_nt(a, b):
    return lax.dot_general(a.astype(BF16), b.astype(BF16), (((1,), (1,)), ((), ())),
                           preferred_element_type=F32)


def _dot_tn(a, b):
    return lax.dot_general(a.astype(BF16), b.astype(BF16), (((0,), (0,)), ((), ())),
                           preferred_element_type=F32)


def _split3(x):
    hi = x.astype(BF16)
    r1 = x - hi.astype(F32)
    mid = r1.astype(BF16)
    lo = (r1 - mid.astype(F32)).astype(BF16)
    return hi, mid, lo


def _group_size(n):
    for g in (8, 4, 2):
        if n % g == 0:
            return g
    return 1


def _dot01(m01, x):
    hi, mid, lo = _split3(x)
    d = lambda p: jnp.dot(m01, p, preferred_element_type=F32)
    return d(hi) + d(mid) + d(lo)


def _dot01_nt(m01, x):
    hi, mid, lo = _split3(x)
    d = lambda p: lax.dot_general(m01, p, (((1,), (1,)), ((), ())), preferred_element_type=F32)
    return d(hi) + d(mid) + d(lo)


def _rms(x, g):
    return x * lax.rsqrt(jnp.mean(x * x, axis=-1, keepdims=True) + NORM_EPS) * g


def _sigmoid(x):
    return jax.nn.sigmoid(x)


def _silu(x):
    return x * jax.nn.sigmoid(x)


def _log1pexp_neg_abs(x):
    return jnp.log(1.0 + jnp.exp(-jnp.abs(x)))


def _softplus(x):
    return jnp.maximum(x, 0.0) + _log1pexp_neg_abs(x)


def _segsum(x, seg_ref):
    w = min(MXU_TILE, x.shape[-1])
    seg = seg_ref[0:w, 0:w]
    return jnp.concatenate([_dot(x[:, s:s + w], seg) for s in range(0, x.shape[-1], w)], axis=1)


def _neg_expm1(z):
    return -jnp.tanh(0.5 * z) * (jnp.exp(z) + 1.0)


def _head_masks():
    lane = lax.broadcasted_iota(jnp.int32, (1, PAIR), 1)
    return lane < HEAD, lane >= HEAD


def _stack_heads(x, m0, m1):
    return jnp.concatenate([jnp.where(m0, x, 0.0), jnp.where(m1, x, 0.0)], axis=0)


def _bd_mask():
    r = lax.broadcasted_iota(jnp.int32, (PAIR, PAIR), 0)
    c = lax.broadcasted_iota(jnp.int32, (PAIR, PAIR), 1)
    return (r >= HEAD) == (c >= HEAD)


def _cat_masks(L):
    t = lax.broadcasted_iota(jnp.int32, (L, 2 * L), 0)
    c = lax.broadcasted_iota(jnp.int32, (L, 2 * L), 1)
    s = jnp.where(c >= L, c - L, c)
    return s < t, s <= t, c < L


def _cat_to_blockdiag(m, first):
    return jnp.concatenate([jnp.where(first, m, 0.0), jnp.where(first, 0.0, m)], axis=0)


def _tri_inv(nns, L):
    n = nns[0].shape[0]
    ri = lax.broadcasted_iota(jnp.int32, (n, n), 0)
    ci = lax.broadcasted_iota(jnp.int32, (n, n), 1)
    tinv = [jnp.where((ri >> 1) == (ci >> 1), t, 0.0) + (ri == ci).astype(F32) for t in nns]
    shift = 1
    while (1 << shift) < L:
        join = ((ri >> (shift + 1)) == (ci >> (shift + 1))) & ((ri >> shift) != (ci >> shift))
        w = [_dot(t, jnp.where(join, x, 0.0)) for t, x in zip(tinv, nns)]
        tinv = [t + _dot(w_, t) for t, w_ in zip(tinv, w)]
        shift += 1
    return tinv


def _mid_rows(b, m):
    n, c = b.shape
    size, half = 1 << m, 1 << (m - 1)
    if size >= 2 * SUBLANES:
        return jnp.concatenate(
            [jnp.broadcast_to(b[s + half - 1:s + half, :], (size, c)) for s in range(0, n, size)], axis=0)
    b3 = b.reshape(n // SUBLANES, SUBLANES, c)
    sub = lax.broadcasted_iota(jnp.int32, b3.shape, 1)
    mids = list(range(half - 1, SUBLANES, size))
    out = jnp.broadcast_to(b3[:, mids[-1]:mids[-1] + 1, :], b3.shape)
    for mid in reversed(mids[:-1]):
        out = jnp.where(sub <= mid + half, b3[:, mid:mid + 1, :], out)
    return out.reshape(n, c)


def _shift_rows(x, d, fill):
    row = lax.broadcasted_iota(jnp.int32, x.shape, 0)
    return jnp.where(row >= d, pltpu.roll(x, d, axis=0), fill)


def _affine_scan(a, u, h_prev):
    n, c = a.shape
    sub = lax.broadcasted_iota(jnp.int32, (n, c), 0) & (SUBLANES - 1)
    d = 1
    while d < SUBLANES:
        keep = sub >= d
        a_s = jnp.where(keep, pltpu.roll(a, d, axis=0), 1.0)
        u_s = jnp.where(keep, pltpu.roll(u, d, axis=0), 0.0)
        u = a * u_s + u
        a = a * a_s
        d *= 2
    nt = n // SUBLANES
    a3 = a.reshape(nt, SUBLANES, c)
    u3 = u.reshape(nt, SUBLANES, c)
    ta = jnp.broadcast_to(a3[:, SUBLANES - 1:SUBLANES, :], a3.shape)
    tu = jnp.broadcast_to(u3[:, SUBLANES - 1:SUBLANES, :], u3.shape)
    carry = jnp.broadcast_to(h_prev, (SUBLANES, c))
    carries = []
    for k in range(nt):
        carries.append(carry)
        carry = ta[k] * carry + tu[k]
    return u + a * jnp.concatenate(carries, axis=0)


def _causal_conv(cbuf, b, xa, tb, w_ref, bias):
    cbuf[b, CONV_PAD:CONV_PAD + tb, :] = xa
    y = bias + xa * w_ref[CONV_W - 1:CONV_W, :]
    for j in range(1, CONV_W):
        y = y + cbuf[b, CONV_PAD - j:CONV_PAD - j + tb, :] * w_ref[CONV_W - 1 - j:CONV_W - j, :]
    hist = cbuf[b, tb + CONV_PAD - (CONV_W - 1):tb + CONV_PAD, :]
    cbuf[b, CONV_PAD - (CONV_W - 1):CONV_PAD, :] = hist
    return y


def _first_step():
    return pl.program_id(1) == 0


def _last_step():
    return pl.program_id(1) == pl.num_programs(1) - 1


def _lru_kernel(x_ref, g_ref, w_ref, cw_ref, cb_ref, wg_ref, bg_ref, lam_ref, conv0_ref, h0_ref,
                y_ref, convo_ref, ho_ref, cbuf, hbuf, *, nb, tb):
    c = y_ref.shape[-1]

    @pl.when(_first_step())
    def _():
        cbuf[:, CONV_PAD - (CONV_W - 1):CONV_PAD, :] = conv0_ref[...]
        hbuf[...] = h0_ref[...]

    x = x_ref[...].reshape(nb * tb, x_ref.shape[-1])
    p = _dot(_rms(x, g_ref[...]), w_ref[...])
    xa = p[:, :c]
    ga = p[:, c:]
    xc = jnp.concatenate(
        [_causal_conv(cbuf, b, xa[b * tb:(b + 1) * tb], tb, cw_ref, cb_ref[...]) for b in range(nb)],
        axis=0)
    gates = _dot(xc, wg_ref[...]) + bg_ref[...]
    r = _sigmoid(gates[:, :c])
    i = _sigmoid(gates[:, c:])
    log_a = (-LRU_C) * r * _softplus(-lam_ref[...])
    a = jnp.exp(log_a)
    u = jnp.sqrt(_neg_expm1(2.0 * log_a)) * (i * xc)
    hs = []
    for b in range(nb):
        h = _affine_scan(a[b * tb:(b + 1) * tb], u[b * tb:(b + 1) * tb], hbuf[b])
        hbuf[b] = h[tb - 1:tb, :]
        hs.append(h)
    h = jnp.concatenate(hs, axis=0)
    y_ref[...] = (h * _silu(ga)).reshape(nb, tb, c)

    @pl.when(_last_step())
    def _():
        convo_ref[...] = cbuf[:, CONV_PAD - (CONV_W - 1):CONV_PAD, :]
        ho_ref[...] = hbuf[...]


def _rwkv_kernel(x_ref, g_ref, w_ref, mu_ref, w0_ref, a0_ref, wup_ref, kk_ref, ka_ref, rk_ref,
                 lng_ref, lnb_ref, seg_ref, tri_ref, shift0_ref, s0_ref,
                 y_ref, shifto_ref, so_ref,
                 shbuf, sbuf, lw_s, kk_s, kka_s, kp_s, r_s, v_s, y_s, *, nb, tb, L):
    c = y_ref.shape[-1]
    nc = tb // L
    npair = c // PAIR

    @pl.when(_first_step())
    def _():
        shbuf[...] = shift0_ref[...]
        sbuf[...] = s0_ref[...]

    x = x_ref[...].reshape(nb * tb, x_ref.shape[-1])
    p = _dot(_rms(x, g_ref[...]), w_ref[...])
    nsh = shbuf.shape[-1]
    pb = p[:, :nsh]
    gb = p[:, nsh:]
    xms = []
    for b in range(nb):
        slab = pb[b * tb:(b + 1) * tb]
        sh = _shift_rows(slab, 1, shbuf[b])
        shbuf[b] = slab[tb - 1:tb, :]
        xms.append(slab + (sh - slab) * mu_ref[...])
    xm = jnp.concatenate(xms, axis=0)
    r = xm[:, 0:c]
    k = xm[:, c:2 * c]
    v = xm[:, 2 * c:3 * c]
    lowrank = xm[:, 3 * c:]
    lane = lax.broadcasted_iota(jnp.int32, (1, lowrank.shape[-1]), 1)
    lowrank = jnp.where(lane < lowrank.shape[-1] // 2, jnp.tanh(lowrank), lowrank)
    up = _dot(lowrank, wup_ref[...])
    lw = (-RWKV_DECAY_SCALE) * _sigmoid(w0_ref[...] + up[:, :c])
    a = _sigmoid(a0_ref[...] + up[:, c:])
    kk = k * kk_ref[...]
    kk = kk * lax.rsqrt(_segsum(kk * kk, seg_ref) + 1e-12)
    kp = k * (1.0 + (a - 1.0) * ka_ref[...])
    lw_s[...] = lw
    kk_s[...] = kk
    kka_s[...] = kk * a
    kp_s[...] = kp
    r_s[...] = r
    v_s[...] = v
    bonus = _segsum(r * kp * rk_ref[...], seg_ref) * v

    m0, m1 = _head_masks()
    bd = _bd_mask()
    strict, incl, first = _cat_masks(L)
    tri = tri_ref[...]

    grp = _group_size(nb * nc)
    chains = [(g, j) for g in range(grp) for j in range(npair)]
    lanes = lambda j: slice(j * PAIR, (j + 1) * PAIR)

    def group(i, carry):
        per = []
        for g in range(grp):
            it = i * grp + g
            rows = pl.ds(pl.multiple_of(it * L, L), L)
            lw_c = lw_s[rows, :]
            cl = _dot01(tri, lw_c)
            cl_last = cl[L - 1:L, :]
            e_ip = jnp.exp(-cl)
            e_rel = jnp.exp(cl_last - cl)
            kk_c = kk_s[rows, :]
            kka_c = kka_s[rows, :]
            kp_c = kp_s[rows, :]
            per.append(dict(
                b=it // nc, rows=rows,
                ab=-kk_c * jnp.exp(cl - lw_c), bb=kka_c * e_ip, kb=kp_c * e_ip,
                rb=r_s[rows, :] * jnp.exp(cl), bbl=kka_c * e_rel, kbl=kp_c * e_rel,
                p_last=jnp.exp(cl_last), v=v_s[rows, :]))
        part = lambda key: [per[g][key][:, lanes(j)] for g, j in chains]
        abp, rbp, vp = part("ab"), part("rb"), part("v")
        bblp, kblp = part("bbl"), part("kbl")
        zb = [_stack_heads(t, m0, m1) for t in part("bb")]
        zk = [_stack_heads(t, m0, m1) for t in part("kb")]
        arp = [jnp.concatenate([a, r], axis=0) for a, r in zip(abp, rbp)]
        g1 = [_dot_nt(x_, jnp.concatenate([b_, k_], axis=0))
              for x_, b_, k_ in zip(arp, zb, zk)]
        nn = [_cat_to_blockdiag(jnp.where(strict, t[0:L, 0:2 * L], 0.0), first) for t in g1]
        mk = [_cat_to_blockdiag(jnp.where(strict, t[0:L, 2 * L:4 * L], 0.0), first) for t in g1]
        r_b = [jnp.where(incl, t[L:2 * L, 0:2 * L], 0.0) for t in g1]
        r_k = [jnp.where(incl, t[L:2 * L, 2 * L:4 * L], 0.0) for t in g1]
        tinv = _tri_inv(nn, L)
        vst = [_stack_heads(t, m0, m1) for t in vp]
        kv = [_dot(jnp.concatenate([m_, k_], axis=0), v_) for m_, k_, v_ in zip(mk, r_k, vst)]
        bkl = [jnp.concatenate([b_, k_], axis=0) for b_, k_ in zip(bblp, kblp)]
        for g in range(grp):
            ns = [n for n, (gg, _) in enumerate(chains) if gg == g]
            b, rows, p_last = per[g]["b"], per[g]["rows"], per[g]["p_last"]
            s = [sbuf[b, j] for j in range(npair)]
            ars = [_dot_nt(arp[n], s[j]) for j, n in enumerate(ns)]
            ust = [_dot(tinv[n], _stack_heads(ars[j][0:L], m0, m1) + kv[n][0:2 * L]) for j, n in enumerate(ns)]
            yv = [ars[j][L:2 * L] + _dot(r_b[n], ust[j]) + kv[n][2 * L:3 * L] for j, n in enumerate(ns)]
            su = [_dot_tn(jnp.concatenate([ust[j][0:L] + ust[j][L:2 * L], vp[n]], axis=0), bkl[n])
                  for j, n in enumerate(ns)]
            for j, n in enumerate(ns):
                sbuf[b, j] = jnp.where(bd, s[j] * p_last[:, lanes(j)] + su[j], 0.0)
                y_s[rows, lanes(j)] = yv[j]
        return carry

    lax.fori_loop(0, nb * nc // grp, group, 0)

    y = y_s[...]
    inv = 1.0 / HEAD
    mean = _segsum(y, seg_ref) * inv
    yc = y - mean
    var = _segsum(yc * yc, seg_ref) * inv
    yn = yc * lax.rsqrt(var + RWKV_LN_EPS) * lng_ref[...] + lnb_ref[...]
    y_ref[...] = ((yn + bonus) * _silu(gb)).reshape(nb, tb, c)

    @pl.when(_last_step())
    def _():
        shifto_ref[...] = shbuf[...]
        so_ref[...] = sbuf[...]


def _ssd_kernel(x_ref, g_ref, w_ref, cw_ref, cb_ref, dtb_ref, alog_ref, d_ref, ng_ref, tri_ref,
                pick_ref, conv0_ref, h0_ref,
                y_ref, convo_ref, ho_ref,
                cbuf, hbuf, xs_s, bd_s, cd_s, dt_s, dta_s, y_s, *, nb, tb, L):
    c = y_ref.shape[-1]
    nc = tb // L
    npair = c // PAIR
    nconv = cbuf.shape[-1]

    @pl.when(_first_step())
    def _():
        cbuf[:, CONV_PAD - (CONV_W - 1):CONV_PAD, :] = conv0_ref[...]
        hbuf[...] = h0_ref[...]

    x = x_ref[...].reshape(nb * tb, x_ref.shape[-1])
    p = _dot(_rms(x, g_ref[...]), w_ref[...])
    xbc = p[:, :nconv]
    dtr = p[:, nconv:nconv + c]
    z = p[:, nconv + c:]
    conv = jnp.concatenate(
        [_causal_conv(cbuf, b, xbc[b * tb:(b + 1) * tb], tb, cw_ref, cb_ref[...]) for b in range(nb)],
        axis=0)
    conv = _silu(conv)
    xs = conv[:, :c]
    nbc = (nconv - c) // 2
    dt = _softplus(dtr + dtb_ref[...])
    xs_s[...] = xs
    bd_s[...] = conv[:, c:c + nbc]
    cd_s[...] = conv[:, c + nbc:]
    dt_s[...] = dt
    dta_s[...] = dt * (-jnp.exp(alog_ref[...]))

    m0, m1 = _head_masks()
    bdm = _bd_mask()
    _, incl, _ = _cat_masks(L)
    tri = tri_ref[...]
    pick = pick_ref[...]

    def cat_cols(x):
        if 2 * L == PAIR:
            return x
        return jnp.concatenate([x[:, 0:L], x[:, HEAD:HEAD + L]], axis=1)

    grp = _group_size(nb * nc)
    chains = [(g, j) for g in range(grp) for j in range(npair)]
    lanes = lambda j: slice(j * PAIR, (j + 1) * PAIR)

    def group(i, carry):
        per = []
        for g in range(grp):
            it = i * grp + g
            rows = pl.ds(pl.multiple_of(it * L, L), L)
            acs = _dot01(tri, dta_s[rows, :])
            acs_last = acs[L - 1:L, :]
            dt_c = dt_s[rows, :]
            per.append(dict(b=it // nc, rows=rows, acs=acs, dt=dt_c, xs=xs_s[rows, :], e_acs=jnp.exp(acs),
                            dl=jnp.exp(acs_last - acs) * dt_c, tot=jnp.exp(acs_last),
                            bd=bd_s[rows, :], cd=cd_s[rows, :]))
        part = lambda key: [per[g][key][:, lanes(j)] for g, j in chains]
        group_part = lambda key: [per[g][key][:, lanes(j // 2)] for g, j in chains]
        acs_p, dt_p, xs_p, dl_p = part("acs"), part("dt"), part("xs"), part("dl")
        bd_p, cd_p = group_part("bd"), group_part("cd")
        picked = [_dot01_nt(pick, jnp.concatenate([a, d_], axis=0)) for a, d_ in zip(acs_p, dt_p)]
        cb = [_dot_nt(c_, _stack_heads(b_, m0, m1)) for c_, b_ in zip(cd_p, bd_p)]
        upd = [_dot_tn(x_ * d_, b_) for x_, d_, b_ in zip(xs_p, dl_p, bd_p)]
        scores = []
        for n in range(len(chains)):
            acs_row = jnp.concatenate([picked[n][0:1, 0:L], picked[n][1:2, 0:L]], axis=1)
            dt_row = jnp.concatenate([picked[n][0:1, L:2 * L], picked[n][1:2, L:2 * L]], axis=1)
            decay = jnp.where(incl, jnp.exp(cat_cols(acs_p[n]) - acs_row), 0.0)
            scores.append(cb[n] * decay * dt_row)
        yx = [_dot(s_, _stack_heads(x_, m0, m1)) for s_, x_ in zip(scores, xs_p)]
        for g in range(grp):
            b, rows = per[g]["b"], per[g]["rows"]
            ns = [n for n, (gg, _) in enumerate(chains) if gg == g]
            h = [hbuf[b, j] for j in range(npair)]
            chg = [_dot_nt(cd_p[n], h[j]) for j, n in enumerate(ns)]
            for j, n in enumerate(ns):
                y_s[rows, lanes(j)] = yx[n] + per[g]["e_acs"][:, lanes(j)] * chg[j]
                hbuf[b, j] = jnp.where(bdm, h[j] * per[g]["tot"][:, lanes(j)] + upd[n], 0.0)
        return carry

    lax.fori_loop(0, nb * nc // grp, group, 0)

    yc = y_s[...] + d_ref[...] * xs
    y_ref[...] = _rms(yc * _silu(z), ng_ref[...]).reshape(nb, tb, c)

    @pl.when(_last_step())
    def _():
        convo_ref[...] = cbuf[:, CONV_PAD - (CONV_W - 1):CONV_PAD, :]
        ho_ref[...] = hbuf[...]


def _hgrn_kernel(x_ref, g_ref, w_ref, lbl_ref, ng_ref, tri_ref, lmask_ref, seg_ref, s0_ref,
                 y_ref, so_ref,
                 sbuf, q_s, k_s, v_s, lf_s, y_s, *, nb, tb, L, layer):
    c = y_ref.shape[-1]
    nc = tb // L
    npair = c // PAIR
    nlev = lmask_ref.shape[0]

    @pl.when(_first_step())
    def _():
        sbuf[...] = s0_ref[...]

    x = x_ref[...].reshape(nb * tb, x_ref.shape[-1])
    p = _dot(_rms(x, g_ref[...]), w_ref[...])
    qd = p[:, 0:c]
    fd = p[:, c:2 * c]
    v = p[:, 2 * c:3 * c]
    gd = p[:, 3 * c:]
    logits = lbl_ref[...]
    ex = jnp.exp(logits - jnp.max(logits, axis=0, keepdims=True))
    sm = ex / jnp.sum(ex, axis=0, keepdims=True)
    lb = jnp.zeros_like(sm[0:1])
    for i in range(1, layer + 1):
        lb = lb + sm[i:i + 1]
    log_lb = jnp.log(lb)
    b2 = jnp.log1p(-lb) - _softplus(-fd)
    logf = jnp.maximum(log_lb, b2) + _log1pexp_neg_abs(log_lb - b2)
    q = _silu(qd)
    kx = (1.0 - lb) * _sigmoid(-fd)
    q_s[...] = q
    k_s[...] = kx
    v_s[...] = v
    lf_s[...] = logf
    diag = _segsum(q * kx, seg_ref) * v

    m0, m1 = _head_masks()
    bdm = _bd_mask()
    tri = tri_ref[...]
    level = [lmask_ref[m] > 0.5 for m in range(nlev)]

    grp = _group_size(nb * nc)
    chains = [(g, j) for g in range(grp) for j in range(npair)]
    lanes = lambda j: slice(j * PAIR, (j + 1) * PAIR)

    def group(i, carry):
        per = []
        for g in range(grp):
            it = i * grp + g
            rows = pl.ds(pl.multiple_of(it * L, L), L)
            bc = _dot01(tri, lf_s[rows, :])
            b_last = bc[L - 1:L, :]
            q_c = q_s[rows, :]
            k_c = k_s[rows, :]
            e_lev = [jnp.exp(-jnp.abs(bc - _mid_rows(bc, m + 1))) for m in range(nlev)]
            per.append(dict(b=it // nc, rows=rows, v=v_s[rows, :],
                            qe=q_c * jnp.exp(bc), kl=k_c * jnp.exp(b_last - bc), tot=jnp.exp(b_last),
                            qn=[q_c * e for e in e_lev], kn=[k_c * e for e in e_lev]))
        part = lambda key: [per[g][key][:, lanes(j)] for g, j in chains]
        vp, qe_p, kl_p = part("v"), part("qe"), part("kl")
        att = [jnp.zeros((L, 2 * L), F32) for _ in chains]
        for m in range(nlev):
            lev = [_dot_nt(per[g]["qn"][m][:, lanes(j)], _stack_heads(per[g]["kn"][m][:, lanes(j)], m0, m1))
                   for g, j in chains]
            att = [jnp.where(level[m], t, a) for a, t in zip(att, lev)]
        yv = [_dot(a, _stack_heads(v_, m0, m1)) for a, v_ in zip(att, vp)]
        upd = [_dot_tn(v_, k_) for v_, k_ in zip(vp, kl_p)]
        for g in range(grp):
            b, rows = per[g]["b"], per[g]["rows"]
            ns = [n for n, (gg, _) in enumerate(chains) if gg == g]
            st = [sbuf[b, j] for j in range(npair)]
            ys = [_dot_nt(qe_p[n], st[j]) for j, n in enumerate(ns)]
            for j, n in enumerate(ns):
                y_s[rows, lanes(j)] = ys[j] + yv[n]
                sbuf[b, j] = jnp.where(bdm, st[j] * per[g]["tot"][:, lanes(j)] + upd[n], 0.0)
        return carry

    lax.fori_loop(0, nb * nc // grp, group, 0)

    o = y_s[...] + diag
    ms = _segsum(o * o, seg_ref) * (1.0 / HEAD)
    yd = o * lax.rsqrt(ms + NORM_EPS) * ng_ref[...]
    y_ref[...] = (yd * _silu(gd)).reshape(nb, tb, c)

    @pl.when(_last_step())
    def _():
        so_ref[...] = sbuf[...]


def _post_kernel(ya_ref, yb_ref, yc_ref, yd_ref, x_ref, wo_ref, gpost_ref, gprex_ref, wq_ref,
                 mk_ref, mv_ref, wox_ref, gpostx_ref, o_ref, *, nb, tb, heads):
    d = x_ref.shape[-1]
    c = ya_ref.shape[-1]
    hd = d // heads
    rows = nb * tb
    y = None
    for i, ref in enumerate((ya_ref, yb_ref, yc_ref, yd_ref)):
        t = _dot(ref[...].reshape(rows, c), wo_ref[i * c:(i + 1) * c, :])
        y = t if y is None else y + t
    x1 = x_ref[...].reshape(rows, d) + _rms(y, gpost_ref[...])
    q = _dot(_rms(x1, gprex_ref[...]), wq_ref[...])
    scale = hd ** -0.5
    pairs = [(b, h) for b in range(nb) for h in range(heads)]
    cols = lambda h: slice(h * hd, (h + 1) * hd)
    scores = [_dot_nt(q[b * tb:(b + 1) * tb, cols(h)], mk_ref[b, :, cols(h)]) * scale for b, h in pairs]
    probs = []
    for s in scores:
        e = jnp.exp(s - jnp.max(s, axis=-1, keepdims=True))
        probs.append(e * (1.0 / jnp.sum(e, axis=-1, keepdims=True)))
    outs = [_dot(pr, mv_ref[b, :, cols(h)]) for pr, (b, h) in zip(probs, pairs)]
    o = jnp.concatenate(
        [jnp.concatenate(outs[b * heads:(b + 1) * heads], axis=1) for b in range(nb)], axis=0)
    x2 = x1 + _rms(_dot(o, wox_ref[...]), gpostx_ref[...])
    o_ref[...] = x2.reshape(nb, tb, d)


def _memkv_kernel(m_ref, g_ref, wk_ref, wv_ref, k_ref, v_ref):
    m = _rms(m_ref[0], g_ref[...])
    k_ref[0] = _dot(m, wk_ref[...])
    v_ref[0] = _dot(m, wv_ref[...])


def _full(shape):
    nd = len(shape)
    return pl.BlockSpec(shape, lambda b, t: (0,) * nd)


def _per_batch(shape_tail, nb):
    nd = len(shape_tail)
    return pl.BlockSpec((nb,) + shape_tail, lambda b, t: (b,) + (0,) * nd)


def _state_in(shape_tail, nb, layer):
    nd = len(shape_tail)
    return pl.BlockSpec((None, nb) + shape_tail, lambda b, t: (layer, b) + (0,) * nd)


def _tokens(nb, tb, width):
    return pl.BlockSpec((nb, tb, width), lambda b, t: (b, t, 0))


def _params():
    return pltpu.CompilerParams(dimension_semantics=("arbitrary", "arbitrary"),
                                vmem_limit_bytes=VMEM_LIMIT)


def _call(kern, name, grid, in_arrays, in_specs, out_shapes, out_specs, scratch):
    return pl.pallas_call(
        kern, name=name, grid=grid, in_specs=in_specs, out_specs=out_specs,
        out_shape=out_shapes, scratch_shapes=scratch, compiler_params=_params())(*in_arrays)


def _row(v):
    return v.reshape(1, -1).astype(F32)


def _lru_call(x, lp, conv0, h0, layer, nb, tb):
    B, T, D = x.shape
    c = lp["lru_cw"].shape[-1]
    grid = (B // nb, T // tb)
    ins = [x, lp["g_pre"], lp["w_a"], lp["lru_cw"], lp["lru_cb"], lp["lru_wg"], lp["lru_bg"], lp["lru_lam"],
           conv0, h0]
    specs = [_tokens(nb, tb, D)] + [_full(a.shape) for a in ins[1:8]] + [
        _state_in((CONV_W - 1, c), nb, layer), _state_in((1, c), nb, layer)]
    outs = [jax.ShapeDtypeStruct((B, T, c), F32), jax.ShapeDtypeStruct((B, CONV_W - 1, c), F32),
            jax.ShapeDtypeStruct((B, 1, c), F32)]
    ospecs = [_tokens(nb, tb, c), _per_batch((CONV_W - 1, c), nb), _per_batch((1, c), nb)]
    scratch = [pltpu.VMEM((nb, tb + CONV_PAD, c), F32), pltpu.VMEM((nb, 1, c), F32)]
    return _call(functools.partial(_lru_kernel, nb=nb, tb=tb), "mix_lru", grid, ins, specs, outs, ospecs, scratch)


def _rwkv_call(x, lp, consts, shift0, s0, layer, nb, tb, L):
    B, T, D = x.shape
    c = lp["rw_w0"].shape[-1]
    nsh = shift0.shape[-1]
    npair = c // PAIR
    grid = (B // nb, T // tb)
    ins = [x, lp["g_pre"], lp["w_b"], lp["rw_mu"], lp["rw_w0"], lp["rw_a0"], lp["rw_wup"], lp["rw_kk"],
           lp["rw_ka"], lp["rw_rk"], lp["rw_lng"], lp["rw_lnb"], consts["seg"], consts["tri"], shift0, s0]
    specs = [_tokens(nb, tb, D)] + [_full(a.shape) for a in ins[1:14]] + [
        _state_in((1, nsh), nb, layer), _state_in((npair, PAIR, PAIR), nb, layer)]
    outs = [jax.ShapeDtypeStruct((B, T, c), F32), jax.ShapeDtypeStruct((B, 1, nsh), F32),
            jax.ShapeDtypeStruct((B, npair, PAIR, PAIR), F32)]
    ospecs = [_tokens(nb, tb, c), _per_batch((1, nsh), nb), _per_batch((npair, PAIR, PAIR), nb)]
    rows = nb * tb
    scratch = [pltpu.VMEM((nb, 1, nsh), F32), pltpu.VMEM((nb, npair, PAIR, PAIR), F32)] + [
        pltpu.VMEM((rows, c), F32) for _ in range(7)]
    return _call(functools.partial(_rwkv_kernel, nb=nb, tb=tb, L=L), "mix_rwkv", grid, ins, specs, outs,
                 ospecs, scratch)


def _ssd_call(x, lp, consts, conv0, h0, layer, nb, tb, L):
    B, T, D = x.shape
    c = lp["ssd_dtb"].shape[-1]
    nconv = lp["ssd_cw"].shape[-1]
    nbc = (nconv - c) // 2
    npair = c // PAIR
    grid = (B // nb, T // tb)
    ins = [x, lp["g_pre"], lp["w_c"], lp["ssd_cw"], lp["ssd_cb"], lp["ssd_dtb"], lp["ssd_alog"], lp["ssd_d"],
           lp["ssd_ng"], consts["tri"], consts["pick"], conv0, h0]
    specs = [_tokens(nb, tb, D)] + [_full(a.shape) for a in ins[1:11]] + [
        _state_in((CONV_W - 1, nconv), nb, layer), _state_in((npair, PAIR, PAIR), nb, layer)]
    outs = [jax.ShapeDtypeStruct((B, T, c), F32), jax.ShapeDtypeStruct((B, CONV_W - 1, nconv), F32),
            jax.ShapeDtypeStruct((B, npair, PAIR, PAIR), F32)]
    ospecs = [_tokens(nb, tb, c), _per_batch((CONV_W - 1, nconv), nb), _per_batch((npair, PAIR, PAIR), nb)]
    rows = nb * tb
    scratch = [pltpu.VMEM((nb, tb + CONV_PAD, nconv), F32), pltpu.VMEM((nb, npair, PAIR, PAIR), F32),
               pltpu.VMEM((rows, c), F32), pltpu.VMEM((rows, nbc), F32), pltpu.VMEM((rows, nbc), F32),
               pltpu.VMEM((rows, c), F32), pltpu.VMEM((rows, c), F32), pltpu.VMEM((rows, c), F32)]
    return _call(functools.partial(_ssd_kernel, nb=nb, tb=tb, L=L), "mix_ssd", grid, ins, specs, outs,
                 ospecs, scratch)


def _hgrn_call(x, lp, consts, s0, layer, nb, tb, L):
    B, T, D = x.shape
    c = lp["hg_ng"].shape[-1]
    npair = c // PAIR
    grid = (B // nb, T // tb)
    ins = [x, lp["g_pre"], lp["w_d"], lp["hg_lbl"], lp["hg_ng"], consts["tri"], consts["lmask"], consts["seg"], s0]
    specs = [_tokens(nb, tb, D)] + [_full(a.shape) for a in ins[1:8]] + [_state_in((npair, PAIR, PAIR), nb, layer)]
    outs = [jax.ShapeDtypeStruct((B, T, c), F32), jax.ShapeDtypeStruct((B, npair, PAIR, PAIR), F32)]
    ospecs = [_tokens(nb, tb, c), _per_batch((npair, PAIR, PAIR), nb)]
    rows = nb * tb
    scratch = [pltpu.VMEM((nb, npair, PAIR, PAIR), F32)] + [pltpu.VMEM((rows, c), F32) for _ in range(5)]
    return _call(functools.partial(_hgrn_kernel, nb=nb, tb=tb, L=L, layer=layer), "mix_hgrn", grid, ins, specs,
                 outs, ospecs, scratch)


def _post_call(ys, x, lp, mk, mv, layer, nb, tb, heads):
    B, T, D = x.shape
    c = ys[0].shape[-1]
    M = mk.shape[2]
    grid = (B // nb, T // tb)
    ins = list(ys) + [x, lp["w_out"], lp["g_post"], lp["g_pre_x"], lp["w_q"], mk, mv, lp["w_o"], lp["g_post_x"]]
    kv_spec = pl.BlockSpec((None, nb, M, D), lambda b, t: (layer, b, 0, 0))
    specs = [_tokens(nb, tb, c)] * 4 + [_tokens(nb, tb, D)] + [_full(a.shape) for a in ins[5:9]] + [
        kv_spec, kv_spec] + [_full(a.shape) for a in ins[11:13]]
    return _call(functools.partial(_post_kernel, nb=nb, tb=tb, heads=heads), "post_attn", grid, ins, specs,
                 jax.ShapeDtypeStruct((B, T, D), F32), _tokens(nb, tb, D), [])


def _memkv_call(mem, g, wk, wv):
    B, M, D = mem.shape
    depth = wk.shape[0]
    wspec = pl.BlockSpec((None, D, D), lambda l, b: (l, 0, 0))
    ospec = pl.BlockSpec((None, 1, M, D), lambda l, b: (l, b, 0, 0))
    return pl.pallas_call(
        _memkv_kernel, name="mem_kv", grid=(depth, B),
        in_specs=[pl.BlockSpec((1, M, D), lambda l, b: (b, 0, 0)),
                  pl.BlockSpec((None, 1, D), lambda l, b: (l, 0, 0)), wspec, wspec],
        out_specs=[ospec, ospec],
        out_shape=[jax.ShapeDtypeStruct((depth, B, M, D), F32)] * 2,
        compiler_params=_params())(mem, g, wk, wv)


def _chunk_consts(L, c):
    r = np.arange(L)[:, None]
    j = np.arange(L)[None, :]
    tri = (j <= r).astype(np.float32)
    masks = []
    col = np.arange(2 * L)[None, :]
    s = col % L
    m = 1
    while (1 << m) <= L:
        size, half = 1 << m, 1 << (m - 1)
        masks.append(((r // size == s // size) & (r % size >= half) & (s % size < half)).astype(np.float32))
        m += 1
    seg = (np.arange(c)[:, None] // HEAD == np.arange(c)[None, :] // HEAD).astype(np.float32)
    pick = np.zeros((SUBLANES, PAIR), np.float32)
    pick[0, 0] = 1.0
    pick[1, HEAD] = 1.0
    return {
        "tri": jnp.asarray(tri, BF16),
        "lmask": jnp.asarray(np.stack(masks), F32),
        "seg": jnp.asarray(seg, BF16),
        "pick": jnp.asarray(pick, BF16),
    }


def _to_pairs(s):
    lead, (H, a, b) = s.shape[:-3], s.shape[-3:]
    s = s.reshape(lead + (H // 2, 2, a, b))
    z = jnp.zeros_like(s[..., 0, :, :])
    top = jnp.concatenate([s[..., 0, :, :], z], axis=-1)
    bot = jnp.concatenate([z, s[..., 1, :, :]], axis=-1)
    return jnp.concatenate([top, bot], axis=-2)


def _from_pairs(s):
    lead, P = s.shape[:-3], s.shape[-3]
    both = jnp.stack([s[..., :HEAD, :HEAD], s[..., HEAD:, HEAD:]], axis=-3)
    return both.reshape(lead + (2 * P, HEAD, HEAD))


def _expand_bc(t, c, n):
    xs, bm, cm = t[..., :c], t[..., c:c + 2 * n], t[..., c + 2 * n:]
    dup = lambda u: jnp.concatenate([u[..., :n], u[..., :n], u[..., n:], u[..., n:]], axis=-1)
    return jnp.concatenate([xs, dup(bm), dup(cm)], axis=-1)


def _shrink_bc(t, c, n):
    pick = lambda u: jnp.concatenate([u[..., :n], u[..., 2 * n:3 * n]], axis=-1)
    return jnp.concatenate([t[..., :c], pick(t[..., c:c + 4 * n]), pick(t[..., c + 4 * n:])], axis=-1)


def _blockdiag(w):
    n, d, e = w.shape
    eye = jnp.eye(n, dtype=w.dtype)
    return (eye[:, None, :, None] * w[:, :, None, :]).reshape(n * d, n * e)


def kernel(x_prompt, x_sample, state_lru_conv, state_lru_h, state_rwkv_shift, state_rwkv_wkv, state_ssd_conv, state_ssd_h, state_hgrn_s, cache_mem_k, cache_mem_v, mem_prompt, g_pre, g_post, g_pre_x, g_post_x, w_in, w_out, lru_conv_w, lru_conv_b, lru_w_r, lru_b_r, lru_w_i, lru_b_i, lru_lambda, rwkv_mu, rwkv_w0, rwkv_w_up, rwkv_a0, rwkv_a_up, rwkv_k_k, rwkv_k_a, rwkv_r_k, rwkv_ln_g, rwkv_ln_b, ssd_conv_w, ssd_conv_b, ssd_dt_bias, ssd_a_log, ssd_d, ssd_norm_g, hgrn_lb_logits, hgrn_norm_g, mem_g, mem_w_q, mem_w_k, mem_w_v, mem_w_o):
    depth = w_in.shape[0]
    D = x_prompt.shape[-1]
    c = lru_conv_w.shape[-1]
    nsh = rwkv_mu.shape[-1]
    rank = (nsh - 3 * c) // 2
    ssd_heads = ssd_dt_bias.shape[-1]
    nstate = (ssd_conv_w.shape[-1] - c) // 4
    heads_x = cache_mem_k.shape[-2]
    mem_len = mem_prompt.shape[1]

    o_b = 2 * c
    o_c = o_b + nsh + c
    o_dt = o_c + c + 4 * nstate
    o_z = o_dt + ssd_heads
    o_d = o_z + c

    layers = []
    for l in range(depth):
        w = w_in[l]
        zero = jnp.zeros((rank, c), F32)
        wup = jnp.concatenate([jnp.concatenate([rwkv_w_up[l], zero], axis=1),
                               jnp.concatenate([zero, rwkv_a_up[l]], axis=1)], axis=0)
        rep = lambda v: jnp.repeat(v, c // ssd_heads, axis=-1)
        w_c = jnp.concatenate([_expand_bc(w[:, o_c:o_dt], c, nstate), rep(w[:, o_dt:o_z]), w[:, o_z:o_d]], axis=1)
        layers.append({
            "g_pre": _row(g_pre[l]), "g_post": _row(g_post[l]), "g_pre_x": _row(g_pre_x[l]),
            "g_post_x": _row(g_post_x[l]),
            "w_a": w[:, :o_b].astype(BF16), "w_b": w[:, o_b:o_c].astype(BF16), "w_c": w_c.astype(BF16),
            "w_d": w[:, o_d:].astype(BF16),
            "w_out": w_out[l].astype(BF16), "w_q": mem_w_q[l].astype(BF16), "w_o": mem_w_o[l].astype(BF16),
            "lru_cw": lru_conv_w[l], "lru_cb": _row(lru_conv_b[l]),
            "lru_wg": jnp.concatenate([_blockdiag(lru_w_r[l]), _blockdiag(lru_w_i[l])], axis=1).astype(BF16),
            "lru_bg": _row(jnp.concatenate([lru_b_r[l], lru_b_i[l]])), "lru_lam": _row(lru_lambda[l]),
            "rw_mu": _row(rwkv_mu[l]), "rw_w0": _row(rwkv_w0[l]), "rw_a0": _row(rwkv_a0[l]),
            "rw_wup": wup.astype(BF16), "rw_kk": _row(rwkv_k_k[l]), "rw_ka": _row(rwkv_k_a[l]),
            "rw_rk": _row(rwkv_r_k[l]), "rw_lng": _row(rwkv_ln_g[l]), "rw_lnb": _row(rwkv_ln_b[l]),
            "ssd_cw": _expand_bc(ssd_conv_w[l], c, nstate), "ssd_cb": _row(_expand_bc(ssd_conv_b[l], c, nstate)),
            "ssd_dtb": _row(rep(ssd_dt_bias[l])), "ssd_alog": _row(rep(ssd_a_log[l])), "ssd_d": _row(rep(ssd_d[l])),
            "ssd_ng": _row(ssd_norm_g[l]),
            "hg_lbl": hgrn_lb_logits.astype(F32), "hg_ng": _row(hgrn_norm_g[l]),
        })

    def run(x, mk, mv, conv_a, h_a, shift_b, wkv_b, conv_c, h_c, s_d, nb, tb, L):
        consts = _chunk_consts(L, c)
        B = x.shape[0]
        h_a = h_a.reshape(depth, B, 1, c)
        shift_b = shift_b.reshape(depth, B, 1, nsh)
        wkv_b = _to_pairs(wkv_b)
        conv_c = _expand_bc(conv_c, c, nstate)
        h_c = _to_pairs(h_c)
        s_d = _to_pairs(jnp.swapaxes(s_d, -1, -2))
        acc = [[] for _ in range(7)]
        for l in range(depth):
            lp = layers[l]
            ya, nca, nha = _lru_call(x, lp, conv_a, h_a, l, nb, tb)
            yb, nsb, nwb = _rwkv_call(x, lp, consts, shift_b, wkv_b, l, nb, tb, L)
            yc, ncc, nhc = _ssd_call(x, lp, consts, conv_c, h_c, l, nb, tb, L)
            yd, nsd = _hgrn_call(x, lp, consts, s_d, l, nb, tb, L)
            x = _post_call((ya, yb, yc, yd), x, lp, mk, mv, l, nb, tb, heads_x)
            for lst, val in zip(acc, (nca, nha, nsb, nwb, ncc, nhc, nsd)):
                lst.append(val)
        nca, nha, nsb, nwb, ncc, nhc, nsd = (jnp.stack(v) for v in acc)
        return (x, nca, nha.reshape(depth, B, c), nsb.reshape(depth, B, nsh), _from_pairs(nwb),
                _shrink_bc(ncc, c, nstate), _from_pairs(nhc), jnp.swapaxes(_from_pairs(nsd), -1, -2))

    Bp, Tp = x_prompt.shape[:2]
    Bs, Ts = x_sample.shape[:2]
    k_p, v_p = _memkv_call(mem_prompt, mem_g.reshape(depth, 1, D).astype(F32), mem_w_k.astype(BF16),
                           mem_w_v.astype(BF16))
    kv_shape = (depth, Bp, mem_len, heads_x, D // heads_x)
    mem_k_p = k_p.reshape(kv_shape)
    mem_v_p = v_p.reshape(kv_shape)
    zeros = lambda *s: jnp.zeros((depth, Bp) + s, F32)
    Lp = CHUNK if Tp % CHUNK == 0 else Tp
    Ls = CHUNK if Ts % CHUNK == 0 else Ts
    tb_p = PROMPT_CHUNKS * Lp if Tp % (PROMPT_CHUNKS * Lp) == 0 else Lp
    nb_s = SAMPLE_BATCH if Bs % SAMPLE_BATCH == 0 else 1
    (y_prompt, lru_conv_p, lru_h_p, rwkv_shift_p, rwkv_wkv_p, ssd_conv_p, ssd_h_p, hgrn_s_p) = run(
        x_prompt, k_p, v_p,
        zeros(CONV_W - 1, c), zeros(c), zeros(nsh), zeros(c // HEAD, HEAD, HEAD),
        zeros(CONV_W - 1, c + 4 * nstate), zeros(ssd_heads, HEAD, nstate), zeros(c // HEAD, HEAD, HEAD),
        1, tb_p, Lp)
    (y_sample, lru_conv_s, lru_h_s, rwkv_shift_s, rwkv_wkv_s, ssd_conv_s, ssd_h_s, hgrn_s_s) = run(
        x_sample, cache_mem_k.astype(BF16).reshape(depth, Bs, mem_len, D),
        cache_mem_v.astype(BF16).reshape(depth, Bs, mem_len, D),
        state_lru_conv, state_lru_h, state_rwkv_shift, state_rwkv_wkv,
        state_ssd_conv, state_ssd_h, state_hgrn_s, nb_s, Ts, Ls)
    return (y_prompt, y_sample, lru_conv_p, lru_conv_s, lru_h_p, lru_h_s, rwkv_shift_p, rwkv_shift_s,
            rwkv_wkv_p, rwkv_wkv_s, ssd_conv_p, ssd_conv_s, ssd_h_p, ssd_h_s, hgrn_s_p, hgrn_s_s,
            mem_k_p, mem_v_p)
```

```python
import functools

import numpy as np
import jax
import jax.numpy as jnp
from jax import lax
from jax.experimental import pallas as pl
from jax.experimental.pallas import tpu as pltpu

F32 = jnp.float32
BF16 = jnp.bfloat16

SUBLANES = 8
MXU_TILE = 256
HEAD = 64
PAIR = 2 * HEAD
CHUNK = 64
CONV_W = 4
CONV_PAD = 8
LRU_C = 8.0
RWKV_DECAY_SCALE = 0.6065306597126334
RWKV_LN_EPS = 64e-5
NORM_EPS = 1e-6
VMEM_LIMIT = 56 * 1024 * 1024
PROMPT_CHUNKS = 8
SAMPLE_BATCH = 8


def _dot(a, b):
    return jnp.dot(a.astype(BF16), b.astype(BF16), preferred_element_type=F32)


def _dot_nt(a, b):
    return lax.dot_general(a.astype(BF16), b.astype(BF16), (((1,), (1,)), ((), ())),
                           preferred_element_type=F32)


def _dot_tn(a, b):
    return lax.dot_general(a.astype(BF16), b.astype(BF16), (((0,), (0,)), ((), ())),
                           preferred_element_type=F32)


def _split3(x):
    hi = x.astype(BF16)
    r1 = x - hi.astype(F32)
    mid = r1.astype(BF16)
    lo = (r1 - mid.astype(F32)).astype(BF16)
    return hi, mid, lo


def _group_size(n):
    for g in (8, 4, 2):
        if n % g == 0:
            return g
    return 1


def _dot01(m01, x):
    hi, mid, lo = _split3(x)
    d = lambda p: jnp.dot(m01, p, preferred_element_type=F32)
    return d(hi) + d(mid) + d(lo)


def _dot01_nt(m01, x):
    hi, mid, lo = _split3(x)
    d = lambda p: lax.dot_general(m01, p, (((1,), (1,)), ((), ())), preferred_element_type=F32)
    return d(hi) + d(mid) + d(lo)


def _rms(x, g):
    return x * lax.rsqrt(jnp.mean(x * x, axis=-1, keepdims=True) + NORM_EPS) * g


def _sigmoid(x):
    return jax.nn.sigmoid(x)


def _silu(x):
    return x * jax.nn.sigmoid(x)


def _log1pexp_neg_abs(x):
    return jnp.log(1.0 + jnp.exp(-jnp.abs(x)))


def _softplus(x):
    return jnp.maximum(x, 0.0) + _log1pexp_neg_abs(x)


def _segsum(x, seg_ref):
    w = min(MXU_TILE, x.shape[-1])
    seg = seg_ref[0:w, 0:w]
    return jnp.concatenate([_dot(x[:, s:s + w], seg) for s in range(0, x.shape[-1], w)], axis=1)


def _neg_expm1(z):
    return -jnp.tanh(0.5 * z) * (jnp.exp(z) + 1.0)


def _head_masks():
    lane = lax.broadcasted_iota(jnp.int32, (1, PAIR), 1)
    return lane < HEAD, lane >= HEAD


def _stack_heads(x, m0, m1):
    return jnp.concatenate([jnp.where(m0, x, 0.0), jnp.where(m1, x, 0.0)], axis=0)


def _bd_mask():
    r = lax.broadcasted_iota(jnp.int32, (PAIR, PAIR), 0)
    c = lax.broadcasted_iota(jnp.int32, (PAIR, PAIR), 1)
    return (r >= HEAD) == (c >= HEAD)


def _cat_masks(L):
    t = lax.broadcasted_iota(jnp.int32, (L, 2 * L), 0)
    c = lax.broadcasted_iota(jnp.int32, (L, 2 * L), 1)
    s = jnp.where(c >= L, c - L, c)
    return s < t, s <= t, c < L


def _cat_to_blockdiag(m, first):
    return jnp.concatenate([jnp.where(first, m, 0.0), jnp.where(first, 0.0, m)], axis=0)


def _tri_inv_stages(nns, L):
    n = nns[0].shape[0]
    ri = lax.broadcasted_iota(jnp.int32, (n, n), 0)
    ci = lax.broadcasted_iota(jnp.int32, (n, n), 1)
    tinv = [jnp.where((ri >> 1) == (ci >> 1), t, 0.0) + (ri == ci).astype(F32) for t in nns]
    shift = 1
    while (1 << shift) < L:
        join = ((ri >> (shift + 1)) == (ci >> (shift + 1))) & ((ri >> shift) != (ci >> shift))
        w = [_dot(t, jnp.where(join, x, 0.0)) for t, x in zip(tinv, nns)]
        yield None
        tinv = [t + _dot(w_, t) for t, w_ in zip(tinv, w)]
        shift += 1
        yield None
    yield tinv


def _run_stages(*gens):
    last = [None] * len(gens)
    live = list(range(len(gens)))
    while live:
        for k in list(live):
            try:
                out = next(gens[k])
            except StopIteration:
                live.remove(k)
                continue
            if out is not None:
                last[k] = out
    return last


def _mid_rows(b, m):
    n, c = b.shape
    size, half = 1 << m, 1 << (m - 1)
    if size >= 2 * SUBLANES:
        return jnp.concatenate(
            [jnp.broadcast_to(b[s + half - 1:s + half, :], (size, c)) for s in range(0, n, size)], axis=0)
    b3 = b.reshape(n // SUBLANES, SUBLANES, c)
    sub = lax.broadcasted_iota(jnp.int32, b3.shape, 1)
    mids = list(range(half - 1, SUBLANES, size))
    out = jnp.broadcast_to(b3[:, mids[-1]:mids[-1] + 1, :], b3.shape)
    for mid in reversed(mids[:-1]):
        out = jnp.where(sub <= mid + half, b3[:, mid:mid + 1, :], out)
    return out.reshape(n, c)


def _shift_rows(x, d, fill):
    row = lax.broadcasted_iota(jnp.int32, x.shape, 0)
    return jnp.where(row >= d, pltpu.roll(x, d, axis=0), fill)


def _affine_scan(a, u, h_prev):
    n, c = a.shape
    sub = lax.broadcasted_iota(jnp.int32, (n, c), 0) & (SUBLANES - 1)
    d = 1
    while d < SUBLANES:
        keep = sub >= d
        a_s = jnp.where(keep, pltpu.roll(a, d, axis=0), 1.0)
        u_s = jnp.where(keep, pltpu.roll(u, d, axis=0), 0.0)
        u = a * u_s + u
        a = a * a_s
        d *= 2
    nt = n // SUBLANES
    a3 = a.reshape(nt, SUBLANES, c)
    u3 = u.reshape(nt, SUBLANES, c)
    ta = jnp.broadcast_to(a3[:, SUBLANES - 1:SUBLANES, :], a3.shape)
    tu = jnp.broadcast_to(u3[:, SUBLANES - 1:SUBLANES, :], u3.shape)
    carry = jnp.broadcast_to(h_prev, (SUBLANES, c))
    carries = []
    for k in range(nt):
        carries.append(carry)
        carry = ta[k] * carry + tu[k]
    return u + a * jnp.concatenate(carries, axis=0)


def _causal_conv(cbuf, b, xa, tb, w_ref, bias):
    cbuf[b, CONV_PAD:CONV_PAD + tb, :] = xa
    y = bias + xa * w_ref[CONV_W - 1:CONV_W, :]
    for j in range(1, CONV_W):
        y = y + cbuf[b, CONV_PAD - j:CONV_PAD - j + tb, :] * w_ref[CONV_W - 1 - j:CONV_W - j, :]
    hist = cbuf[b, tb + CONV_PAD - (CONV_W - 1):tb + CONV_PAD, :]
    cbuf[b, CONV_PAD - (CONV_W - 1):CONV_PAD, :] = hist
    return y


def _first_step():
    return pl.program_id(1) == 0


def _last_step():
    return pl.program_id(1) == pl.num_programs(1) - 1


def _lru_kernel(x_ref, g_ref, w_ref, cw_ref, cb_ref, wg_ref, bg_ref, lam_ref, conv0_ref, h0_ref,
                y_ref, convo_ref, ho_ref, cbuf, hbuf, *, nb, tb):
    c = y_ref.shape[-1]

    @pl.when(_first_step())
    def _():
        cbuf[:, CONV_PAD - (CONV_W - 1):CONV_PAD, :] = conv0_ref[...]
        hbuf[...] = h0_ref[...]

    x = x_ref[...].reshape(nb * tb, x_ref.shape[-1])
    p = _dot(_rms(x, g_ref[...]), w_ref[...])
    xa = p[:, :c]
    ga = p[:, c:]
    xc = jnp.concatenate(
        [_causal_conv(cbuf, b, xa[b * tb:(b + 1) * tb], tb, cw_ref, cb_ref[...]) for b in range(nb)],
        axis=0)
    gates = _dot(xc, wg_ref[...]) + bg_ref[...]
    r = _sigmoid(gates[:, :c])
    i = _sigmoid(gates[:, c:])
    log_a = (-LRU_C) * r * _softplus(-lam_ref[...])
    a = jnp.exp(log_a)
    u = jnp.sqrt(_neg_expm1(2.0 * log_a)) * (i * xc)
    hs = []
    for b in range(nb):
        h = _affine_scan(a[b * tb:(b + 1) * tb], u[b * tb:(b + 1) * tb], hbuf[b])
        hbuf[b] = h[tb - 1:tb, :]
        hs.append(h)
    h = jnp.concatenate(hs, axis=0)
    y_ref[...] = (h * _silu(ga)).reshape(nb, tb, c)

    @pl.when(_last_step())
    def _():
        convo_ref[...] = cbuf[:, CONV_PAD - (CONV_W - 1):CONV_PAD, :]
        ho_ref[...] = hbuf[...]


def _rwkv_kernel(x_ref, g_ref, w_ref, mu_ref, w0_ref, a0_ref, wup_ref, kk_ref, ka_ref, rk_ref,
                 lng_ref, lnb_ref, seg_ref, tri_ref, shift0_ref, s0_ref,
                 y_ref, shifto_ref, so_ref,
                 shbuf, sbuf, lw_s, kk_s, kka_s, kp_s, r_s, v_s, y_s, *, nb, tb, L):
    c = y_ref.shape[-1]
    nc = tb // L
    npair = c // PAIR

    @pl.when(_first_step())
    def _():
        shbuf[...] = shift0_ref[...]
        sbuf[...] = s0_ref[...]

    x = x_ref[...].reshape(nb * tb, x_ref.shape[-1])
    p = _dot(_rms(x, g_ref[...]), w_ref[...])
    nsh = shbuf.shape[-1]
    pb = p[:, :nsh]
    gb = p[:, nsh:]
    xms = []
    for b in range(nb):
        slab = pb[b * tb:(b + 1) * tb]
        sh = _shift_rows(slab, 1, shbuf[b])
        shbuf[b] = slab[tb - 1:tb, :]
        xms.append(slab + (sh - slab) * mu_ref[...])
    xm = jnp.concatenate(xms, axis=0)
    r = xm[:, 0:c]
    k = xm[:, c:2 * c]
    v = xm[:, 2 * c:3 * c]
    lowrank = xm[:, 3 * c:]
    lane = lax.broadcasted_iota(jnp.int32, (1, lowrank.shape[-1]), 1)
    lowrank = jnp.where(lane < lowrank.shape[-1] // 2, jnp.tanh(lowrank), lowrank)
    up = _dot(lowrank, wup_ref[...])
    lw = (-RWKV_DECAY_SCALE) * _sigmoid(w0_ref[...] + up[:, :c])
    a = _sigmoid(a0_ref[...] + up[:, c:])
    kk = k * kk_ref[...]
    kk = kk * lax.rsqrt(_segsum(kk * kk, seg_ref) + 1e-12)
    kp = k * (1.0 + (a - 1.0) * ka_ref[...])
    lw_s[...] = lw
    kk_s[...] = kk
    kka_s[...] = kk * a
    kp_s[...] = kp
    r_s[...] = r
    v_s[...] = v
    bonus = _segsum(r * kp * rk_ref[...], seg_ref) * v

    m0, m1 = _head_masks()
    bd = _bd_mask()
    strict, incl, first = _cat_masks(L)
    tri = tri_ref[...]

    grp = _group_size(nb * nc)
    lanes = lambda j: slice(j * PAIR, (j + 1) * PAIR)

    def prepare(i, gs):
        per = {}
        for g in gs:
            it = i * grp + g
            rows = pl.ds(pl.multiple_of(it * L, L), L)
            lw_c = lw_s[rows, :]
            cl = _dot01(tri, lw_c)
            cl_last = cl[L - 1:L, :]
            e_ip = jnp.exp(-cl)
            e_rel = jnp.exp(cl_last - cl)
            kk_c = kk_s[rows, :]
            kka_c = kka_s[rows, :]
            kp_c = kp_s[rows, :]
            per[g] = dict(
                b=it // nc, rows=rows,
                ab=-kk_c * jnp.exp(cl - lw_c), bb=kka_c * e_ip, kb=kp_c * e_ip,
                rb=r_s[rows, :] * jnp.exp(cl), bbl=kka_c * e_rel, kbl=kp_c * e_rel,
                p_last=jnp.exp(cl_last), v=v_s[rows, :])
        chains = [(g, j) for g in gs for j in range(npair)]
        part = lambda key: [per[g][key][:, lanes(j)] for g, j in chains]
        abp, rbp, vp = part("ab"), part("rb"), part("v")
        bblp, kblp = part("bbl"), part("kbl")
        zb = [_stack_heads(t, m0, m1) for t in part("bb")]
        zk = [_stack_heads(t, m0, m1) for t in part("kb")]
        arp = [jnp.concatenate([a, r], axis=0) for a, r in zip(abp, rbp)]
        g1 = [_dot_nt(x_, jnp.concatenate([b_, k_], axis=0))
              for x_, b_, k_ in zip(arp, zb, zk)]
        nn = [_cat_to_blockdiag(jnp.where(strict, t[0:L, 0:2 * L], 0.0), first) for t in g1]
        mk = [_cat_to_blockdiag(jnp.where(strict, t[0:L, 2 * L:4 * L], 0.0), first) for t in g1]
        r_b = [jnp.where(incl, t[L:2 * L, 0:2 * L], 0.0) for t in g1]
        r_k = [jnp.where(incl, t[L:2 * L, 2 * L:4 * L], 0.0) for t in g1]
        vst = [_stack_heads(t, m0, m1) for t in vp]
        kv = [_dot(jnp.concatenate([m_, k_], axis=0), v_) for m_, k_, v_ in zip(mk, r_k, vst)]
        bkl = [jnp.concatenate([b_, k_], axis=0) for b_, k_ in zip(bblp, kblp)]
        yield None
        tinv = None
        for out in _tri_inv_stages(nn, L):
            if out is None:
                yield None
            else:
                tinv = out
        yield dict(per=per, chains=chains, arp=arp, r_b=r_b, kv=kv, bkl=bkl, vp=vp, tinv=tinv)

    def sequential(st, gs):
        for g in gs:
            ns = [n for n, (gg, _) in enumerate(st["chains"]) if gg == g]
            meta = st["per"][g]
            b, rows, p_last = meta["b"], meta["rows"], meta["p_last"]
            s = [sbuf[b, j] for j in range(npair)]
            ars = [_dot_nt(st["arp"][n], s[j]) for j, n in enumerate(ns)]
            yield None
            ust = [_dot(st["tinv"][n], _stack_heads(ars[j][0:L], m0, m1) + st["kv"][n][0:2 * L])
                   for j, n in enumerate(ns)]
            yield None
            yv = [ars[j][L:2 * L] + _dot(st["r_b"][n], ust[j]) + st["kv"][n][2 * L:3 * L]
                  for j, n in enumerate(ns)]
            su = [_dot_tn(jnp.concatenate([ust[j][0:L] + ust[j][L:2 * L], st["vp"][n]], axis=0), st["bkl"][n])
                  for j, n in enumerate(ns)]
            for j, n in enumerate(ns):
                sbuf[b, j] = jnp.where(bd, s[j] * p_last[:, lanes(j)] + su[j], 0.0)
                y_s[rows, lanes(j)] = yv[j]
            yield None

    def group(i, carry):
        early, late = list(range(grp // 2)), list(range(grp // 2, grp))
        if not early:
            st, = _run_stages(prepare(i, late))
            _run_stages(sequential(st, late))
            return carry
        st_early, = _run_stages(prepare(i, early))
        st_late, _ = _run_stages(prepare(i, late), sequential(st_early, early))
        _run_stages(sequential(st_late, late))
        return carry

    lax.fori_loop(0, nb * nc // grp, group, 0)

    y = y_s[...]
    inv = 1.0 / HEAD
    mean = _segsum(y, seg_ref) * inv
    yc = y - mean
    var = _segsum(yc * yc, seg_ref) * inv
    yn = yc * lax.rsqrt(var + RWKV_LN_EPS) * lng_ref[...] + lnb_ref[...]
    y_ref[...] = ((yn + bonus) * _silu(gb)).reshape(nb, tb, c)

    @pl.when(_last_step())
    def _():
        shifto_ref[...] = shbuf[...]
        so_ref[...] = sbuf[...]


def _ssd_kernel(x_ref, g_ref, w_ref, cw_ref, cb_ref, dtb_ref, alog_ref, d_ref, ng_ref, tri_ref,
                pick_ref, conv0_ref, h0_ref,
                y_ref, convo_ref, ho_ref,
                cbuf, hbuf, xs_s, bd_s, cd_s, dt_s, dta_s, y_s, *, nb, tb, L):
    c = y_ref.shape[-1]
    nc = tb // L
    npair = c // PAIR
    nconv = cbuf.shape[-1]

    @pl.when(_first_step())
    def _():
        cbuf[:, CONV_PAD - (CONV_W - 1):CONV_PAD, :] = conv0_ref[...]
        hbuf[...] = h0_ref[...]

    x = x_ref[...].reshape(nb * tb, x_ref.shape[-1])
    p = _dot(_rms(x, g_ref[...]), w_ref[...])
    xbc = p[:, :nconv]
    dtr = p[:, nconv:nconv + c]
    z = p[:, nconv + c:]
    conv = jnp.concatenate(
        [_causal_conv(cbuf, b, xbc[b * tb:(b + 1) * tb], tb, cw_ref, cb_ref[...]) for b in range(nb)],
        axis=0)
    conv = _silu(conv)
    xs = conv[:, :c]
    nbc = (nconv - c) // 2
    dt = _softplus(dtr + dtb_ref[...])
    xs_s[...] = xs
    bd_s[...] = conv[:, c:c + nbc]
    cd_s[...] = conv[:, c + nbc:]
    dt_s[...] = dt
    dta_s[...] = dt * (-jnp.exp(alog_ref[...]))

    m0, m1 = _head_masks()
    bdm = _bd_mask()
    _, incl, _ = _cat_masks(L)
    tri = tri_ref[...]
    pick = pick_ref[...]

    def cat_cols(x):
        if 2 * L == PAIR:
            return x
        return jnp.concatenate([x[:, 0:L], x[:, HEAD:HEAD + L]], axis=1)

    grp = _group_size(nb * nc)
    chains = [(g, j) for g in range(grp) for j in range(npair)]
    lanes = lambda j: slice(j * PAIR, (j + 1) * PAIR)

    def group(i, carry):
        per = []
        for g in range(grp):
            it = i * grp + g
            rows = pl.ds(pl.multiple_of(it * L, L), L)
            acs = _dot01(tri, dta_s[rows, :])
            acs_last = acs[L - 1:L, :]
            dt_c = dt_s[rows, :]
            per.append(dict(b=it // nc, rows=rows, acs=acs, dt=dt_c, xs=xs_s[rows, :], e_acs=jnp.exp(acs),
                            dl=jnp.exp(acs_last - acs) * dt_c, tot=jnp.exp(acs_last),
                            bd=bd_s[rows, :], cd=cd_s[rows, :]))
        part = lambda key: [per[g][key][:, lanes(j)] for g, j in chains]
        group_part = lambda key: [per[g][key][:, lanes(j // 2)] for g, j in chains]
        acs_p, dt_p, xs_p, dl_p = part("acs"), part("dt"), part("xs"), part("dl")
        bd_p, cd_p = group_part("bd"), group_part("cd")
        picked = [_dot01_nt(pick, jnp.concatenate([a, d_], axis=0)) for a, d_ in zip(acs_p, dt_p)]
        cb = [_dot_nt(c_, _stack_heads(b_, m0, m1)) for c_, b_ in zip(cd_p, bd_p)]
        upd = [_dot_tn(x_ * d_, b_) for x_, d_, b_ in zip(xs_p, dl_p, bd_p)]
        scores = []
        for n in range(len(chains)):
            acs_row = jnp.concatenate([picked[n][0:1, 0:L], picked[n][1:2, 0:L]], axis=1)
            dt_row = jnp.concatenate([picked[n][0:1, L:2 * L], picked[n][1:2, L:2 * L]], axis=1)
            decay = jnp.where(incl, jnp.exp(cat_cols(acs_p[n]) - acs_row), 0.0)
            scores.append(cb[n] * decay * dt_row)
        yx = [_dot(s_, _stack_heads(x_, m0, m1)) for s_, x_ in zip(scores, xs_p)]
        for g in range(grp):
            b, rows = per[g]["b"], per[g]["rows"]
            ns = [n for n, (gg, _) in enumerate(chains) if gg == g]
            h = [hbuf[b, j] for j in range(npair)]
            chg = [_dot_nt(cd_p[n], h[j]) for j, n in enumerate(ns)]
            for j, n in enumerate(ns):
                y_s[rows, lanes(j)] = yx[n] + per[g]["e_acs"][:, lanes(j)] * chg[j]
                hbuf[b, j] = jnp.where(bdm, h[j] * per[g]["tot"][:, lanes(j)] + upd[n], 0.0)
        return carry

    lax.fori_loop(0, nb * nc // grp, group, 0)

    yc = y_s[...] + d_ref[...] * xs
    y_ref[...] = _rms(yc * _silu(z), ng_ref[...]).reshape(nb, tb, c)

    @pl.when(_last_step())
    def _():
        convo_ref[...] = cbuf[:, CONV_PAD - (CONV_W - 1):CONV_PAD, :]
        ho_ref[...] = hbuf[...]


def _hgrn_kernel(x_ref, g_ref, w_ref, lbl_ref, ng_ref, tri_ref, lmask_ref, seg_ref, s0_ref,
                 y_ref, so_ref,
                 sbuf, q_s, k_s, v_s, lf_s, y_s, *, nb, tb, L, layer):
    c = y_ref.shape[-1]
    nc = tb // L
    npair = c // PAIR
    nlev = lmask_ref.shape[0]

    @pl.when(_first_step())
    def _():
        sbuf[...] = s0_ref[...]

    x = x_ref[...].reshape(nb * tb, x_ref.shape[-1])
    p = _dot(_rms(x, g_ref[...]), w_ref[...])
    qd = p[:, 0:c]
    fd = p[:, c:2 * c]
    v = p[:, 2 * c:3 * c]
    gd = p[:, 3 * c:]
    logits = lbl_ref[...]
    ex = jnp.exp(logits - jnp.max(logits, axis=0, keepdims=True))
    sm = ex / jnp.sum(ex, axis=0, keepdims=True)
    lb = jnp.zeros_like(sm[0:1])
    for i in range(1, layer + 1):
        lb = lb + sm[i:i + 1]
    log_lb = jnp.log(lb)
    b2 = jnp.log1p(-lb) - _softplus(-fd)
    logf = jnp.maximum(log_lb, b2) + _log1pexp_neg_abs(log_lb - b2)
    q = _silu(qd)
    kx = (1.0 - lb) * _sigmoid(-fd)
    q_s[...] = q
    k_s[...] = kx
    v_s[...] = v
    lf_s[...] = logf
    diag = _segsum(q * kx, seg_ref) * v

    m0, m1 = _head_masks()
    bdm = _bd_mask()
    tri = tri_ref[...]
    level = [lmask_ref[m] > 0.5 for m in range(nlev)]

    grp = _group_size(nb * nc)
    chains = [(g, j) for g in range(grp) for j in range(npair)]
    lanes = lambda j: slice(j * PAIR, (j + 1) * PAIR)

    def group(i, carry):
        per = []
        for g in range(grp):
            it = i * grp + g
            rows = pl.ds(pl.multiple_of(it * L, L), L)
            bc = _dot01(tri, lf_s[rows, :])
            b_last = bc[L - 1:L, :]
            q_c = q_s[rows, :]
            k_c = k_s[rows, :]
            e_lev = [jnp.exp(-jnp.abs(bc - _mid_rows(bc, m + 1))) for m in range(nlev)]
            per.append(dict(b=it // nc, rows=rows, v=v_s[rows, :],
                            qe=q_c * jnp.exp(bc), kl=k_c * jnp.exp(b_last - bc), tot=jnp.exp(b_last),
                            qn=[q_c * e for e in e_lev], kn=[k_c * e for e in e_lev]))
        part = lambda key: [per[g][key][:, lanes(j)] for g, j in chains]
        vp, qe_p, kl_p = part("v"), part("qe"), part("kl")
        att = [jnp.zeros((L, 2 * L), F32) for _ in chains]
        for m in range(nlev):
            lev = [_dot_nt(per[g]["qn"][m][:, lanes(j)], _stack_heads(per[g]["kn"][m][:, lanes(j)], m0, m1))
                   for g, j in chains]
            att = [jnp.where(level[m], t, a) for a, t in zip(att, lev)]
        yv = [_dot(a, _stack_heads(v_, m0, m1)) for a, v_ in zip(att, vp)]
        upd = [_dot_tn(v_, k_) for v_, k_ in zip(vp, kl_p)]
        for g in range(grp):
            b, rows = per[g]["b"], per[g]["rows"]
            ns = [n for n, (gg, _) in enumerate(chains) if gg == g]
            st = [sbuf[b, j] for j in range(npair)]
            ys = [_dot_nt(qe_p[n], st[j]) for j, n in enumerate(ns)]
            for j, n in enumerate(ns):
                y_s[rows, lanes(j)] = ys[j] + yv[n]
                sbuf[b, j] = jnp.where(bdm, st[j] * per[g]["tot"][:, lanes(j)] + upd[n], 0.0)
        return carry

    lax.fori_loop(0, nb * nc // grp, group, 0)

    o = y_s[...] + diag
    ms = _segsum(o * o, seg_ref) * (1.0 / HEAD)
    yd = o * lax.rsqrt(ms + NORM_EPS) * ng_ref[...]
    y_ref[...] = (yd * _silu(gd)).reshape(nb, tb, c)

    @pl.when(_last_step())
    def _():
        so_ref[...] = sbuf[...]


def _post_kernel(ya_ref, yb_ref, yc_ref, yd_ref, x_ref, wo_ref, gpost_ref, gprex_ref, wq_ref,
                 mk_ref, mv_ref, wox_ref, gpostx_ref, o_ref, *, nb, tb, heads):
    d = x_ref.shape[-1]
    c = ya_ref.shape[-1]
    hd = d // heads
    rows = nb * tb
    y = None
    for i, ref in enumerate((ya_ref, yb_ref, yc_ref, yd_ref)):
        t = _dot(ref[...].reshape(rows, c), wo_ref[i * c:(i + 1) * c, :])
        y = t if y is None else y + t
    x1 = x_ref[...].reshape(rows, d) + _rms(y, gpost_ref[...])
    q = _dot(_rms(x1, gprex_ref[...]), wq_ref[...])
    scale = hd ** -0.5
    pairs = [(b, h) for b in range(nb) for h in range(heads)]
    cols = lambda h: slice(h * hd, (h + 1) * hd)
    scores = [_dot_nt(q[b * tb:(b + 1) * tb, cols(h)], mk_ref[b, :, cols(h)]) * scale for b, h in pairs]
    probs = []
    for s in scores:
        e = jnp.exp(s - jnp.max(s, axis=-1, keepdims=True))
        probs.append(e * (1.0 / jnp.sum(e, axis=-1, keepdims=True)))
    outs = [_dot(pr, mv_ref[b, :, cols(h)]) for pr, (b, h) in zip(probs, pairs)]
    o = jnp.concatenate(
        [jnp.concatenate(outs[b * heads:(b + 1) * heads], axis=1) for b in range(nb)], axis=0)
    x2 = x1 + _rms(_dot(o, wox_ref[...]), gpostx_ref[...])
    o_ref[...] = x2.reshape(nb, tb, d)


def _memkv_kernel(m_ref, g_ref, wk_ref, wv_ref, k_ref, v_ref):
    m = _rms(m_ref[0], g_ref[...])
    k_ref[0] = _dot(m, wk_ref[...])
    v_ref[0] = _dot(m, wv_ref[...])


def _full(shape):
    nd = len(shape)
    return pl.BlockSpec(shape, lambda b, t: (0,) * nd)


def _per_batch(shape_tail, nb):
    nd = len(shape_tail)
    return pl.BlockSpec((nb,) + shape_tail, lambda b, t: (b,) + (0,) * nd)


def _state_in(shape_tail, nb, layer):
    nd = len(shape_tail)
    return pl.BlockSpec((None, nb) + shape_tail, lambda b, t: (layer, b) + (0,) * nd)


def _tokens(nb, tb, width):
    return pl.BlockSpec((nb, tb, width), lambda b, t: (b, t, 0))


def _params():
    return pltpu.CompilerParams(dimension_semantics=("arbitrary", "arbitrary"),
                                vmem_limit_bytes=VMEM_LIMIT)


def _call(kern, name, grid, in_arrays, in_specs, out_shapes, out_specs, scratch):
    return pl.pallas_call(
        kern, name=name, grid=grid, in_specs=in_specs, out_specs=out_specs,
        out_shape=out_shapes, scratch_shapes=scratch, compiler_params=_params())(*in_arrays)


def _row(v):
    return v.reshape(1, -1).astype(F32)


def _lru_call(x, lp, conv0, h0, layer, nb, tb):
    B, T, D = x.shape
    c = lp["lru_cw"].shape[-1]
    grid = (B // nb, T // tb)
    ins = [x, lp["g_pre"], lp["w_a"], lp["lru_cw"], lp["lru_cb"], lp["lru_wg"], lp["lru_bg"], lp["lru_lam"],
           conv0, h0]
    specs = [_tokens(nb, tb, D)] + [_full(a.shape) for a in ins[1:8]] + [
        _state_in((CONV_W - 1, c), nb, layer), _state_in((1, c), nb, layer)]
    outs = [jax.ShapeDtypeStruct((B, T, c), F32), jax.ShapeDtypeStruct((B, CONV_W - 1, c), F32),
            jax.ShapeDtypeStruct((B, 1, c), F32)]
    ospecs = [_tokens(nb, tb, c), _per_batch((CONV_W - 1, c), nb), _per_batch((1, c), nb)]
    scratch = [pltpu.VMEM((nb, tb + CONV_PAD, c), F32), pltpu.VMEM((nb, 1, c), F32)]
    return _call(functools.partial(_lru_kernel, nb=nb, tb=tb), "mix_lru", grid, ins, specs, outs, ospecs, scratch)


def _rwkv_call(x, lp, consts, shift0, s0, layer, nb, tb, L):
    B, T, D = x.shape
    c = lp["rw_w0"].shape[-1]
    nsh = shift0.shape[-1]
    npair = c // PAIR
    grid = (B // nb, T // tb)
    ins = [x, lp["g_pre"], lp["w_b"], lp["rw_mu"], lp["rw_w0"], lp["rw_a0"], lp["rw_wup"], lp["rw_kk"],
           lp["rw_ka"], lp["rw_rk"], lp["rw_lng"], lp["rw_lnb"], consts["seg"], consts["tri"], shift0, s0]
    specs = [_tokens(nb, tb, D)] + [_full(a.shape) for a in ins[1:14]] + [
        _state_in((1, nsh), nb, layer), _state_in((npair, PAIR, PAIR), nb, layer)]
    outs = [jax.ShapeDtypeStruct((B, T, c), F32), jax.ShapeDtypeStruct((B, 1, nsh), F32),
            jax.ShapeDtypeStruct((B, npair, PAIR, PAIR), F32)]
    ospecs = [_tokens(nb, tb, c), _per_batch((1, nsh), nb), _per_batch((npair, PAIR, PAIR), nb)]
    rows = nb * tb
    scratch = [pltpu.VMEM((nb, 1, nsh), F32), pltpu.VMEM((nb, npair, PAIR, PAIR), F32)] + [
        pltpu.VMEM((rows, c), F32) for _ in range(7)]
    return _call(functools.partial(_rwkv_kernel, nb=nb, tb=tb, L=L), "mix_rwkv", grid, ins, specs, outs,
                 ospecs, scratch)


def _ssd_call(x, lp, consts, conv0, h0, layer, nb, tb, L):
    B, T, D = x.shape
    c = lp["ssd_dtb"].shape[-1]
    nconv = lp["ssd_cw"].shape[-1]
    nbc = (nconv - c) // 2
    npair = c // PAIR
    grid = (B // nb, T // tb)
    ins = [x, lp["g_pre"], lp["w_c"], lp["ssd_cw"], lp["ssd_cb"], lp["ssd_dtb"], lp["ssd_alog"], lp["ssd_d"],
           lp["ssd_ng"], consts["tri"], consts["pick"], conv0, h0]
    specs = [_tokens(nb, tb, D)] + [_full(a.shape) for a in ins[1:11]] + [
        _state_in((CONV_W - 1, nconv), nb, layer), _state_in((npair, PAIR, PAIR), nb, layer)]
    outs = [jax.ShapeDtypeStruct((B, T, c), F32), jax.ShapeDtypeStruct((B, CONV_W - 1, nconv), F32),
            jax.ShapeDtypeStruct((B, npair, PAIR, PAIR), F32)]
    ospecs = [_tokens(nb, tb, c), _per_batch((CONV_W - 1, nconv), nb), _per_batch((npair, PAIR, PAIR), nb)]
    rows = nb * tb
    scratch = [pltpu.VMEM((nb, tb + CONV_PAD, nconv), F32), pltpu.VMEM((nb, npair, PAIR, PAIR), F32),
               pltpu.VMEM((rows, c), F32), pltpu.VMEM((rows, nbc), F32), pltpu.VMEM((rows, nbc), F32),
               pltpu.VMEM((rows, c), F32), pltpu.VMEM((rows, c), F32), pltpu.VMEM((rows, c), F32)]
    return _call(functools.partial(_ssd_kernel, nb=nb, tb=tb, L=L), "mix_ssd", grid, ins, specs, outs,
                 ospecs, scratch)


def _hgrn_call(x, lp, consts, s0, layer, nb, tb, L):
    B, T, D = x.shape
    c = lp["hg_ng"].shape[-1]
    npair = c // PAIR
    grid = (B // nb, T // tb)
    ins = [x, lp["g_pre"], lp["w_d"], lp["hg_lbl"], lp["hg_ng"], consts["tri"], consts["lmask"], consts["seg"], s0]
    specs = [_tokens(nb, tb, D)] + [_full(a.shape) for a in ins[1:8]] + [_state_in((npair, PAIR, PAIR), nb, layer)]
    outs = [jax.ShapeDtypeStruct((B, T, c), F32), jax.ShapeDtypeStruct((B, npair, PAIR, PAIR), F32)]
    ospecs = [_tokens(nb, tb, c), _per_batch((npair, PAIR, PAIR), nb)]
    rows = nb * tb
    scratch = [pltpu.VMEM((nb, npair, PAIR, PAIR), F32)] + [pltpu.VMEM((rows, c), F32) for _ in range(5)]
    return _call(functools.partial(_hgrn_kernel, nb=nb, tb=tb, L=L, layer=layer), "mix_hgrn", grid, ins, specs,
                 outs, ospecs, scratch)


def _post_call(ys, x, lp, mk, mv, layer, nb, tb, heads):
    B, T, D = x.shape
    c = ys[0].shape[-1]
    M = mk.shape[2]
    grid = (B // nb, T // tb)
    ins = list(ys) + [x, lp["w_out"], lp["g_post"], lp["g_pre_x"], lp["w_q"], mk, mv, lp["w_o"], lp["g_post_x"]]
    kv_spec = pl.BlockSpec((None, nb, M, D), lambda b, t: (layer, b, 0, 0))
    specs = [_tokens(nb, tb, c)] * 4 + [_tokens(nb, tb, D)] + [_full(a.shape) for a in ins[5:9]] + [
        kv_spec, kv_spec] + [_full(a.shape) for a in ins[11:13]]
    return _call(functools.partial(_post_kernel, nb=nb, tb=tb, heads=heads), "post_attn", grid, ins, specs,
                 jax.ShapeDtypeStruct((B, T, D), F32), _tokens(nb, tb, D), [])


def _memkv_call(mem, g, wk, wv):
    B, M, D = mem.shape
    depth = wk.shape[0]
    wspec = pl.BlockSpec((None, D, D), lambda l, b: (l, 0, 0))
    ospec = pl.BlockSpec((None, 1, M, D), lambda l, b: (l, b, 0, 0))
    return pl.pallas_call(
        _memkv_kernel, name="mem_kv", grid=(depth, B),
        in_specs=[pl.BlockSpec((1, M, D), lambda l, b: (b, 0, 0)),
                  pl.BlockSpec((None, 1, D), lambda l, b: (l, 0, 0)), wspec, wspec],
        out_specs=[ospec, ospec],
        out_shape=[jax.ShapeDtypeStruct((depth, B, M, D), F32)] * 2,
        compiler_params=_params())(mem, g, wk, wv)


def _chunk_consts(L, c):
    r = np.arange(L)[:, None]
    j = np.arange(L)[None, :]
    tri = (j <= r).astype(np.float32)
    masks = []
    col = np.arange(2 * L)[None, :]
    s = col % L
    m = 1
    while (1 << m) <= L:
        size, half = 1 << m, 1 << (m - 1)
        masks.append(((r // size == s // size) & (r % size >= half) & (s % size < half)).astype(np.float32))
        m += 1
    seg = (np.arange(c)[:, None] // HEAD == np.arange(c)[None, :] // HEAD).astype(np.float32)
    pick = np.zeros((SUBLANES, PAIR), np.float32)
    pick[0, 0] = 1.0
    pick[1, HEAD] = 1.0
    return {
        "tri": jnp.asarray(tri, BF16),
        "lmask": jnp.asarray(np.stack(masks), F32),
        "seg": jnp.asarray(seg, BF16),
        "pick": jnp.asarray(pick, BF16),
    }


def _to_pairs(s):
    lead, (H, a, b) = s.shape[:-3], s.shape[-3:]
    s = s.reshape(lead + (H // 2, 2, a, b))
    z = jnp.zeros_like(s[..., 0, :, :])
    top = jnp.concatenate([s[..., 0, :, :], z], axis=-1)
    bot = jnp.concatenate([z, s[..., 1, :, :]], axis=-1)
    return jnp.concatenate([top, bot], axis=-2)


def _from_pairs(s):
    lead, P = s.shape[:-3], s.shape[-3]
    both = jnp.stack([s[..., :HEAD, :HEAD], s[..., HEAD:, HEAD:]], axis=-3)
    return both.reshape(lead + (2 * P, HEAD, HEAD))


def _expand_bc(t, c, n):
    xs, bm, cm = t[..., :c], t[..., c:c + 2 * n], t[..., c + 2 * n:]
    dup = lambda u: jnp.concatenate([u[..., :n], u[..., :n], u[..., n:], u[..., n:]], axis=-1)
    return jnp.concatenate([xs, dup(bm), dup(cm)], axis=-1)


def _shrink_bc(t, c, n):
    pick = lambda u: jnp.concatenate([u[..., :n], u[..., 2 * n:3 * n]], axis=-1)
    return jnp.concatenate([t[..., :c], pick(t[..., c:c + 4 * n]), pick(t[..., c + 4 * n:])], axis=-1)


def _blockdiag(w):
    n, d, e = w.shape
    eye = jnp.eye(n, dtype=w.dtype)
    return (eye[:, None, :, None] * w[:, :, None, :]).reshape(n * d, n * e)


def kernel(x_prompt, x_sample, state_lru_conv, state_lru_h, state_rwkv_shift, state_rwkv_wkv, state_ssd_conv, state_ssd_h, state_hgrn_s, cache_mem_k, cache_mem_v, mem_prompt, g_pre, g_post, g_pre_x, g_post_x, w_in, w_out, lru_conv_w, lru_conv_b, lru_w_r, lru_b_r, lru_w_i, lru_b_i, lru_lambda, rwkv_mu, rwkv_w0, rwkv_w_up, rwkv_a0, rwkv_a_up, rwkv_k_k, rwkv_k_a, rwkv_r_k, rwkv_ln_g, rwkv_ln_b, ssd_conv_w, ssd_conv_b, ssd_dt_bias, ssd_a_log, ssd_d, ssd_norm_g, hgrn_lb_logits, hgrn_norm_g, mem_g, mem_w_q, mem_w_k, mem_w_v, mem_w_o):
    depth = w_in.shape[0]
    D = x_prompt.shape[-1]
    c = lru_conv_w.shape[-1]
    nsh = rwkv_mu.shape[-1]
    rank = (nsh - 3 * c) // 2
    ssd_heads = ssd_dt_bias.shape[-1]
    nstate = (ssd_conv_w.shape[-1] - c) // 4
    heads_x = cache_mem_k.shape[-2]
    mem_len = mem_prompt.shape[1]

    o_b = 2 * c
    o_c = o_b + nsh + c
    o_dt = o_c + c + 4 * nstate
    o_z = o_dt + ssd_heads
    o_d = o_z + c

    layers = []
    for l in range(depth):
        w = w_in[l]
        zero = jnp.zeros((rank, c), F32)
        wup = jnp.concatenate([jnp.concatenate([rwkv_w_up[l], zero], axis=1),
                               jnp.concatenate([zero, rwkv_a_up[l]], axis=1)], axis=0)
        rep = lambda v: jnp.repeat(v, c // ssd_heads, axis=-1)
        w_c = jnp.concatenate([_expand_bc(w[:, o_c:o_dt], c, nstate), rep(w[:, o_dt:o_z]), w[:, o_z:o_d]], axis=1)
        layers.append({
            "g_pre": _row(g_pre[l]), "g_post": _row(g_post[l]), "g_pre_x": _row(g_pre_x[l]),
            "g_post_x": _row(g_post_x[l]),
            "w_a": w[:, :o_b].astype(BF16), "w_b": w[:, o_b:o_c].astype(BF16), "w_c": w_c.astype(BF16),
            "w_d": w[:, o_d:].astype(BF16),
            "w_out": w_out[l].astype(BF16), "w_q": mem_w_q[l].astype(BF16), "w_o": mem_w_o[l].astype(BF16),
            "lru_cw": lru_conv_w[l], "lru_cb": _row(lru_conv_b[l]),
            "lru_wg": jnp.concatenate([_blockdiag(lru_w_r[l]), _blockdiag(lru_w_i[l])], axis=1).astype(BF16),
            "lru_bg": _row(jnp.concatenate([lru_b_r[l], lru_b_i[l]])), "lru_lam": _row(lru_lambda[l]),
            "rw_mu": _row(rwkv_mu[l]), "rw_w0": _row(rwkv_w0[l]), "rw_a0": _row(rwkv_a0[l]),
            "rw_wup": wup.astype(BF16), "rw_kk": _row(rwkv_k_k[l]), "rw_ka": _row(rwkv_k_a[l]),
            "rw_rk": _row(rwkv_r_k[l]), "rw_lng": _row(rwkv_ln_g[l]), "rw_lnb": _row(rwkv_ln_b[l]),
            "ssd_cw": _expand_bc(ssd_conv_w[l], c, nstate), "ssd_cb": _row(_expand_bc(ssd_conv_b[l], c, nstate)),
            "ssd_dtb": _row(rep(ssd_dt_bias[l])), "ssd_alog": _row(rep(ssd_a_log[l])), "ssd_d": _row(rep(ssd_d[l])),
            "ssd_ng": _row(ssd_norm_g[l]),
            "hg_lbl": hgrn_lb_logits.astype(F32), "hg_ng": _row(hgrn_norm_g[l]),
        })

    def run(x, mk, mv, conv_a, h_a, shift_b, wkv_b, conv_c, h_c, s_d, nb, tb, L):
        consts = _chunk_consts(L, c)
        B = x.shape[0]
        h_a = h_a.reshape(depth, B, 1, c)
        shift_b = shift_b.reshape(depth, B, 1, nsh)
        wkv_b = _to_pairs(wkv_b)
        conv_c = _expand_bc(conv_c, c, nstate)
        h_c = _to_pairs(h_c)
        s_d = _to_pairs(jnp.swapaxes(s_d, -1, -2))
        acc = [[] for _ in range(7)]
        for l in range(depth):
            lp = layers[l]
            ya, nca, nha = _lru_call(x, lp, conv_a, h_a, l, nb, tb)
            yb, nsb, nwb = _rwkv_call(x, lp, consts, shift_b, wkv_b, l, nb, tb, L)
            yc, ncc, nhc = _ssd_call(x, lp, consts, conv_c, h_c, l, nb, tb, L)
            yd, nsd = _hgrn_call(x, lp, consts, s_d, l, nb, tb, L)
            x = _post_call((ya, yb, yc, yd), x, lp, mk, mv, l, nb, tb, heads_x)
            for lst, val in zip(acc, (nca, nha, nsb, nwb, ncc, nhc, nsd)):
                lst.append(val)
        nca, nha, nsb, nwb, ncc, nhc, nsd = (jnp.stack(v) for v in acc)
        return (x, nca, nha.reshape(depth, B, c), nsb.reshape(depth, B, nsh), _from_pairs(nwb),
                _shrink_bc(ncc, c, nstate), _from_pairs(nhc), jnp.swapaxes(_from_pairs(nsd), -1, -2))

    Bp, Tp = x_prompt.shape[:2]
    Bs, Ts = x_sample.shape[:2]
    k_p, v_p = _memkv_call(mem_prompt, mem_g.reshape(depth, 1, D).astype(F32), mem_w_k.astype(BF16),
                           mem_w_v.astype(BF16))
    kv_shape = (depth, Bp, mem_len, heads_x, D // heads_x)
    mem_k_p = k_p.reshape(kv_shape)
    mem_v_p = v_p.reshape(kv_shape)
    zeros = lambda *s: jnp.zeros((depth, Bp) + s, F32)
    Lp = CHUNK if Tp % CHUNK == 0 else Tp
    Ls = CHUNK if Ts % CHUNK == 0 else Ts
    tb_p = PROMPT_CHUNKS * Lp if Tp % (PROMPT_CHUNKS * Lp) == 0 else Lp
    nb_s = SAMPLE_BATCH if Bs % SAMPLE_BATCH == 0 else 1
    (y_prompt, lru_conv_p, lru_h_p, rwkv_shift_p, rwkv_wkv_p, ssd_conv_p, ssd_h_p, hgrn_s_p) = run(
        x_prompt, k_p, v_p,
        zeros(CONV_W - 1, c), zeros(c), zeros(nsh), zeros(c // HEAD, HEAD, HEAD),
        zeros(CONV_W - 1, c + 4 * nstate), zeros(ssd_heads, HEAD, nstate), zeros(c // HEAD, HEAD, HEAD),
        1, tb_p, Lp)
    (y_sample, lru_conv_s, lru_h_s, rwkv_shift_s, rwkv_wkv_s, ssd_conv_s, ssd_h_s, hgrn_s_s) = run(
        x_sample, cache_mem_k.reshape(depth, Bs, mem_len, D).astype(BF16),
        cache_mem_v.reshape(depth, Bs, mem_len, D).astype(BF16),
        state_lru_conv, state_lru_h, state_rwkv_shift, state_rwkv_wkv,
        state_ssd_conv, state_ssd_h, state_hgrn_s, nb_s, Ts, Ls)
    return (y_prompt, y_sample, lru_conv_p, lru_conv_s, lru_h_p, lru_h_s, rwkv_shift_p, rwkv_shift_s,
            rwkv_wkv_p, rwkv_wkv_s, ssd_conv_p, ssd_conv_s, ssd_h_p, ssd_h_s, hgrn_s_p, hgrn_s_s,
            mem_k_p, mem_v_p)
```

```python
import functools

import numpy as np
import jax
import jax.numpy as jnp
from jax import lax
from jax.experimental import pallas as pl
from jax.experimental.pallas import tpu as pltpu

F32 = jnp.float32
BF16 = jnp.bfloat16

SUBLANES = 8
MXU_TILE = 256
HEAD = 64
PAIR = 2 * HEAD
CHUNK = 64
CONV_W = 4
CONV_PAD = 8
LRU_C = 8.0
RWKV_DECAY_SCALE = 0.6065306597126334
RWKV_LN_EPS = 64e-5
NORM_EPS = 1e-6
VMEM_LIMIT = 56 * 1024 * 1024
PROMPT_CHUNKS = 8
SAMPLE_BATCH = 8


def _dot(a, b):
    return jnp.dot(a.astype(BF16), b.astype(BF16), preferred_element_type=F32)


def _dot_nt(a, b):
    return lax.dot_general(a.astype(BF16), b.astype(BF16), (((1,), (1,)), ((), ())),
                           preferred_element_type=F32)


def _dot_tn(a, b):
    return lax.dot_general(a.astype(BF16), b.astype(BF16), (((0,), (0,)), ((), ())),
                           preferred_element_type=F32)


def _split3(x):
    hi = x.astype(BF16)
    r1 = x - hi.astype(F32)
    mid = r1.astype(BF16)
    lo = (r1 - mid.astype(F32)).astype(BF16)
    return hi, mid, lo


def _group_size(n):
    for g in (8, 4, 2):
        if n % g == 0:
            return g
    return 1


def _halves(n):
    return [list(range(n // 2)), list(range(n // 2, n))] if n >= 2 else [list(range(n))]


def _dot01(m01, x):
    hi, mid, lo = _split3(x)
    d = lambda p: jnp.dot(m01, p, preferred_element_type=F32)
    return d(hi) + d(mid) + d(lo)


def _dot01_nt(m01, x):
    hi, mid, lo = _split3(x)
    d = lambda p: lax.dot_general(m01, p, (((1,), (1,)), ((), ())), preferred_element_type=F32)
    return d(hi) + d(mid) + d(lo)


def _rms(x, g):
    return x * lax.rsqrt(jnp.mean(x * x, axis=-1, keepdims=True) + NORM_EPS) * g


def _sigmoid(x):
    return jax.nn.sigmoid(x)


def _silu(x):
    return x * jax.nn.sigmoid(x)


def _log1pexp_neg_abs(x):
    return jnp.log(1.0 + jnp.exp(-jnp.abs(x)))


def _softplus(x):
    return jnp.maximum(x, 0.0) + _log1pexp_neg_abs(x)


def _segsum(x, seg_ref):
    w = min(MXU_TILE, x.shape[-1])
    seg = seg_ref[0:w, 0:w]
    return jnp.concatenate([_dot(x[:, s:s + w], seg) for s in range(0, x.shape[-1], w)], axis=1)


def _neg_expm1(z):
    return -jnp.tanh(0.5 * z) * (jnp.exp(z) + 1.0)


def _head_masks():
    lane = lax.broadcasted_iota(jnp.int32, (1, PAIR), 1)
    return lane < HEAD, lane >= HEAD


def _stack_heads(x, m0, m1):
    return jnp.concatenate([jnp.where(m0, x, 0.0), jnp.where(m1, x, 0.0)], axis=0)


def _bd_mask():
    r = lax.broadcasted_iota(jnp.int32, (PAIR, PAIR), 0)
    c = lax.broadcasted_iota(jnp.int32, (PAIR, PAIR), 1)
    return (r >= HEAD) == (c >= HEAD)


def _cat_masks(L):
    t = lax.broadcasted_iota(jnp.int32, (L, 2 * L), 0)
    c = lax.broadcasted_iota(jnp.int32, (L, 2 * L), 1)
    s = jnp.where(c >= L, c - L, c)
    return s < t, s <= t, c < L


def _cat_to_blockdiag(m, first):
    return jnp.concatenate([jnp.where(first, m, 0.0), jnp.where(first, 0.0, m)], axis=0)


def _tri_inv_stages(nns, L):
    n = nns[0].shape[0]
    ri = lax.broadcasted_iota(jnp.int32, (n, n), 0)
    ci = lax.broadcasted_iota(jnp.int32, (n, n), 1)
    tinv = [jnp.where((ri >> 1) == (ci >> 1), t, 0.0) + (ri == ci).astype(F32) for t in nns]
    shift = 1
    while (1 << shift) < L:
        join = ((ri >> (shift + 1)) == (ci >> (shift + 1))) & ((ri >> shift) != (ci >> shift))
        w = [_dot(t, jnp.where(join, x, 0.0)) for t, x in zip(tinv, nns)]
        yield None
        tinv = [t + _dot(w_, t) for t, w_ in zip(tinv, w)]
        shift += 1
        yield None
    yield tinv


def _run_stages(*gens):
    last = [None] * len(gens)
    live = list(range(len(gens)))
    while live:
        for k in list(live):
            try:
                out = next(gens[k])
            except StopIteration:
                live.remove(k)
                continue
            if out is not None:
                last[k] = out
    return last


def _mid_rows(b, m):
    n, c = b.shape
    size, half = 1 << m, 1 << (m - 1)
    if size >= 2 * SUBLANES:
        return jnp.concatenate(
            [jnp.broadcast_to(b[s + half - 1:s + half, :], (size, c)) for s in range(0, n, size)], axis=0)
    b3 = b.reshape(n // SUBLANES, SUBLANES, c)
    sub = lax.broadcasted_iota(jnp.int32, b3.shape, 1)
    mids = list(range(half - 1, SUBLANES, size))
    out = jnp.broadcast_to(b3[:, mids[-1]:mids[-1] + 1, :], b3.shape)
    for mid in reversed(mids[:-1]):
        out = jnp.where(sub <= mid + half, b3[:, mid:mid + 1, :], out)
    return out.reshape(n, c)


def _shift_rows(x, d, fill):
    row = lax.broadcasted_iota(jnp.int32, x.shape, 0)
    return jnp.where(row >= d, pltpu.roll(x, d, axis=0), fill)


def _affine_scan(a, u, h_prev):
    n, c = a.shape
    sub = lax.broadcasted_iota(jnp.int32, (n, c), 0) & (SUBLANES - 1)
    d = 1
    while d < SUBLANES:
        keep = sub >= d
        a_s = jnp.where(keep, pltpu.roll(a, d, axis=0), 1.0)
        u_s = jnp.where(keep, pltpu.roll(u, d, axis=0), 0.0)
        u = a * u_s + u
        a = a * a_s
        d *= 2
    nt = n // SUBLANES
    a3 = a.reshape(nt, SUBLANES, c)
    u3 = u.reshape(nt, SUBLANES, c)
    ta = jnp.broadcast_to(a3[:, SUBLANES - 1:SUBLANES, :], a3.shape)
    tu = jnp.broadcast_to(u3[:, SUBLANES - 1:SUBLANES, :], u3.shape)
    carry = jnp.broadcast_to(h_prev, (SUBLANES, c))
    carries = []
    for k in range(nt):
        carries.append(carry)
        carry = ta[k] * carry + tu[k]
    return u + a * jnp.concatenate(carries, axis=0)


def _causal_conv(cbuf, b, xa, tb, w_ref, bias):
    cbuf[b, CONV_PAD:CONV_PAD + tb, :] = xa
    y = bias + xa * w_ref[CONV_W - 1:CONV_W, :]
    for j in range(1, CONV_W):
        y = y + cbuf[b, CONV_PAD - j:CONV_PAD - j + tb, :] * w_ref[CONV_W - 1 - j:CONV_W - j, :]
    hist = cbuf[b, tb + CONV_PAD - (CONV_W - 1):tb + CONV_PAD, :]
    cbuf[b, CONV_PAD - (CONV_W - 1):CONV_PAD, :] = hist
    return y


def _first_step():
    return pl.program_id(1) == 0


def _last_step():
    return pl.program_id(1) == pl.num_programs(1) - 1


def _lru_kernel(x_ref, g_ref, w_ref, cw_ref, cb_ref, wg_ref, bg_ref, lam_ref, conv0_ref, h0_ref,
                y_ref, convo_ref, ho_ref, cbuf, hbuf, *, nb, tb):
    c = y_ref.shape[-1]

    @pl.when(_first_step())
    def _():
        cbuf[:, CONV_PAD - (CONV_W - 1):CONV_PAD, :] = conv0_ref[...]
        hbuf[...] = h0_ref[...]

    x = x_ref[...].reshape(nb * tb, x_ref.shape[-1])
    p = _dot(_rms(x, g_ref[...]), w_ref[...])
    xa = p[:, :c]
    ga = p[:, c:]
    xc = jnp.concatenate(
        [_causal_conv(cbuf, b, xa[b * tb:(b + 1) * tb], tb, cw_ref, cb_ref[...]) for b in range(nb)],
        axis=0)
    gates = _dot(xc, wg_ref[...]) + bg_ref[...]
    r = _sigmoid(gates[:, :c])
    i = _sigmoid(gates[:, c:])
    log_a = (-LRU_C) * r * _softplus(-lam_ref[...])
    a = jnp.exp(log_a)
    u = jnp.sqrt(_neg_expm1(2.0 * log_a)) * (i * xc)
    hs = []
    for b in range(nb):
        h = _affine_scan(a[b * tb:(b + 1) * tb], u[b * tb:(b + 1) * tb], hbuf[b])
        hbuf[b] = h[tb - 1:tb, :]
        hs.append(h)
    h = jnp.concatenate(hs, axis=0)
    y_ref[...] = (h * _silu(ga)).reshape(nb, tb, c)

    @pl.when(_last_step())
    def _():
        convo_ref[...] = cbuf[:, CONV_PAD - (CONV_W - 1):CONV_PAD, :]
        ho_ref[...] = hbuf[...]


def _rwkv_kernel(x_ref, g_ref, w_ref, mu_ref, w0_ref, a0_ref, wup_ref, kk_ref, ka_ref, rk_ref,
                 lng_ref, lnb_ref, seg_ref, tri_ref, shift0_ref, s0_ref,
                 y_ref, shifto_ref, so_ref,
                 shbuf, sbuf, lw_s, kk_s, kka_s, kp_s, r_s, v_s, y_s, *, nb, tb, L):
    c = y_ref.shape[-1]
    nc = tb // L
    npair = c // PAIR

    @pl.when(_first_step())
    def _():
        shbuf[...] = shift0_ref[...]
        sbuf[...] = s0_ref[...]

    x = x_ref[...].reshape(nb * tb, x_ref.shape[-1])
    p = _dot(_rms(x, g_ref[...]), w_ref[...])
    nsh = shbuf.shape[-1]
    pb = p[:, :nsh]
    gb = p[:, nsh:]
    xms = []
    for b in range(nb):
        slab = pb[b * tb:(b + 1) * tb]
        sh = _shift_rows(slab, 1, shbuf[b])
        shbuf[b] = slab[tb - 1:tb, :]
        xms.append(slab + (sh - slab) * mu_ref[...])
    xm = jnp.concatenate(xms, axis=0)
    r = xm[:, 0:c]
    k = xm[:, c:2 * c]
    v = xm[:, 2 * c:3 * c]
    lowrank = xm[:, 3 * c:]
    lane = lax.broadcasted_iota(jnp.int32, (1, lowrank.shape[-1]), 1)
    lowrank = jnp.where(lane < lowrank.shape[-1] // 2, jnp.tanh(lowrank), lowrank)
    up = _dot(lowrank, wup_ref[...])
    lw = (-RWKV_DECAY_SCALE) * _sigmoid(w0_ref[...] + up[:, :c])
    a = _sigmoid(a0_ref[...] + up[:, c:])
    kk = k * kk_ref[...]
    kk = kk * lax.rsqrt(_segsum(kk * kk, seg_ref) + 1e-12)
    kp = k * (1.0 + (a - 1.0) * ka_ref[...])
    lw_s[...] = lw
    kk_s[...] = kk
    kka_s[...] = kk * a
    kp_s[...] = kp
    r_s[...] = r
    v_s[...] = v
    bonus = _segsum(r * kp * rk_ref[...], seg_ref) * v

    m0, m1 = _head_masks()
    bd = _bd_mask()
    strict, incl, first = _cat_masks(L)
    tri = tri_ref[...]

    grp = _group_size(nb * nc)
    lanes = lambda j: slice(j * PAIR, (j + 1) * PAIR)

    def prepare(i, gs):
        per = {}
        for g in gs:
            it = i * grp + g
            rows = pl.ds(pl.multiple_of(it * L, L), L)
            lw_c = lw_s[rows, :]
            cl = _dot01(tri, lw_c)
            cl_last = cl[L - 1:L, :]
            e_ip = jnp.exp(-cl)
            e_rel = jnp.exp(cl_last - cl)
            kk_c = kk_s[rows, :]
            kka_c = kka_s[rows, :]
            kp_c = kp_s[rows, :]
            per[g] = dict(
                b=it // nc, rows=rows,
                ab=-kk_c * jnp.exp(cl - lw_c), bb=kka_c * e_ip, kb=kp_c * e_ip,
                rb=r_s[rows, :] * jnp.exp(cl), bbl=kka_c * e_rel, kbl=kp_c * e_rel,
                p_last=jnp.exp(cl_last), v=v_s[rows, :])
        chains = [(g, j) for g in gs for j in range(npair)]
        part = lambda key: [per[g][key][:, lanes(j)] for g, j in chains]
        abp, rbp, vp = part("ab"), part("rb"), part("v")
        bblp, kblp = part("bbl"), part("kbl")
        zb = [_stack_heads(t, m0, m1) for t in part("bb")]
        zk = [_stack_heads(t, m0, m1) for t in part("kb")]
        arp = [jnp.concatenate([a, r], axis=0) for a, r in zip(abp, rbp)]
        g1 = [_dot_nt(x_, jnp.concatenate([b_, k_], axis=0))
              for x_, b_, k_ in zip(arp, zb, zk)]
        nn = [_cat_to_blockdiag(jnp.where(strict, t[0:L, 0:2 * L], 0.0), first) for t in g1]
        mk = [_cat_to_blockdiag(jnp.where(strict, t[0:L, 2 * L:4 * L], 0.0), first) for t in g1]
        r_b = [jnp.where(incl, t[L:2 * L, 0:2 * L], 0.0) for t in g1]
        r_k = [jnp.where(incl, t[L:2 * L, 2 * L:4 * L], 0.0) for t in g1]
        vst = [_stack_heads(t, m0, m1) for t in vp]
        kv = [_dot(jnp.concatenate([m_, k_], axis=0), v_) for m_, k_, v_ in zip(mk, r_k, vst)]
        bkl = [jnp.concatenate([b_, k_], axis=0) for b_, k_ in zip(bblp, kblp)]
        yield None
        tinv = None
        for out in _tri_inv_stages(nn, L):
            if out is None:
                yield None
            else:
                tinv = out
        yield dict(per=per, chains=chains, arp=arp, r_b=r_b, kv=kv, bkl=bkl, vp=vp, tinv=tinv)

    def sequential(st, gs):
        for g in gs:
            ns = [n for n, (gg, _) in enumerate(st["chains"]) if gg == g]
            meta = st["per"][g]
            b, rows, p_last = meta["b"], meta["rows"], meta["p_last"]
            s = [sbuf[b, j] for j in range(npair)]
            ars = [_dot_nt(st["arp"][n], s[j]) for j, n in enumerate(ns)]
            yield None
            ust = [_dot(st["tinv"][n], _stack_heads(ars[j][0:L], m0, m1) + st["kv"][n][0:2 * L])
                   for j, n in enumerate(ns)]
            yield None
            yv = [ars[j][L:2 * L] + _dot(st["r_b"][n], ust[j]) + st["kv"][n][2 * L:3 * L]
                  for j, n in enumerate(ns)]
            su = [_dot_tn(jnp.concatenate([ust[j][0:L] + ust[j][L:2 * L], st["vp"][n]], axis=0), st["bkl"][n])
                  for j, n in enumerate(ns)]
            for j, n in enumerate(ns):
                sbuf[b, j] = jnp.where(bd, s[j] * p_last[:, lanes(j)] + su[j], 0.0)
                y_s[rows, lanes(j)] = yv[j]
            yield None

    def group(i, carry):
        early, late = list(range(grp // 2)), list(range(grp // 2, grp))
        if not early:
            st, = _run_stages(prepare(i, late))
            _run_stages(sequential(st, late))
            return carry
        st_early, = _run_stages(prepare(i, early))
        st_late, _ = _run_stages(prepare(i, late), sequential(st_early, early))
        _run_stages(sequential(st_late, late))
        return carry

    lax.fori_loop(0, nb * nc // grp, group, 0)

    y = y_s[...]
    inv = 1.0 / HEAD
    mean = _segsum(y, seg_ref) * inv
    yc = y - mean
    var = _segsum(yc * yc, seg_ref) * inv
    yn = yc * lax.rsqrt(var + RWKV_LN_EPS) * lng_ref[...] + lnb_ref[...]
    y_ref[...] = ((yn + bonus) * _silu(gb)).reshape(nb, tb, c)

    @pl.when(_last_step())
    def _():
        shifto_ref[...] = shbuf[...]
        so_ref[...] = sbuf[...]


def _ssd_kernel(x_ref, g_ref, w_ref, cw_ref, cb_ref, dtb_ref, alog_ref, d_ref, ng_ref, tri_ref,
                pick_ref, conv0_ref, h0_ref,
                y_ref, convo_ref, ho_ref,
                cbuf, hbuf, xs_s, bd_s, cd_s, dt_s, dta_s, y_s, *, nb, tb, L):
    c = y_ref.shape[-1]
    nc = tb // L
    npair = c // PAIR
    nconv = cbuf.shape[-1]

    @pl.when(_first_step())
    def _():
        cbuf[:, CONV_PAD - (CONV_W - 1):CONV_PAD, :] = conv0_ref[...]
        hbuf[...] = h0_ref[...]

    x = x_ref[...].reshape(nb * tb, x_ref.shape[-1])
    p = _dot(_rms(x, g_ref[...]), w_ref[...])
    xbc = p[:, :nconv]
    dtr = p[:, nconv:nconv + c]
    z = p[:, nconv + c:]
    conv = jnp.concatenate(
        [_causal_conv(cbuf, b, xbc[b * tb:(b + 1) * tb], tb, cw_ref, cb_ref[...]) for b in range(nb)],
        axis=0)
    conv = _silu(conv)
    xs = conv[:, :c]
    nbc = (nconv - c) // 2
    dt = _softplus(dtr + dtb_ref[...])
    xs_s[...] = xs
    bd_s[...] = conv[:, c:c + nbc]
    cd_s[...] = conv[:, c + nbc:]
    dt_s[...] = dt
    dta_s[...] = dt * (-jnp.exp(alog_ref[...]))

    m0, m1 = _head_masks()
    bdm = _bd_mask()
    _, incl, _ = _cat_masks(L)
    tri = tri_ref[...]
    pick = pick_ref[...]

    def cat_cols(x):
        if 2 * L == PAIR:
            return x
        return jnp.concatenate([x[:, 0:L], x[:, HEAD:HEAD + L]], axis=1)

    grp = _group_size(nb * nc)
    lanes = lambda j: slice(j * PAIR, (j + 1) * PAIR)

    def run(i, gs):
        per = {}
        chains = [(g, j) for g in gs for j in range(npair)]
        for g in gs:
            it = i * grp + g
            rows = pl.ds(pl.multiple_of(it * L, L), L)
            acs = _dot01(tri, dta_s[rows, :])
            acs_last = acs[L - 1:L, :]
            dt_c = dt_s[rows, :]
            per[g] = dict(b=it // nc, rows=rows, acs=acs, dt=dt_c, xs=xs_s[rows, :], e_acs=jnp.exp(acs),
                          dl=jnp.exp(acs_last - acs) * dt_c, tot=jnp.exp(acs_last),
                          bd=bd_s[rows, :], cd=cd_s[rows, :])
        part = lambda key: [per[g][key][:, lanes(j)] for g, j in chains]
        group_part = lambda key: [per[g][key][:, lanes(j // 2)] for g, j in chains]
        acs_p, dt_p, xs_p, dl_p = part("acs"), part("dt"), part("xs"), part("dl")
        bd_p, cd_p = group_part("bd"), group_part("cd")
        picked = [_dot01_nt(pick, jnp.concatenate([a, d_], axis=0)) for a, d_ in zip(acs_p, dt_p)]
        cb = [_dot_nt(c_, _stack_heads(b_, m0, m1)) for c_, b_ in zip(cd_p, bd_p)]
        upd = [_dot_tn(x_ * d_, b_) for x_, d_, b_ in zip(xs_p, dl_p, bd_p)]
        scores = []
        for n in range(len(chains)):
            acs_row = jnp.concatenate([picked[n][0:1, 0:L], picked[n][1:2, 0:L]], axis=1)
            dt_row = jnp.concatenate([picked[n][0:1, L:2 * L], picked[n][1:2, L:2 * L]], axis=1)
            decay = jnp.where(incl, jnp.exp(cat_cols(acs_p[n]) - acs_row), 0.0)
            scores.append(cb[n] * decay * dt_row)
        yx = [_dot(s_, _stack_heads(x_, m0, m1)) for s_, x_ in zip(scores, xs_p)]
        for g in gs:
            b, rows = per[g]["b"], per[g]["rows"]
            ns = [n for n, (gg, _) in enumerate(chains) if gg == g]
            h = [hbuf[b, j] for j in range(npair)]
            chg = [_dot_nt(cd_p[n], h[j]) for j, n in enumerate(ns)]
            for j, n in enumerate(ns):
                y_s[rows, lanes(j)] = yx[n] + per[g]["e_acs"][:, lanes(j)] * chg[j]
                hbuf[b, j] = jnp.where(bdm, h[j] * per[g]["tot"][:, lanes(j)] + upd[n], 0.0)

    def group(i, carry):
        for gs in _halves(grp):
            run(i, gs)
        return carry

    lax.fori_loop(0, nb * nc // grp, group, 0)

    yc = y_s[...] + d_ref[...] * xs
    y_ref[...] = _rms(yc * _silu(z), ng_ref[...]).reshape(nb, tb, c)

    @pl.when(_last_step())
    def _():
        convo_ref[...] = cbuf[:, CONV_PAD - (CONV_W - 1):CONV_PAD, :]
        ho_ref[...] = hbuf[...]


def _hgrn_kernel(x_ref, g_ref, w_ref, lbl_ref, ng_ref, tri_ref, lmask_ref, seg_ref, s0_ref,
                 y_ref, so_ref,
                 sbuf, q_s, k_s, v_s, lf_s, y_s, *, nb, tb, L, layer):
    c = y_ref.shape[-1]
    nc = tb // L
    npair = c // PAIR
    nlev = lmask_ref.shape[0]

    @pl.when(_first_step())
    def _():
        sbuf[...] = s0_ref[...]

    x = x_ref[...].reshape(nb * tb, x_ref.shape[-1])
    p = _dot(_rms(x, g_ref[...]), w_ref[...])
    qd = p[:, 0:c]
    fd = p[:, c:2 * c]
    v = p[:, 2 * c:3 * c]
    gd = p[:, 3 * c:]
    logits = lbl_ref[...]
    ex = jnp.exp(logits - jnp.max(logits, axis=0, keepdims=True))
    sm = ex / jnp.sum(ex, axis=0, keepdims=True)
    lb = jnp.zeros_like(sm[0:1])
    for i in range(1, layer + 1):
        lb = lb + sm[i:i + 1]
    log_lb = jnp.log(lb)
    b2 = jnp.log1p(-lb) - _softplus(-fd)
    logf = jnp.maximum(log_lb, b2) + _log1pexp_neg_abs(log_lb - b2)
    q = _silu(qd)
    kx = (1.0 - lb) * _sigmoid(-fd)
    q_s[...] = q
    k_s[...] = kx
    v_s[...] = v
    lf_s[...] = logf
    diag = _segsum(q * kx, seg_ref) * v

    m0, m1 = _head_masks()
    bdm = _bd_mask()
    tri = tri_ref[...]
    level = [lmask_ref[m] > 0.5 for m in range(nlev)]

    grp = _group_size(nb * nc)
    lanes = lambda j: slice(j * PAIR, (j + 1) * PAIR)

    def run(i, gs):
        per = {}
        chains = [(g, j) for g in gs for j in range(npair)]
        for g in gs:
            it = i * grp + g
            rows = pl.ds(pl.multiple_of(it * L, L), L)
            bc = _dot01(tri, lf_s[rows, :])
            b_last = bc[L - 1:L, :]
            q_c = q_s[rows, :]
            k_c = k_s[rows, :]
            e_lev = [jnp.exp(-jnp.abs(bc - _mid_rows(bc, m + 1))) for m in range(nlev)]
            per[g] = dict(b=it // nc, rows=rows, v=v_s[rows, :],
                          qe=q_c * jnp.exp(bc), kl=k_c * jnp.exp(b_last - bc), tot=jnp.exp(b_last),
                          qn=[q_c * e for e in e_lev], kn=[k_c * e for e in e_lev])
        part = lambda key: [per[g][key][:, lanes(j)] for g, j in chains]
        vp, qe_p, kl_p = part("v"), part("qe"), part("kl")
        att = [jnp.zeros((L, 2 * L), F32) for _ in chains]
        for m in range(nlev):
            lev = [_dot_nt(per[g]["qn"][m][:, lanes(j)], _stack_heads(per[g]["kn"][m][:, lanes(j)], m0, m1))
                   for g, j in chains]
            att = [jnp.where(level[m], t, a) for a, t in zip(att, lev)]
        yv = [_dot(a, _stack_heads(v_, m0, m1)) for a, v_ in zip(att, vp)]
        upd = [_dot_tn(v_, k_) for v_, k_ in zip(vp, kl_p)]
        for g in gs:
            b, rows = per[g]["b"], per[g]["rows"]
            ns = [n for n, (gg, _) in enumerate(chains) if gg == g]
            st = [sbuf[b, j] for j in range(npair)]
            ys = [_dot_nt(qe_p[n], st[j]) for j, n in enumerate(ns)]
            for j, n in enumerate(ns):
                y_s[rows, lanes(j)] = ys[j] + yv[n]
                sbuf[b, j] = jnp.where(bdm, st[j] * per[g]["tot"][:, lanes(j)] + upd[n], 0.0)

    def group(i, carry):
        for gs in _halves(grp):
            run(i, gs)
        return carry

    lax.fori_loop(0, nb * nc // grp, group, 0)

    o = y_s[...] + diag
    ms = _segsum(o * o, seg_ref) * (1.0 / HEAD)
    yd = o * lax.rsqrt(ms + NORM_EPS) * ng_ref[...]
    y_ref[...] = (yd * _silu(gd)).reshape(nb, tb, c)

    @pl.when(_last_step())
    def _():
        so_ref[...] = sbuf[...]


def _post_kernel(ya_ref, yb_ref, yc_ref, yd_ref, x_ref, wo_ref, gpost_ref, gprex_ref, wq_ref,
                 mk_ref, mv_ref, wox_ref, gpostx_ref, o_ref, *, nb, tb, heads):
    d = x_ref.shape[-1]
    c = ya_ref.shape[-1]
    hd = d // heads
    rows = nb * tb
    y = None
    for i, ref in enumerate((ya_ref, yb_ref, yc_ref, yd_ref)):
        t = _dot(ref[...].reshape(rows, c), wo_ref[i * c:(i + 1) * c, :])
        y = t if y is None else y + t
    x1 = x_ref[...].reshape(rows, d) + _rms(y, gpost_ref[...])
    q = _dot(_rms(x1, gprex_ref[...]), wq_ref[...])
    scale = hd ** -0.5
    pairs = [(b, h) for b in range(nb) for h in range(heads)]
    cols = lambda h: slice(h * hd, (h + 1) * hd)
    scores = [_dot_nt(q[b * tb:(b + 1) * tb, cols(h)], mk_ref[b, :, cols(h)]) * scale for b, h in pairs]
    probs = []
    for s in scores:
        e = jnp.exp(s - jnp.max(s, axis=-1, keepdims=True))
        probs.append(e * (1.0 / jnp.sum(e, axis=-1, keepdims=True)))
    outs = [_dot(pr, mv_ref[b, :, cols(h)]) for pr, (b, h) in zip(probs, pairs)]
    o = jnp.concatenate(
        [jnp.concatenate(outs[b * heads:(b + 1) * heads], axis=1) for b in range(nb)], axis=0)
    x2 = x1 + _rms(_dot(o, wox_ref[...]), gpostx_ref[...])
    o_ref[...] = x2.reshape(nb, tb, d)


def _memkv_kernel(m_ref, g_ref, wk_ref, wv_ref, k_ref, v_ref):
    m = _rms(m_ref[0], g_ref[...])
    k_ref[0] = _dot(m, wk_ref[...])
    v_ref[0] = _dot(m, wv_ref[...])


def _full(shape):
    nd = len(shape)
    return pl.BlockSpec(shape, lambda b, t: (0,) * nd)


def _per_batch(shape_tail, nb):
    nd = len(shape_tail)
    return pl.BlockSpec((nb,) + shape_tail, lambda b, t: (b,) + (0,) * nd)


def _state_in(shape_tail, nb, layer):
    nd = len(shape_tail)
    return pl.BlockSpec((None, nb) + shape_tail, lambda b, t: (layer, b) + (0,) * nd)


def _tokens(nb, tb, width):
    return pl.BlockSpec((nb, tb, width), lambda b, t: (b, t, 0))


def _params():
    return pltpu.CompilerParams(dimension_semantics=("arbitrary", "arbitrary"),
                                vmem_limit_bytes=VMEM_LIMIT)


def _call(kern, name, grid, in_arrays, in_specs, out_shapes, out_specs, scratch):
    return pl.pallas_call(
        kern, name=name, grid=grid, in_specs=in_specs, out_specs=out_specs,
        out_shape=out_shapes, scratch_shapes=scratch, compiler_params=_params())(*in_arrays)


def _row(v):
    return v.reshape(1, -1).astype(F32)


def _lru_call(x, lp, conv0, h0, layer, nb, tb):
    B, T, D = x.shape
    c = lp["lru_cw"].shape[-1]
    grid = (B // nb, T // tb)
    ins = [x, lp["g_pre"], lp["w_a"], lp["lru_cw"], lp["lru_cb"], lp["lru_wg"], lp["lru_bg"], lp["lru_lam"],
           conv0, h0]
    specs = [_tokens(nb, tb, D)] + [_full(a.shape) for a in ins[1:8]] + [
        _state_in((CONV_W - 1, c), nb, layer), _state_in((1, c), nb, layer)]
    outs = [jax.ShapeDtypeStruct((B, T, c), F32), jax.ShapeDtypeStruct((B, CONV_W - 1, c), F32),
            jax.ShapeDtypeStruct((B, 1, c), F32)]
    ospecs = [_tokens(nb, tb, c), _per_batch((CONV_W - 1, c), nb), _per_batch((1, c), nb)]
    scratch = [pltpu.VMEM((nb, tb + CONV_PAD, c), F32), pltpu.VMEM((nb, 1, c), F32)]
    return _call(functools.partial(_lru_kernel, nb=nb, tb=tb), "mix_lru", grid, ins, specs, outs, ospecs, scratch)


def _rwkv_call(x, lp, consts, shift0, s0, layer, nb, tb, L):
    B, T, D = x.shape
    c = lp["rw_w0"].shape[-1]
    nsh = shift0.shape[-1]
    npair = c // PAIR
    grid = (B // nb, T // tb)
    ins = [x, lp["g_pre"], lp["w_b"], lp["rw_mu"], lp["rw_w0"], lp["rw_a0"], lp["rw_wup"], lp["rw_kk"],
           lp["rw_ka"], lp["rw_rk"], lp["rw_lng"], lp["rw_lnb"], consts["seg"], consts["tri"], shift0, s0]
    specs = [_tokens(nb, tb, D)] + [_full(a.shape) for a in ins[1:14]] + [
        _state_in((1, nsh), nb, layer), _state_in((npair, PAIR, PAIR), nb, layer)]
    outs = [jax.ShapeDtypeStruct((B, T, c), F32), jax.ShapeDtypeStruct((B, 1, nsh), F32),
            jax.ShapeDtypeStruct((B, npair, PAIR, PAIR), F32)]
    ospecs = [_tokens(nb, tb, c), _per_batch((1, nsh), nb), _per_batch((npair, PAIR, PAIR), nb)]
    rows = nb * tb
    scratch = [pltpu.VMEM((nb, 1, nsh), F32), pltpu.VMEM((nb, npair, PAIR, PAIR), F32)] + [
        pltpu.VMEM((rows, c), F32) for _ in range(7)]
    return _call(functools.partial(_rwkv_kernel, nb=nb, tb=tb, L=L), "mix_rwkv", grid, ins, specs, outs,
                 ospecs, scratch)


def _ssd_call(x, lp, consts, conv0, h0, layer, nb, tb, L):
    B, T, D = x.shape
    c = lp["ssd_dtb"].shape[-1]
    nconv = lp["ssd_cw"].shape[-1]
    nbc = (nconv - c) // 2
    npair = c // PAIR
    grid = (B // nb, T // tb)
    ins = [x, lp["g_pre"], lp["w_c"], lp["ssd_cw"], lp["ssd_cb"], lp["ssd_dtb"], lp["ssd_alog"], lp["ssd_d"],
           lp["ssd_ng"], consts["tri"], consts["pick"], conv0, h0]
    specs = [_tokens(nb, tb, D)] + [_full(a.shape) for a in ins[1:11]] + [
        _state_in((CONV_W - 1, nconv), nb, layer), _state_in((npair, PAIR, PAIR), nb, layer)]
    outs = [jax.ShapeDtypeStruct((B, T, c), F32), jax.ShapeDtypeStruct((B, CONV_W - 1, nconv), F32),
            jax.ShapeDtypeStruct((B, npair, PAIR, PAIR), F32)]
    ospecs = [_tokens(nb, tb, c), _per_batch((CONV_W - 1, nconv), nb), _per_batch((npair, PAIR, PAIR), nb)]
    rows = nb * tb
    scratch = [pltpu.VMEM((nb, tb + CONV_PAD, nconv), F32), pltpu.VMEM((nb, npair, PAIR, PAIR), F32),
               pltpu.VMEM((rows, c), F32), pltpu.VMEM((rows, nbc), F32), pltpu.VMEM((rows, nbc), F32),
               pltpu.VMEM((rows, c), F32), pltpu.VMEM((rows, c), F32), pltpu.VMEM((rows, c), F32)]
    return _call(functools.partial(_ssd_kernel, nb=nb, tb=tb, L=L), "mix_ssd", grid, ins, specs, outs,
                 ospecs, scratch)


def _hgrn_call(x, lp, consts, s0, layer, nb, tb, L):
    B, T, D = x.shape
    c = lp["hg_ng"].shape[-1]
    npair = c // PAIR
    grid = (B // nb, T // tb)
    ins = [x, lp["g_pre"], lp["w_d"], lp["hg_lbl"], lp["hg_ng"], consts["tri"], consts["lmask"], consts["seg"], s0]
    specs = [_tokens(nb, tb, D)] + [_full(a.shape) for a in ins[1:8]] + [_state_in((npair, PAIR, PAIR), nb, layer)]
    outs = [jax.ShapeDtypeStruct((B, T, c), F32), jax.ShapeDtypeStruct((B, npair, PAIR, PAIR), F32)]
    ospecs = [_tokens(nb, tb, c), _per_batch((npair, PAIR, PAIR), nb)]
    rows = nb * tb
    scratch = [pltpu.VMEM((nb, npair, PAIR, PAIR), F32)] + [pltpu.VMEM((rows, c), F32) for _ in range(5)]
    return _call(functools.partial(_hgrn_kernel, nb=nb, tb=tb, L=L, layer=layer), "mix_hgrn", grid, ins, specs,
                 outs, ospecs, scratch)


def _post_call(ys, x, lp, mk, mv, layer, nb, tb, heads):
    B, T, D = x.shape
    c = ys[0].shape[-1]
    M = mk.shape[2]
    grid = (B // nb, T // tb)
    ins = list(ys) + [x, lp["w_out"], lp["g_post"], lp["g_pre_x"], lp["w_q"], mk, mv, lp["w_o"], lp["g_post_x"]]
    kv_spec = pl.BlockSpec((None, nb, M, D), lambda b, t: (layer, b, 0, 0))
    specs = [_tokens(nb, tb, c)] * 4 + [_tokens(nb, tb, D)] + [_full(a.shape) for a in ins[5:9]] + [
        kv_spec, kv_spec] + [_full(a.shape) for a in ins[11:13]]
    return _call(functools.partial(_post_kernel, nb=nb, tb=tb, heads=heads), "post_attn", grid, ins, specs,
                 jax.ShapeDtypeStruct((B, T, D), F32), _tokens(nb, tb, D), [])


def _memkv_call(mem, g, wk, wv):
    B, M, D = mem.shape
    depth = wk.shape[0]
    wspec = pl.BlockSpec((None, D, D), lambda l, b: (l, 0, 0))
    ospec = pl.BlockSpec((None, 1, M, D), lambda l, b: (l, b, 0, 0))
    return pl.pallas_call(
        _memkv_kernel, name="mem_kv", grid=(depth, B),
        in_specs=[pl.BlockSpec((1, M, D), lambda l, b: (b, 0, 0)),
                  pl.BlockSpec((None, 1, D), lambda l, b: (l, 0, 0)), wspec, wspec],
        out_specs=[ospec, ospec],
        out_shape=[jax.ShapeDtypeStruct((depth, B, M, D), F32)] * 2,
        compiler_params=_params())(mem, g, wk, wv)


def _chunk_consts(L, c):
    r = np.arange(L)[:, None]
    j = np.arange(L)[None, :]
    tri = (j <= r).astype(np.float32)
    masks = []
    col = np.arange(2 * L)[None, :]
    s = col % L
    m = 1
    while (1 << m) <= L:
        size, half = 1 << m, 1 << (m - 1)
        masks.append(((r // size == s // size) & (r % size >= half) & (s % size < half)).astype(np.float32))
        m += 1
    seg = (np.arange(c)[:, None] // HEAD == np.arange(c)[None, :] // HEAD).astype(np.float32)
    pick = np.zeros((SUBLANES, PAIR), np.float32)
    pick[0, 0] = 1.0
    pick[1, HEAD] = 1.0
    return {
        "tri": jnp.asarray(tri, BF16),
        "lmask": jnp.asarray(np.stack(masks), F32),
        "seg": jnp.asarray(seg, BF16),
        "pick": jnp.asarray(pick, BF16),
    }


def _to_pairs(s):
    lead, (H, a, b) = s.shape[:-3], s.shape[-3:]
    s = s.reshape(lead + (H // 2, 2, a, b))
    z = jnp.zeros_like(s[..., 0, :, :])
    top = jnp.concatenate([s[..., 0, :, :], z], axis=-1)
    bot = jnp.concatenate([z, s[..., 1, :, :]], axis=-1)
    return jnp.concatenate([top, bot], axis=-2)


def _from_pairs(s):
    lead, P = s.shape[:-3], s.shape[-3]
    both = jnp.stack([s[..., :HEAD, :HEAD], s[..., HEAD:, HEAD:]], axis=-3)
    return both.reshape(lead + (2 * P, HEAD, HEAD))


def _expand_bc(t, c, n):
    xs, bm, cm = t[..., :c], t[..., c:c + 2 * n], t[..., c + 2 * n:]
    dup = lambda u: jnp.concatenate([u[..., :n], u[..., :n], u[..., n:], u[..., n:]], axis=-1)
    return jnp.concatenate([xs, dup(bm), dup(cm)], axis=-1)


def _shrink_bc(t, c, n):
    pick = lambda u: jnp.concatenate([u[..., :n], u[..., 2 * n:3 * n]], axis=-1)
    return jnp.concatenate([t[..., :c], pick(t[..., c:c + 4 * n]), pick(t[..., c + 4 * n:])], axis=-1)


def _blockdiag(w):
    n, d, e = w.shape
    eye = jnp.eye(n, dtype=w.dtype)
    return (eye[:, None, :, None] * w[:, :, None, :]).reshape(n * d, n * e)


def kernel(x_prompt, x_sample, state_lru_conv, state_lru_h, state_rwkv_shift, state_rwkv_wkv, state_ssd_conv, state_ssd_h, state_hgrn_s, cache_mem_k, cache_mem_v, mem_prompt, g_pre, g_post, g_pre_x, g_post_x, w_in, w_out, lru_conv_w, lru_conv_b, lru_w_r, lru_b_r, lru_w_i, lru_b_i, lru_lambda, rwkv_mu, rwkv_w0, rwkv_w_up, rwkv_a0, rwkv_a_up, rwkv_k_k, rwkv_k_a, rwkv_r_k, rwkv_ln_g, rwkv_ln_b, ssd_conv_w, ssd_conv_b, ssd_dt_bias, ssd_a_log, ssd_d, ssd_norm_g, hgrn_lb_logits, hgrn_norm_g, mem_g, mem_w_q, mem_w_k, mem_w_v, mem_w_o):
    depth = w_in.shape[0]
    D = x_prompt.shape[-1]
    c = lru_conv_w.shape[-1]
    nsh = rwkv_mu.shape[-1]
    rank = (nsh - 3 * c) // 2
    ssd_heads = ssd_dt_bias.shape[-1]
    nstate = (ssd_conv_w.shape[-1] - c) // 4
    heads_x = cache_mem_k.shape[-2]
    mem_len = mem_prompt.shape[1]

    o_b = 2 * c
    o_c = o_b + nsh + c
    o_dt = o_c + c + 4 * nstate
    o_z = o_dt + ssd_heads
    o_d = o_z + c

    layers = []
    for l in range(depth):
        w = w_in[l]
        zero = jnp.zeros((rank, c), F32)
        wup = jnp.concatenate([jnp.concatenate([rwkv_w_up[l], zero], axis=1),
                               jnp.concatenate([zero, rwkv_a_up[l]], axis=1)], axis=0)
        rep = lambda v: jnp.repeat(v, c // ssd_heads, axis=-1)
        w_c = jnp.concatenate([_expand_bc(w[:, o_c:o_dt], c, nstate), rep(w[:, o_dt:o_z]), w[:, o_z:o_d]], axis=1)
        layers.append({
            "g_pre": _row(g_pre[l]), "g_post": _row(g_post[l]), "g_pre_x": _row(g_pre_x[l]),
            "g_post_x": _row(g_post_x[l]),
            "w_a": w[:, :o_b].astype(BF16), "w_b": w[:, o_b:o_c].astype(BF16), "w_c": w_c.astype(BF16),
            "w_d": w[:, o_d:].astype(BF16),
            "w_out": w_out[l].astype(BF16), "w_q": mem_w_q[l].astype(BF16), "w_o": mem_w_o[l].astype(BF16),
            "lru_cw": lru_conv_w[l], "lru_cb": _row(lru_conv_b[l]),
            "lru_wg": jnp.concatenate([_blockdiag(lru_w_r[l]), _blockdiag(lru_w_i[l])], axis=1).astype(BF16),
            "lru_bg": _row(jnp.concatenate([lru_b_r[l], lru_b_i[l]])), "lru_lam": _row(lru_lambda[l]),
            "rw_mu": _row(rwkv_mu[l]), "rw_w0": _row(rwkv_w0[l]), "rw_a0": _row(rwkv_a0[l]),
            "rw_wup": wup.astype(BF16), "rw_kk": _row(rwkv_k_k[l]), "rw_ka": _row(rwkv_k_a[l]),
            "rw_rk": _row(rwkv_r_k[l]), "rw_lng": _row(rwkv_ln_g[l]), "rw_lnb": _row(rwkv_ln_b[l]),
            "ssd_cw": _expand_bc(ssd_conv_w[l], c, nstate), "ssd_cb": _row(_expand_bc(ssd_conv_b[l], c, nstate)),
            "ssd_dtb": _row(rep(ssd_dt_bias[l])), "ssd_alog": _row(rep(ssd_a_log[l])), "ssd_d": _row(rep(ssd_d[l])),
            "ssd_ng": _row(ssd_norm_g[l]),
            "hg_lbl": hgrn_lb_logits.astype(F32), "hg_ng": _row(hgrn_norm_g[l]),
        })

    def run(x, mk, mv, conv_a, h_a, shift_b, wkv_b, conv_c, h_c, s_d, nb, tb, L):
        consts = _chunk_consts(L, c)
        B = x.shape[0]
        h_a = h_a.reshape(depth, B, 1, c)
        shift_b = shift_b.reshape(depth, B, 1, nsh)
        wkv_b = _to_pairs(wkv_b)
        conv_c = _expand_bc(conv_c, c, nstate)
        h_c = _to_pairs(h_c)
        s_d = _to_pairs(jnp.swapaxes(s_d, -1, -2))
        acc = [[] for _ in range(7)]
        for l in range(depth):
            lp = layers[l]
            ya, nca, nha = _lru_call(x, lp, conv_a, h_a, l, nb, tb)
            yb, nsb, nwb = _rwkv_call(x, lp, consts, shift_b, wkv_b, l, nb, tb, L)
            yc, ncc, nhc = _ssd_call(x, lp, consts, conv_c, h_c, l, nb, tb, L)
            yd, nsd = _hgrn_call(x, lp, consts, s_d, l, nb, tb, L)
            x = _post_call((ya, yb, yc, yd), x, lp, mk, mv, l, nb, tb, heads_x)
            for lst, val in zip(acc, (nca, nha, nsb, nwb, ncc, nhc, nsd)):
                lst.append(val)
        nca, nha, nsb, nwb, ncc, nhc, nsd = (jnp.stack(v) for v in acc)
        return (x, nca, nha.reshape(depth, B, c), nsb.reshape(depth, B, nsh), _from_pairs(nwb),
                _shrink_bc(ncc, c, nstate), _from_pairs(nhc), jnp.swapaxes(_from_pairs(nsd), -1, -2))

    Bp, Tp = x_prompt.shape[:2]
    Bs, Ts = x_sample.shape[:2]
    k_p, v_p = _memkv_call(mem_prompt, mem_g.reshape(depth, 1, D).astype(F32), mem_w_k.astype(BF16),
                           mem_w_v.astype(BF16))
    kv_shape = (depth, Bp, mem_len, heads_x, D // heads_x)
    mem_k_p = k_p.reshape(kv_shape)
    mem_v_p = v_p.reshape(kv_shape)
    zeros = lambda *s: jnp.zeros((depth, Bp) + s, F32)
    Lp = CHUNK if Tp % CHUNK == 0 else Tp
    Ls = CHUNK if Ts % CHUNK == 0 else Ts
    tb_p = PROMPT_CHUNKS * Lp if Tp % (PROMPT_CHUNKS * Lp) == 0 else Lp
    nb_s = SAMPLE_BATCH if Bs % SAMPLE_BATCH == 0 else 1
    (y_prompt, lru_conv_p, lru_h_p, rwkv_shift_p, rwkv_wkv_p, ssd_conv_p, ssd_h_p, hgrn_s_p) = run(
        x_prompt, k_p, v_p,
        zeros(CONV_W - 1, c), zeros(c), zeros(nsh), zeros(c // HEAD, HEAD, HEAD),
        zeros(CONV_W - 1, c + 4 * nstate), zeros(ssd_heads, HEAD, nstate), zeros(c // HEAD, HEAD, HEAD),
        1, tb_p, Lp)
    (y_sample, lru_conv_s, lru_h_s, rwkv_shift_s, rwkv_wkv_s, ssd_conv_s, ssd_h_s, hgrn_s_s) = run(
        x_sample, cache_mem_k.reshape(depth, Bs, mem_len, D).astype(BF16),
        cache_mem_v.reshape(depth, Bs, mem_len, D).astype(BF16),
        state_lru_conv, state_lru_h, state_rwkv_shift, state_rwkv_wkv,
        state_ssd_conv, state_ssd_h, state_hgrn_s, nb_s, Ts, Ls)
    return (y_prompt, y_sample, lru_conv_p, lru_conv_s, lru_h_p, lru_h_s, rwkv_shift_p, rwkv_shift_s,
            rwkv_wkv_p, rwkv_wkv_s, ssd_conv_p, ssd_conv_s, ssd_h_p, ssd_h_s, hgrn_s_p, hgrn_s_s,
            mem_k_p, mem_v_p)
```
